```python
import math
import jax, jax.numpy as jnp
from jax import lax
import numpy as np

D_MODEL = 2048
BATCH = 2
SEQ = 8192
DEPTH = 1

MEM_LEN = 256
DIFF_HEADS = D_MODEL // 256
DIFF_QK_DIM = 64
DIFF_V_DIM = 2 * DIFF_QK_DIM
SB_HEADS = D_MODEL // 256
SB_DIM = 128
MEM_HEADS = 4
MEM_DIM = D_MODEL // 8
N_BRANCH = 3
N_BUCKETS = 32
MAX_DISTANCE = 128
Q_BLOCK = 128
N_EXPERTS = 32
TOP_K = 4
EXPERT_FF = D_MODEL
SWIGLU_LIMIT = 7.0
SWIGLU_ALPHA = 1.702
MOE_BLOCK = 256
NORM_EPS = 1e-6
NEG_INF = -1e30

DIFF_QK_W = DIFF_HEADS * 2 * DIFF_QK_DIM
DIFF_V_W = DIFF_HEADS * DIFF_V_DIM
SB_W = SB_HEADS * SB_DIM
MEM_W = MEM_HEADS * MEM_DIM
IN_W = 2 * DIFF_QK_W + DIFF_V_W + 3 * SB_W + MEM_W + N_BRANCH * D_MODEL

kernel_name = "hybrid_diffattn_stickbreak_memxattn_moe"


def rms_norm(x, g):
    xf = x.astype(jnp.float32)
    y = xf * lax.rsqrt(jnp.mean(xf * xf, axis=-1, keepdims=True) + NORM_EPS)
    return (y * g.astype(jnp.float32)).astype(x.dtype)


def t5_bucket(n):
    max_exact = N_BUCKETS // 2
    nf = jnp.maximum(n, 1).astype(jnp.float32)
    large = max_exact + (jnp.log(nf / max_exact) / math.log(MAX_DISTANCE / max_exact)
                         * (N_BUCKETS - max_exact)).astype(jnp.int32)
    large = jnp.minimum(large, N_BUCKETS - 1)
    return jnp.where(n < max_exact, n, large)


def diff_attention(q, k, v, rel_bias, lam, lam_init, subln_g):
    B, H, _, S, dq = q.shape
    dv = v.shape[-1]
    scale = dq ** -0.5
    kpos = jnp.arange(S)

    def block(i):
        qb = lax.dynamic_slice_in_dim(q, i * Q_BLOCK, Q_BLOCK, axis=3)
        qpos = i * Q_BLOCK + jnp.arange(Q_BLOCK)
        dist = qpos[:, None] - kpos[None, :]
        bias = rel_bias[t5_bucket(jnp.maximum(dist, 0))]
        bias = jnp.transpose(bias, (2, 0, 1)).astype(jnp.float32)[None, :, None]
        s = jnp.einsum('bhmqd,bhmkd->bhmqk', qb, k).astype(jnp.float32) * scale + bias
        s = jnp.where(dist >= 0, s, NEG_INF)
        p = jax.nn.softmax(s, axis=-1)
        w = p[:, :, 0] - lam * p[:, :, 1]
        return jnp.einsum('bhqk,bhkd->bqhd', w.astype(v.dtype), v)

    o = lax.map(block, jnp.arange(S // Q_BLOCK))
    o = jnp.transpose(o, (1, 0, 2, 3, 4)).reshape(B, S, H, dv)
    o = rms_norm(o, subln_g) * (1.0 - lam_init)
    return o.reshape(B, S, H * dv)


def stick_breaking_attention(q, k, v):
    B, H, S, d = q.shape
    scale = d ** -0.5
    kpos = jnp.arange(S)

    def block(i):
        qb = lax.dynamic_slice_in_dim(q, i * Q_BLOCK, Q_BLOCK, axis=2)
        qpos = i * Q_BLOCK + jnp.arange(Q_BLOCK)
        strict = kpos[None, :] < qpos[:, None]
        z = jnp.einsum('bhqd,bhkd->bhqk', qb, k).astype(jnp.float32) * scale
        log_keep = jnp.where(strict, -jax.nn.softplus(z), 0.0)
        after = lax.cumsum(log_keep, axis=3, reverse=True) - log_keep
        a = jnp.where(strict, jnp.exp(jax.nn.log_sigmoid(z) + after), 0.0)
        return jnp.einsum('bhqk,bhkd->bqhd', a.astype(v.dtype), v)

    o = lax.map(block, jnp.arange(S // Q_BLOCK))
    return jnp.transpose(o, (1, 0, 2, 3, 4)).reshape(B, S, H * d)


def memory_attention(q, k, v):
    B, H, S, d = q.shape
    s = jnp.einsum('bhsd,bhmd->bhsm', q, k).astype(jnp.float32) * (d ** -0.5)
    p = jax.nn.softmax(s, axis=-1)
    o = jnp.einsum('bhsm,bhmd->bshd', p.astype(v.dtype), v)
    return o.reshape(B, S, H * d)


def moe_ffn(h, w_r, b_r, w1, b1, w2, b2):
    T, D = h.shape
    logits = (h @ w_r).astype(jnp.float32) + b_r.astype(jnp.float32)
    top_v, top_i = lax.top_k(logits, TOP_K)
    gates = jax.nn.softmax(top_v, axis=-1)
    N = T * TOP_K
    e_flat = top_i.reshape(N).astype(jnp.int32)
    tok_flat = (jnp.arange(N) // TOP_K).astype(jnp.int32)
    g_flat = gates.reshape(N)
    order = jnp.argsort(e_flat)
    e_sorted = e_flat[order]
    counts = jnp.bincount(e_flat, length=N_EXPERTS).astype(jnp.int32)
    starts = jnp.cumsum(counts) - counts
    padded = ((counts + MOE_BLOCK - 1) // MOE_BLOCK) * MOE_BLOCK
    pad_end = jnp.cumsum(padded).astype(jnp.int32)
    pad_start = pad_end - padded
    dest = pad_start[e_sorted] + (jnp.arange(N, dtype=jnp.int32) - starts[e_sorted])
    n_blocks = -(-N // MOE_BLOCK) + N_EXPERTS
    P = n_blocks * MOE_BLOCK
    row_tok = jnp.full((P,), T, jnp.int32).at[dest].set(tok_flat[order])
    row_gate = jnp.zeros((P,), jnp.float32).at[dest].set(g_flat[order])
    blk_start = jnp.arange(n_blocks, dtype=jnp.int32) * MOE_BLOCK
    blk_exp = jnp.minimum(jnp.searchsorted(pad_end, blk_start, side='right'),
                          N_EXPERTS - 1).astype(jnp.int32)
    h_pad = jnp.concatenate([h, jnp.zeros((1, D), h.dtype)], axis=0)
    xs = h_pad[row_tok].reshape(n_blocks, MOE_BLOCK, D)

    def expert_block(args):
        xb, e = args
        a = xb @ w1[e] + b1[e]
        a_glu, a_lin = jnp.split(a, 2, axis=-1)
        a_glu = jnp.minimum(a_glu, SWIGLU_LIMIT)
        a_lin = jnp.clip(a_lin, -SWIGLU_LIMIT, SWIGLU_LIMIT)
        act = a_glu * jax.nn.sigmoid(SWIGLU_ALPHA * a_glu) * (a_lin + 1.0)
        return act @ w2[e] + b2[e]

    ys = lax.map(expert_block, (xs, blk_exp)).reshape(P, D)
    ys = ys.astype(jnp.float32) * row_gate[:, None]
    out = jnp.zeros((T + 1, D), jnp.float32).at[row_tok].add(ys)[:T]
    return out.astype(h.dtype)


def setup_inputs(seed: int = 0) -> dict:
    key = jax.random.key(seed)
    ks = jax.random.split(key, 26)
    f32 = jnp.float32
    L, D = DEPTH, D_MODEL

    def nrm(k, shape, scale):
        return jax.random.normal(k, shape, f32) * scale

    return {
        "x": nrm(ks[0], (BATCH, SEQ, D), 1.0),
        "mem": nrm(ks[1], (BATCH, MEM_LEN, D), 1.0),
        "g_mix": 1.0 + nrm(ks[2], (L, D), 0.01),
        "w_in": nrm(ks[3], (L, D, IN_W), D ** -0.5),
        "b_gate": nrm(ks[4], (L, N_BRANCH * D), 0.01),
        "rel_bias": nrm(ks[5], (N_BUCKETS, DIFF_HEADS), 0.1),
        "lambda_q1": nrm(ks[6], (L, DIFF_QK_DIM), 0.1),
        "lambda_k1": nrm(ks[7], (L, DIFF_QK_DIM), 0.1),
        "lambda_q2": nrm(ks[8], (L, DIFF_QK_DIM), 0.1),
        "lambda_k2": nrm(ks[9], (L, DIFF_QK_DIM), 0.1),
        "diff_subln_g": 1.0 + nrm(ks[10], (L, DIFF_V_DIM), 0.01),
        "g_mem": 1.0 + nrm(ks[11], (L, D), 0.01),
        "w_mem_kv": nrm(ks[12], (L, D, 2 * MEM_W), D ** -0.5),
        "w_br_diff": nrm(ks[13], (L, DIFF_V_W, D), DIFF_V_W ** -0.5),
        "w_br_sb": nrm(ks[14], (L, SB_W, D), SB_W ** -0.5),
        "w_br_mem": nrm(ks[15], (L, MEM_W, D), MEM_W ** -0.5),
        "w_out": nrm(ks[16], (L, D, D), D ** -0.5),
        "g_ffn": 1.0 + nrm(ks[17], (L, D), 0.01),
        "w_router": nrm(ks[18], (L, D, N_EXPERTS), D ** -0.5),
        "b_router": nrm(ks[19], (L, N_EXPERTS), 0.01),
        "w_exp1": nrm(ks[20], (L, N_EXPERTS, D, 2 * EXPERT_FF), D ** -0.5),
        "b_exp1": nrm(ks[21], (L, N_EXPERTS, 2 * EXPERT_FF), 0.01),
        "w_exp2": nrm(ks[22], (L, N_EXPERTS, EXPERT_FF, D), EXPERT_FF ** -0.5),
        "b_exp2": nrm(ks[23], (L, N_EXPERTS, D), 0.01),
        "g_final": 1.0 + nrm(ks[24], (D,), 0.01),
    }


def reference(x, mem, g_mix, w_in, b_gate, rel_bias, lambda_q1, lambda_k1, lambda_q2,
              lambda_k2, diff_subln_g, g_mem, w_mem_kv, w_br_diff, w_br_sb, w_br_mem,
              w_out, g_ffn, w_router, b_router, w_exp1, b_exp1, w_exp2, b_exp2, g_final):
    B, S, D = x.shape
    cuts = np.cumsum([DIFF_QK_W, DIFF_QK_W, DIFF_V_W, SB_W, SB_W, SB_W, MEM_W]).tolist()
    for l in range(DEPTH):
        h = rms_norm(x, g_mix[l])
        proj = h @ w_in[l]
        dq, dk, dv, sq, sk, sv, mq, gl = jnp.split(proj, cuts, axis=-1)
        dq = dq.reshape(B, S, DIFF_HEADS, 2, DIFF_QK_DIM).transpose(0, 2, 3, 1, 4)
        dk = dk.reshape(B, S, DIFF_HEADS, 2, DIFF_QK_DIM).transpose(0, 2, 3, 1, 4)
        dv = dv.reshape(B, S, DIFF_HEADS, DIFF_V_DIM).transpose(0, 2, 1, 3)
        sq = sq.reshape(B, S, SB_HEADS, SB_DIM).transpose(0, 2, 1, 3)
        sk = sk.reshape(B, S, SB_HEADS, SB_DIM).transpose(0, 2, 1, 3)
        sv = sv.reshape(B, S, SB_HEADS, SB_DIM).transpose(0, 2, 1, 3)
        mq = mq.reshape(B, S, MEM_HEADS, MEM_DIM).transpose(0, 2, 1, 3)
        gates = jax.nn.sigmoid((gl + b_gate[l]).astype(jnp.float32)).astype(x.dtype)
        g_a, g_b, g_c = jnp.split(gates, N_BRANCH, axis=-1)

        lam_init = 0.8 - 0.6 * math.exp(-0.3 * l)
        lam = (jnp.exp(jnp.sum(lambda_q1[l].astype(jnp.float32) * lambda_k1[l].astype(jnp.float32)))
               - jnp.exp(jnp.sum(lambda_q2[l].astype(jnp.float32) * lambda_k2[l].astype(jnp.float32)))
               + lam_init)
        y_a = diff_attention(dq, dk, dv, rel_bias, lam, lam_init, diff_subln_g[l])
        y_b = stick_breaking_attention(sq, sk, sv)
        kv = rms_norm(mem, g_mem[l]) @ w_mem_kv[l]
        mk, mv = jnp.split(kv, 2, axis=-1)
        M = mem.shape[1]
        mk = mk.reshape(B, M, MEM_HEADS, MEM_DIM).transpose(0, 2, 1, 3)
        mv = mv.reshape(B, M, MEM_HEADS, MEM_DIM).transpose(0, 2, 1, 3)
        y_c = memory_attention(mq, mk, mv)

        merged = (g_a * (y_a @ w_br_diff[l]) + g_b * (y_b @ w_br_sb[l])
                  + g_c * (y_c @ w_br_mem[l]))
        x = x + merged @ w_out[l]

        h2 = rms_norm(x, g_ffn[l]).reshape(B * S, D)
        y = moe_ffn(h2, w_router[l], b_router[l], w_exp1[l], b_exp1[l], w_exp2[l], b_exp2[l])
        x = x + y.reshape(B, S, D)
    return rms_norm(x, g_final)
```

```python
import functools
import math

import jax
import jax.numpy as jnp
from jax import lax
from jax.experimental import pallas as pl
from jax.experimental.pallas import tpu as pltpu

F32 = jnp.float32
BF16 = jnp.bfloat16

DIFF_HEADS = 8
DIFF_QK_DIM = 64
DIFF_V_DIM = 128
SB_HEADS = 8
SB_DIM = 128
MEM_HEADS = 4
MEM_DIM = 256
N_BUCKETS = 32
MAX_DISTANCE = 128
N_EXPERTS = 32
TOP_K = 4
SWIGLU_LIMIT = 7.0
SWIGLU_ALPHA = 1.702
NORM_EPS = 1e-6
NEG_INF = -1e30

LANES = 128
VMEM_LIMIT_BYTES = 56 * 1024 * 1024

ROW_TILE = 512
MM_TM = 1024
MM_TN = 1024
ATT_TQ = 256
ATT_TK = 256
SB_TK = 128
SB_GROUP = 4
MOE_BLK = 512
FFN1_TN = 512
FFN2_TN = 1024
ROUTER_PAD = LANES


def _cparams(sem):
    return pltpu.CompilerParams(dimension_semantics=sem, vmem_limit_bytes=VMEM_LIMIT_BYTES)


def _rmsnorm_kernel(x_ref, g_ref, o_ref):
    x = x_ref[...].astype(F32)
    ms = jnp.mean(x * x, axis=-1, keepdims=True)
    o_ref[...] = (x * lax.rsqrt(ms + NORM_EPS) * g_ref[...]).astype(o_ref.dtype)


def _rmsnorm(x, g, out_dtype, tm):
    m, d = x.shape
    return pl.pallas_call(
        _rmsnorm_kernel,
        grid=(m // tm,),
        in_specs=[pl.BlockSpec((tm, d), lambda i: (i, 0)),
                  pl.BlockSpec((1, d), lambda i: (0, 0))],
        out_specs=pl.BlockSpec((tm, d), lambda i: (i, 0)),
        out_shape=jax.ShapeDtypeStruct((m, d), out_dtype),
        compiler_params=_cparams(("arbitrary",)),
        name="rmsnorm",
    )(x, g.reshape(1, d).astype(F32))


def _mm_kernel(a_ref, w_ref, o_ref):
    o_ref[...] = jnp.dot(a_ref[...], w_ref[...], preferred_element_type=F32).astype(o_ref.dtype)


def _mm_sigmoid_kernel(a_ref, w_ref, b_ref, o_ref):
    acc = jnp.dot(a_ref[...], w_ref[...], preferred_element_type=F32)
    o_ref[...] = jax.nn.sigmoid(acc + b_ref[...]).astype(o_ref.dtype)


def _matmul(a, w, col_off, n_out, bias=None, tm=MM_TM, tn=MM_TN, name="matmul"):
    m, k = a.shape
    tm = min(tm, m)
    off = col_off // tn
    in_specs = [pl.BlockSpec((tm, k), lambda j, i: (i, 0)),
                pl.BlockSpec((k, tn), lambda j, i: (0, j + off))]
    args = [a, w]
    body = _mm_kernel
    if bias is not None:
        in_specs.append(pl.BlockSpec((1, tn), lambda j, i: (0, j)))
        args.append(bias.reshape(1, n_out).astype(F32))
        body = _mm_sigmoid_kernel
    return pl.pallas_call(
        body,
        grid=(n_out // tn, m // tm),
        in_specs=in_specs,
        out_specs=pl.BlockSpec((tm, tn), lambda j, i: (i, j)),
        out_shape=jax.ShapeDtypeStruct((m, n_out), BF16),
        compiler_params=_cparams(("arbitrary", "arbitrary")),
        name=name,
    )(*args)


def _t5_bias_tile(rel_ref, h, delta, tk, tq):
    r = lax.broadcasted_iota(jnp.int32, (tk, tq), 0)
    c = lax.broadcasted_iota(jnp.int32, (tk, tq), 1)
    n = jnp.maximum(c - r + delta * tk, 0)
    max_exact = N_BUCKETS // 2
    nf = jnp.maximum(n, 1).astype(F32)
    large = max_exact + (jnp.log(nf / max_exact) / math.log(MAX_DISTANCE / max_exact)
                         * (N_BUCKETS - max_exact)).astype(jnp.int32)
    large = jnp.minimum(large, N_BUCKETS - 1)
    bucket = jnp.where(n < max_exact, n, large)
    far = rel_ref[(N_BUCKETS - 1) * DIFF_HEADS + h]
    out = jnp.zeros((tk, tq), F32)
    for j in range(N_BUCKETS - 1):
        out = jnp.where(bucket == j, rel_ref[j * DIFF_HEADS + h] - far, out)
    return out


def _diff_attn_kernel(rel_ref, lam_ref, qT_ref, k_ref, vT_ref, g_ref, o_ref,
                      bias_sc, m_sc, l_sc, acc_sc, *, lam_init):
    h = pl.program_id(1)
    qi = pl.program_id(2)
    tq = qT_ref.shape[-1]
    tk = vT_ref.shape[-1]

    @pl.when(qi == 0)
    def _():
        bias_sc[0] = _t5_bias_tile(rel_ref, h, 0, tk, tq)
        bias_sc[1] = _t5_bias_tile(rel_ref, h, 1, tk, tq)

    qs = qT_ref[0, 0] * jnp.asarray(DIFF_QK_DIM ** -0.5, BF16)
    row = lax.broadcasted_iota(jnp.int32, qs.shape, 0)
    zero = jnp.zeros_like(qs)
    qmaps = (jnp.where(row < DIFF_QK_DIM, qs, zero), jnp.where(row >= DIFF_QK_DIM, qs, zero))

    m_sc[...] = jnp.full(m_sc.shape, NEG_INF, F32)
    l_sc[...] = jnp.zeros(l_sc.shape, F32)
    acc_sc[...] = jnp.zeros(acc_sc.shape, F32)

    def tile(k_tile, v_chunks, bias, mask):
        for mi in range(2):
            s = jnp.dot(k_tile, qmaps[mi], preferred_element_type=F32)
            if bias is not None:
                s = s + bias
            if mask is not None:
                s = jnp.where(mask, s, NEG_INF)
            m_old = m_sc[mi]
            m_new = jnp.maximum(m_old, jnp.max(s, axis=0, keepdims=True))
            alpha = jnp.exp(m_old - m_new)
            p = jnp.exp(s - m_new)
            l_sc[mi] = alpha * l_sc[mi] + jnp.sum(p, axis=0, keepdims=True)
            pb = p.astype(BF16)
            pv = jnp.dot(v_chunks[0], pb[0:tk], preferred_element_type=F32)
            for ci in range(1, len(v_chunks)):
                pv = pv + jnp.dot(v_chunks[ci], pb[ci * tk:(ci + 1) * tk], preferred_element_type=F32)
            acc_sc[mi] = alpha * acc_sc[mi] + pv
            m_sc[mi] = m_new

    n_far = jnp.maximum(qi - 1, 0)

    def far_pair(p, carry):
        base = pl.multiple_of(p * (2 * tk), 2 * tk)
        tile(k_ref[0, 0, pl.ds(base, 2 * tk), :], (vT_ref[0, 0, 2 * p], vT_ref[0, 0, 2 * p + 1]), None, None)
        return carry

    lax.fori_loop(0, n_far // 2, far_pair, 0)

    @pl.when(n_far % 2 == 1)
    def _():
        j = n_far - 1
        base = pl.multiple_of(j * tk, tk)
        tile(k_ref[0, 0, pl.ds(base, tk), :], (vT_ref[0, 0, j],), None, None)

    @pl.when(qi >= 1)
    def _():
        j = qi - 1
        base = pl.multiple_of(j * tk, tk)
        tile(k_ref[0, 0, pl.ds(base, tk), :], (vT_ref[0, 0, j],), bias_sc[1], None)

    base = pl.multiple_of(qi * tk, tk)
    r = lax.broadcasted_iota(jnp.int32, (tk, tq), 0)
    c = lax.broadcasted_iota(jnp.int32, (tk, tq), 1)
    tile(k_ref[0, 0, pl.ds(base, tk), :], (vT_ref[0, 0, qi],), bias_sc[0], r <= c)

    lam = lam_ref[0]
    o = acc_sc[0] / l_sc[0] - lam * (acc_sc[1] / l_sc[1])
    ms = jnp.mean(o * o, axis=0, keepdims=True)
    y = o * lax.rsqrt(ms + NORM_EPS) * g_ref[...]
    o_ref[0, 0] = (y * (1.0 - lam_init)).astype(o_ref.dtype)


def _diff_attention(qT, k, vT5, rel_bias, lam, subln_g, lam_init):
    b, h, dqk, s = qT.shape
    tq, tk = ATT_TQ, ATT_TK
    dv = vT5.shape[3]
    grid_spec = pltpu.PrefetchScalarGridSpec(
        num_scalar_prefetch=2,
        grid=(b, h, s // tq),
        in_specs=[
            pl.BlockSpec((1, 1, dqk, tq), lambda bi, hi, qi, *_: (bi, hi, 0, qi)),
            pl.BlockSpec((1, 1, s, dqk), lambda bi, hi, qi, *_: (bi, hi, 0, 0)),
            pl.BlockSpec((1, 1, s // tk, dv, tk), lambda bi, hi, qi, *_: (bi, hi, 0, 0, 0)),
            pl.BlockSpec((dv, 1), lambda bi, hi, qi, *_: (0, 0)),
        ],
        out_specs=pl.BlockSpec((1, 1, dv, tq), lambda bi, hi, qi, *_: (bi, hi, 0, qi)),
        scratch_shapes=[
            pltpu.VMEM((2, tk, tq), F32),
            pltpu.VMEM((2, 1, tq), F32),
            pltpu.VMEM((2, 1, tq), F32),
            pltpu.VMEM((2, dv, tq), F32),
        ],
    )
    return pl.pallas_call(
        functools.partial(_diff_attn_kernel, lam_init=lam_init),
        grid_spec=grid_spec,
        out_shape=jax.ShapeDtypeStruct((b, h, dv, s), BF16),
        compiler_params=_cparams(("arbitrary", "arbitrary", "arbitrary")),
        name="diff_attention",
    )(rel_bias.reshape(-1).astype(F32), lam.reshape(1).astype(F32), qT, k, vT5, subln_g.reshape(dv, 1).astype(F32))


def _sb_attn_kernel(qT_ref, k_ref, vT_ref, o_ref, acc_sc, carry_sc):
    qi = pl.program_id(2)
    tq = qT_ref.shape[-1]
    tk = vT_ref.shape[-1]
    d = qT_ref.shape[2]
    qs = (qT_ref[0, 0].astype(F32) * (d ** -0.5)).astype(BF16)

    sr = lax.broadcasted_iota(jnp.int32, (tk, 2 * tk), 0)
    sc = lax.broadcasted_iota(jnp.int32, (tk, 2 * tk), 1)
    suffix = jnp.where((sc & (tk - 1)) > sr, 1.0, 0.0).astype(BF16)

    acc_sc[...] = jnp.zeros(acc_sc.shape, F32)
    carry_sc[...] = jnp.zeros(carry_sc.shape, F32)

    def tile(z, v_tile, mask):
        e = jnp.exp(-jnp.abs(z))
        lg = jnp.log(1.0 + e)
        lk = -(jnp.maximum(z, 0.0) + lg)
        if mask is not None:
            lk = jnp.where(mask, lk, 0.0)
        hi = lk.astype(BF16)
        lo = (lk - hi.astype(F32)).astype(BF16)
        after = jnp.dot(suffix, jnp.concatenate([hi, lo], axis=0), preferred_element_type=F32)
        carry = carry_sc[...]
        a = jnp.exp(jnp.minimum(z, 0.0) - lg + after + carry)
        if mask is not None:
            a = jnp.where(mask, a, 0.0)
        acc_sc[...] += jnp.dot(v_tile, a.astype(BF16), preferred_element_type=F32)
        carry_sc[...] = carry + after[0:1, :] + lk[0:1, :]

    def run_tiles(first_tile, count, masks):
        base = pl.multiple_of(first_tile * tk, tk)
        z_all = jnp.dot(k_ref[0, 0, pl.ds(base, count * tk), :], qs, preferred_element_type=F32)
        for i in reversed(range(count)):
            tile(z_all[i * tk:(i + 1) * tk], vT_ref[0, 0, first_tile + i], None if masks is None else masks[i])

    per_q = tq // tk
    r = lax.broadcasted_iota(jnp.int32, (tk, tq), 0)
    c = lax.broadcasted_iota(jnp.int32, (tk, tq), 1)
    run_tiles(qi * per_q, per_q, [(r + i * tk) < c for i in range(per_q)])

    n_below = qi * per_q
    rem = n_below % SB_GROUP

    @pl.when(rem != 0)
    def _():
        run_tiles(n_below - per_q, per_q, None)

    def group(g, carry):
        run_tiles(n_below - rem - SB_GROUP * (g + 1), SB_GROUP, None)
        return carry

    lax.fori_loop(0, n_below // SB_GROUP, group, 0)
    o_ref[0, 0] = acc_sc[...].astype(o_ref.dtype)


def _sb_attention(qT, k, vT5):
    b, h, d, s = qT.shape
    tq, tk = ATT_TQ, SB_TK
    assert SB_GROUP % (tq // tk) == 0 and SB_GROUP // (tq // tk) == 2
    return pl.pallas_call(
        _sb_attn_kernel,
        grid=(b, h, s // tq),
        in_specs=[
            pl.BlockSpec((1, 1, d, tq), lambda bi, hi, qi: (bi, hi, 0, qi)),
            pl.BlockSpec((1, 1, s, d), lambda bi, hi, qi: (bi, hi, 0, 0)),
            pl.BlockSpec((1, 1, s // tk, d, tk), lambda bi, hi, qi: (bi, hi, 0, 0, 0)),
        ],
        out_specs=pl.BlockSpec((1, 1, d, tq), lambda bi, hi, qi: (bi, hi, 0, qi)),
        out_shape=jax.ShapeDtypeStruct((b, h, d, s), BF16),
        scratch_shapes=[pltpu.VMEM((d, tq), F32), pltpu.VMEM((1, tq), F32)],
        compiler_params=_cparams(("arbitrary", "arbitrary", "arbitrary")),
        name="sb_attention",
    )(qT, k, vT5)


def _mem_attn_kernel(q_ref, kv_ref, o_ref):
    hd = q_ref.shape[-1]
    d = MEM_DIM
    outs = []
    for hh in range(hd // d):
        q = q_ref[0, :, hh * d:(hh + 1) * d]
        kk = kv_ref[0, :, hh * d:(hh + 1) * d]
        vv = kv_ref[0, :, hd + hh * d:hd + (hh + 1) * d]
        s = lax.dot_general(q, kk, (((1,), (1,)), ((), ())), preferred_element_type=F32) * (d ** -0.5)
        m = jnp.max(s, axis=-1, keepdims=True)
        p = jnp.exp(s - m)
        p = p / jnp.sum(p, axis=-1, keepdims=True)
        outs.append(jnp.dot(p.astype(BF16), vv, preferred_element_type=F32))
    o_ref[0] = jnp.concatenate(outs, axis=-1).astype(o_ref.dtype)


def _mem_attention(q, kv, tm):
    b, s, hd = q.shape
    mlen = kv.shape[1]
    return pl.pallas_call(
        _mem_attn_kernel,
        grid=(b, s // tm),
        in_specs=[pl.BlockSpec((1, tm, hd), lambda bi, i: (bi, i, 0)),
                  pl.BlockSpec((1, mlen, 2 * hd), lambda bi, i: (bi, 0, 0))],
        out_specs=pl.BlockSpec((1, tm, hd), lambda bi, i: (bi, i, 0)),
        out_shape=jax.ShapeDtypeStruct((b, s, hd), BF16),
        compiler_params=_cparams(("arbitrary", "arbitrary")),
        name="mem_attention",
    )(q, kv)


def _merge_kernel(ya_ref, yb_ref, yc_ref, wa_ref, wb_ref, wc_ref, ga_ref, gb_ref, gc_ref, o_ref):
    pa = jnp.dot(ya_ref[...], wa_ref[...], preferred_element_type=F32)
    pb = jnp.dot(yb_ref[...], wb_ref[...], preferred_element_type=F32)
    pc = jnp.dot(yc_ref[...], wc_ref[...], preferred_element_type=F32)
    merged = (ga_ref[...].astype(F32) * pa + gb_ref[...].astype(F32) * pb + gc_ref[...].astype(F32) * pc)
    o_ref[...] = merged.astype(o_ref.dtype)


def _merge(ya, yb, yc, wa, wb, wc, gates, d_model, tm, tn):
    m = ya.shape[0]
    nblk = d_model // tn
    y_spec = lambda arr: pl.BlockSpec((tm, arr.shape[1]), lambda j, i: (i, 0))
    w_spec = lambda arr: pl.BlockSpec((arr.shape[0], tn), lambda j, i: (0, j))
    g_spec = lambda br: pl.BlockSpec((tm, tn), lambda j, i, br=br: (i, br * nblk + j))
    return pl.pallas_call(
        _merge_kernel,
        grid=(nblk, m // tm),
        in_specs=[y_spec(ya), y_spec(yb), y_spec(yc), w_spec(wa), w_spec(wb), w_spec(wc),
                  g_spec(0), g_spec(1), g_spec(2)],
        out_specs=pl.BlockSpec((tm, tn), lambda j, i: (i, j)),
        out_shape=jax.ShapeDtypeStruct((m, d_model), BF16),
        compiler_params=_cparams(("arbitrary", "arbitrary")),
        name="gated_merge",
    )(ya, yb, yc, wa, wb, wc, gates, gates, gates)


def _outproj_kernel(mg_ref, w_ref, x_ref, g_ref, wrh_ref, wrl_ref, br_ref, x1_ref, h2_ref, lg_ref):
    x1 = x_ref[...] + jnp.dot(mg_ref[...], w_ref[...], preferred_element_type=F32)
    x1_ref[...] = x1
    ms = jnp.mean(x1 * x1, axis=-1, keepdims=True)
    h2 = x1 * lax.rsqrt(ms + NORM_EPS) * g_ref[...]
    h_hi = h2.astype(BF16)
    h2_ref[...] = h_hi
    h_lo = (h2 - h_hi.astype(F32)).astype(BF16)
    lg = (jnp.dot(h_hi, wrh_ref[...], preferred_element_type=F32)
          + jnp.dot(h_lo, wrh_ref[...], preferred_element_type=F32)
          + jnp.dot(h_hi, wrl_ref[...], preferred_element_type=F32))
    lg_ref[...] = lg + br_ref[...]


def _outproj(merged, w_out, x, g_ffn, wr_hi, wr_lo, br_pad, tm):
    m, d = x.shape
    row = lambda n: pl.BlockSpec((tm, n), lambda i: (i, 0))
    full = lambda arr: pl.BlockSpec(arr.shape, lambda i: (0, 0))
    return pl.pallas_call(
        _outproj_kernel,
        grid=(m // tm,),
        in_specs=[row(d), full(w_out), row(d), full(g_ffn), full(wr_hi), full(wr_lo), full(br_pad)],
        out_specs=[row(d), row(d), row(ROUTER_PAD)],
        out_shape=[jax.ShapeDtypeStruct((m, d), F32), jax.ShapeDtypeStruct((m, d), BF16),
                   jax.ShapeDtypeStruct((m, ROUTER_PAD), F32)],
        compiler_params=_cparams(("arbitrary",)),
        name="outproj_norm_router",
    )(merged, w_out, x, g_ffn, wr_hi, wr_lo, br_pad)


def _topk_kernel(lg_ref, idx_ref, gate_ref):
    l = lg_ref[...]
    lane = lax.broadcasted_iota(jnp.int32, l.shape, 1)
    vals, ids = [], []
    for _ in range(TOP_K):
        m = jnp.max(l, axis=-1, keepdims=True)
        idx = jnp.min(jnp.where(l == m, lane, l.shape[-1]), axis=-1, keepdims=True)
        vals.append(m)
        ids.append(idx)
        l = jnp.where(lane == idx, -jnp.inf, l)
    es = [jnp.exp(v - vals[0]) for v in vals]
    den = es[0]
    for e in es[1:]:
        den = den + e
    idx_out = jnp.zeros(l.shape, jnp.int32)
    gate_out = jnp.zeros(l.shape, F32)
    for kk in range(TOP_K):
        idx_out = jnp.where(lane == kk, ids[kk], idx_out)
        gate_out = jnp.where(lane == kk, es[kk] / den, gate_out)
    idx_ref[...] = idx_out
    gate_ref[...] = gate_out


def _topk(logits, tm):
    m, n = logits.shape
    spec = pl.BlockSpec((tm, n), lambda i: (i, 0))
    return pl.pallas_call(
        _topk_kernel,
        grid=(m // tm,),
        in_specs=[spec],
        out_specs=[spec, spec],
        out_shape=[jax.ShapeDtypeStruct((m, n), jnp.int32), jax.ShapeDtypeStruct((m, n), F32)],
        compiler_params=_cparams(("arbitrary",)),
        name="router_topk",
    )(logits)


def _ffn1_kernel(sb_ref, sj_ref, se_ref, sfirst_ref, svalid_ref,
                 x_ref, wg_ref, wl_ref, bg_ref, bl_ref, o_ref, wg_sc, wl_sc):
    s = pl.program_id(0)

    @pl.when(sfirst_ref[s] == 1)
    def _():
        wg_sc[...] = wg_ref[0].astype(BF16)
        wl_sc[...] = wl_ref[0].astype(BF16)

    @pl.when(svalid_ref[s] == 1)
    def _():
        x = x_ref[...]
        a_glu = jnp.dot(x, wg_sc[...], preferred_element_type=F32) + bg_ref[0]
        a_lin = jnp.dot(x, wl_sc[...], preferred_element_type=F32) + bl_ref[0]
        a_glu = jnp.minimum(a_glu, SWIGLU_LIMIT)
        a_lin = jnp.clip(a_lin, -SWIGLU_LIMIT, SWIGLU_LIMIT)
        act = a_glu * jax.nn.sigmoid(SWIGLU_ALPHA * a_glu) * (a_lin + 1.0)
        o_ref[...] = act.astype(o_ref.dtype)


def _ffn2_kernel(sb_ref, sj_ref, se_ref, sfirst_ref, svalid_ref,
                 h_ref, w_ref, b_ref, gate_ref, o_ref, w_sc):
    s = pl.program_id(0)

    @pl.when(sfirst_ref[s] == 1)
    def _():
        w_sc[...] = w_ref[0].astype(BF16)

    @pl.when(svalid_ref[s] == 1)
    def _():
        y = jnp.dot(h_ref[...], w_sc[...], preferred_element_type=F32) + b_ref[0]
        o_ref[...] = (y * gate_ref[...]).astype(o_ref.dtype)


def _step_tables(nb, n_tiles, n_steps):
    per_e = nb * n_tiles
    cum_end = jnp.cumsum(per_e)
    total = cum_end[-1]
    blk_start = jnp.cumsum(nb) - nb
    s = jnp.minimum(jnp.arange(n_steps, dtype=jnp.int32), total - 1)
    e = jnp.minimum(jnp.searchsorted(cum_end, s, side="right"), N_EXPERTS - 1).astype(jnp.int32)
    r = s - (cum_end[e] - per_e[e])
    nbe = jnp.maximum(nb[e], 1)
    j = r // nbe
    bi = r % nbe
    valid = jnp.arange(n_steps, dtype=jnp.int32) < total
    first = jnp.logical_and(bi == 0, valid)
    i32 = lambda a: a.astype(jnp.int32)
    return i32(blk_start[e] + bi), i32(j), e, i32(first), i32(valid)


def _expert_ffn(xs, row_gate, nb, w1, b1, w2, b2):
    p, d = xs.shape
    ff = w2.shape[1]
    n_blocks = p // MOE_BLK
    nj1 = ff // FFN1_TN
    nj2 = d // FFN2_TN
    b1r = b1.reshape(N_EXPERTS, 1, 2 * ff)
    b2r = b2.reshape(N_EXPERTS, 1, d)

    t1 = _step_tables(nb, nj1, n_blocks * nj1)
    h = pl.pallas_call(
        _ffn1_kernel,
        grid_spec=pltpu.PrefetchScalarGridSpec(
            num_scalar_prefetch=5,
            grid=(n_blocks * nj1,),
            in_specs=[
                pl.BlockSpec((MOE_BLK, d), lambda s, sb, sj, se, sf, sv: (sb[s], 0)),
                pl.BlockSpec((1, d, FFN1_TN), lambda s, sb, sj, se, sf, sv: (se[s], 0, sj[s])),
                pl.BlockSpec((1, d, FFN1_TN), lambda s, sb, sj, se, sf, sv: (se[s], 0, nj1 + sj[s])),
                pl.BlockSpec((1, 1, FFN1_TN), lambda s, sb, sj, se, sf, sv: (se[s], 0, sj[s])),
                pl.BlockSpec((1, 1, FFN1_TN), lambda s, sb, sj, se, sf, sv: (se[s], 0, nj1 + sj[s])),
            ],
            out_specs=pl.BlockSpec((MOE_BLK, FFN1_TN), lambda s, sb, sj, se, sf, sv: (sb[s], sj[s])),
            scratch_shapes=[pltpu.VMEM((d, FFN1_TN), BF16), pltpu.VMEM((d, FFN1_TN), BF16)],
        ),
        out_shape=jax.ShapeDtypeStruct((p, ff), BF16),
        compiler_params=_cparams(("arbitrary",)),
        name="expert_ffn_up",
    )(*t1, xs, w1, w1, b1r, b1r)

    t2 = _step_tables(nb, nj2, n_blocks * nj2)
    ys = pl.pallas_call(
        _ffn2_kernel,
        grid_spec=pltpu.PrefetchScalarGridSpec(
            num_scalar_prefetch=5,
            grid=(n_blocks * nj2,),
            in_specs=[
                pl.BlockSpec((MOE_BLK, ff), lambda s, sb, sj, se, sf, sv: (sb[s], 0)),
                pl.BlockSpec((1, ff, FFN2_TN), lambda s, sb, sj, se, sf, sv: (se[s], 0, sj[s])),
                pl.BlockSpec((1, 1, FFN2_TN), lambda s, sb, sj, se, sf, sv: (se[s], 0, sj[s])),
                pl.BlockSpec((MOE_BLK, 1), lambda s, sb, sj, se, sf, sv: (sb[s], 0)),
            ],
            out_specs=pl.BlockSpec((MOE_BLK, FFN2_TN), lambda s, sb, sj, se, sf, sv: (sb[s], sj[s])),
            scratch_shapes=[pltpu.VMEM((ff, FFN2_TN), BF16)],
        ),
        out_shape=jax.ShapeDtypeStruct((p, d), F32),
        compiler_params=_cparams(("arbitrary",)),
        name="expert_ffn_down",
    )(*t2, h, w2, b2r, row_gate.reshape(p, 1))
    return ys


def _final_kernel(x_ref, y_ref, g_ref, o_ref):
    x = x_ref[...] + y_ref[...]
    ms = jnp.mean(x * x, axis=-1, keepdims=True)
    o_ref[...] = (x * lax.rsqrt(ms + NORM_EPS) * g_ref[...]).astype(o_ref.dtype)


def _final(x1, y, g, tm):
    m, d = x1.shape
    row = pl.BlockSpec((tm, d), lambda i: (i, 0))
    return pl.pallas_call(
        _final_kernel,
        grid=(m // tm,),
        in_specs=[row, row, pl.BlockSpec((1, d), lambda i: (0, 0))],
        out_specs=row,
        out_shape=jax.ShapeDtypeStruct((m, d), F32),
        compiler_params=_cparams(("arbitrary",)),
        name="residual_final_norm",
    )(x1, y, g.reshape(1, d).astype(F32))


def _heads_T(a, b, s, h, d):
    return a.reshape(b, s, h, d).transpose(0, 2, 3, 1)


def _heads(a, b, s, h, d):
    return a.reshape(b, s, h, d).transpose(0, 2, 1, 3)


def _heads_T_tiled(a, b, s, h, d, tk):
    return a.reshape(b, s // tk, tk, h, d).transpose(0, 3, 1, 4, 2)


def kernel(x, mem, g_mix, w_in, b_gate, rel_bias, lambda_q1, lambda_k1, lambda_q2, lambda_k2, diff_subln_g, g_mem, w_mem_kv, w_br_diff, w_br_sb, w_br_mem, w_out, g_ffn, w_router, b_router, w_exp1, b_exp1, w_exp2, b_exp2, g_final):
    b, s, d = x.shape
    t = b * s
    depth = g_mix.shape[0]
    dqk_w = DIFF_HEADS * 2 * DIFF_QK_DIM
    dv_w = DIFF_HEADS * DIFF_V_DIM
    sb_w = SB_HEADS * SB_DIM
    mem_w = MEM_HEADS * MEM_DIM
    qkv_w = 2 * dqk_w + dv_w + 3 * sb_w + mem_w
    mlen = mem.shape[1]
    tm = min(ROW_TILE, s)

    xf = x.reshape(t, d)
    for l in range(depth):
        w_in_b = w_in[l].astype(BF16)
        hmix = _rmsnorm(xf, g_mix[l], BF16, tm)
        qkv = _matmul(hmix, w_in_b, 0, qkv_w, name="in_proj_qkv")
        gates = _matmul(hmix, w_in_b, qkv_w, 3 * d, bias=b_gate[l], name="in_proj_gates")

        c0 = 0
        dq = qkv[:, c0:c0 + dqk_w]; c0 += dqk_w
        dk = qkv[:, c0:c0 + dqk_w]; c0 += dqk_w
        dvv = qkv[:, c0:c0 + dv_w]; c0 += dv_w
        sq = qkv[:, c0:c0 + sb_w]; c0 += sb_w
        sk = qkv[:, c0:c0 + sb_w]; c0 += sb_w
        sv = qkv[:, c0:c0 + sb_w]; c0 += sb_w
        mq = qkv[:, c0:c0 + mem_w]

        lam_init = 0.8 - 0.6 * math.exp(-0.3 * l)
        lam = (jnp.exp(jnp.sum(lambda_q1[l].astype(F32) * lambda_k1[l].astype(F32)))
               - jnp.exp(jnp.sum(lambda_q2[l].astype(F32) * lambda_k2[l].astype(F32))) + lam_init)
        ya_t = _diff_attention(_heads_T(dq, b, s, DIFF_HEADS, 2 * DIFF_QK_DIM),
                               _heads(dk, b, s, DIFF_HEADS, 2 * DIFF_QK_DIM),
                               _heads_T_tiled(dvv, b, s, DIFF_HEADS, DIFF_V_DIM, ATT_TK),
                               rel_bias, lam, diff_subln_g[l], lam_init)
        yb_t = _sb_attention(_heads_T(sq, b, s, SB_HEADS, SB_DIM),
                             _heads(sk, b, s, SB_HEADS, SB_DIM),
                             _heads_T_tiled(sv, b, s, SB_HEADS, SB_DIM, SB_TK))
        ya = ya_t.transpose(0, 3, 1, 2).reshape(t, dv_w)
        yb = yb_t.transpose(0, 3, 1, 2).reshape(t, sb_w)

        hmem = _rmsnorm(mem.reshape(b * mlen, d), g_mem[l], BF16, min(ROW_TILE, b * mlen))
        kv = _matmul(hmem, w_mem_kv[l].astype(BF16), 0, 2 * mem_w, name="mem_kv_proj")
        yc = _mem_attention(mq.reshape(b, s, mem_w), kv.reshape(b, mlen, 2 * mem_w), tm).reshape(t, mem_w)

        merged = _merge(ya, yb, yc, w_br_diff[l].astype(BF16), w_br_sb[l].astype(BF16),
                        w_br_mem[l].astype(BF16), gates, d, tm, MM_TN)

        wr = jnp.pad(w_router[l].astype(F32), ((0, 0), (0, ROUTER_PAD - N_EXPERTS)))
        wr_hi = wr.astype(BF16)
        wr_lo = (wr - wr_hi.astype(F32)).astype(BF16)
        br_pad = jnp.pad(b_router[l].astype(F32), (0, ROUTER_PAD - N_EXPERTS),
                         constant_values=-jnp.inf).reshape(1, ROUTER_PAD)
        x1, h2, logits = _outproj(merged, w_out[l].astype(BF16), xf, g_ffn[l].reshape(1, d).astype(F32),
                                  wr_hi, wr_lo, br_pad, tm)
        top_i, top_g = _topk(logits, tm)

        n = t * TOP_K
        e_flat = top_i[:, :TOP_K].reshape(n)
        g_flat = top_g[:, :TOP_K].reshape(n)
        tok_flat = (jnp.arange(n, dtype=jnp.int32) // TOP_K)
        onehot = (e_flat[:, None] == jnp.arange(N_EXPERTS, dtype=jnp.int32)[None, :]).astype(jnp.int32)
        csum = jnp.cumsum(onehot, axis=0)
        rank = jnp.take_along_axis(csum, e_flat[:, None], axis=1)[:, 0] - 1
        counts = csum[-1]
        nb = (counts + MOE_BLK - 1) // MOE_BLK
        pad_start = (jnp.cumsum(nb) - nb) * MOE_BLK
        dest = pad_start[e_flat] + rank
        n_blocks = n // MOE_BLK + N_EXPERTS
        p = n_blocks * MOE_BLK
        row_tok = jnp.full((p,), t, jnp.int32).at[dest].set(tok_flat)
        row_gate = jnp.zeros((p,), F32).at[dest].set(g_flat)
        h_pad = jnp.concatenate([h2, jnp.zeros((1, d), h2.dtype)], axis=0)
        xs = h_pad[row_tok]

        ys = _expert_ffn(xs, row_gate, nb, w_exp1[l], b_exp1[l], w_exp2[l], b_exp2[l])
        y = ys[dest.reshape(t, TOP_K)].sum(axis=1)
        if l + 1 < depth:
            xf = x1 + y
        else:
            return _final(x1, y, g_final, tm).reshape(b, s, d)
```

```python
import functools
import math

import jax
import jax.numpy as jnp
from jax import lax
from jax.experimental import pallas as pl
from jax.experimental.pallas import tpu as pltpu

F32 = jnp.float32
BF16 = jnp.bfloat16

DIFF_HEADS = 8
DIFF_QK_DIM = 64
DIFF_V_DIM = 128
SB_HEADS = 8
SB_DIM = 128
MEM_HEADS = 4
MEM_DIM = 256
N_BUCKETS = 32
MAX_DISTANCE = 128
N_EXPERTS = 32
TOP_K = 4
SWIGLU_LIMIT = 7.0
SWIGLU_ALPHA = 1.702
NORM_EPS = 1e-6
NEG_INF = -1e30

LANES = 128
VMEM_LIMIT_BYTES = 56 * 1024 * 1024

ROW_TILE = 512
MM_TM = 1024
MM_TN = 1024
DIFF_TILE = 512
ATT_CHUNK = 64
SB_TK = 128
SB_GROUP_KEYS = 512
MOE_BLK = 512
FFN1_TN = 512
FFN2_TN = 1024
ROUTER_PAD = LANES


def _cparams(sem):
    return pltpu.CompilerParams(dimension_semantics=sem, vmem_limit_bytes=VMEM_LIMIT_BYTES)


def _rmsnorm_kernel(x_ref, g_ref, o_ref):
    x = x_ref[...].astype(F32)
    ms = jnp.mean(x * x, axis=-1, keepdims=True)
    o_ref[...] = (x * lax.rsqrt(ms + NORM_EPS) * g_ref[...]).astype(o_ref.dtype)


def _rmsnorm(x, g, out_dtype, tm):
    m, d = x.shape
    return pl.pallas_call(
        _rmsnorm_kernel,
        grid=(m // tm,),
        in_specs=[pl.BlockSpec((tm, d), lambda i: (i, 0)),
                  pl.BlockSpec((1, d), lambda i: (0, 0))],
        out_specs=pl.BlockSpec((tm, d), lambda i: (i, 0)),
        out_shape=jax.ShapeDtypeStruct((m, d), out_dtype),
        compiler_params=_cparams(("arbitrary",)),
        name="rmsnorm",
    )(x, g.reshape(1, d).astype(F32))


def _mm_kernel(a_ref, w_ref, o_ref):
    o_ref[...] = jnp.dot(a_ref[...], w_ref[...], preferred_element_type=F32).astype(o_ref.dtype)


def _mm_sigmoid_kernel(a_ref, w_ref, b_ref, o_ref):
    acc = jnp.dot(a_ref[...], w_ref[...], preferred_element_type=F32)
    o_ref[...] = jax.nn.sigmoid(acc + b_ref[...]).astype(o_ref.dtype)


def _matmul(a, w, col_off, n_out, bias=None, tm=MM_TM, tn=MM_TN, name="matmul"):
    m, k = a.shape
    tm = min(tm, m)
    off = col_off // tn
    in_specs = [pl.BlockSpec((tm, k), lambda j, i: (i, 0)),
                pl.BlockSpec((k, tn), lambda j, i: (0, j + off))]
    args = [a, w]
    body = _mm_kernel
    if bias is not None:
        in_specs.append(pl.BlockSpec((1, tn), lambda j, i: (0, j)))
        args.append(bias.reshape(1, n_out).astype(F32))
        body = _mm_sigmoid_kernel
    return pl.pallas_call(
        body,
        grid=(n_out // tn, m // tm),
        in_specs=in_specs,
        out_specs=pl.BlockSpec((tm, tn), lambda j, i: (i, j)),
        out_shape=jax.ShapeDtypeStruct((m, n_out), BF16),
        compiler_params=_cparams(("arbitrary", "arbitrary")),
        name=name,
    )(*args)


def _t5_bias_tile(rel_ref, h, offset, tk, tq):
    r = lax.broadcasted_iota(jnp.int32, (tk, tq), 0)
    c = lax.broadcasted_iota(jnp.int32, (tk, tq), 1)
    n = jnp.maximum(c - r + offset, 0)
    max_exact = N_BUCKETS // 2
    nf = jnp.maximum(n, 1).astype(F32)
    large = max_exact + (jnp.log(nf / max_exact) / math.log(MAX_DISTANCE / max_exact)
                         * (N_BUCKETS - max_exact)).astype(jnp.int32)
    large = jnp.minimum(large, N_BUCKETS - 1)
    bucket = jnp.where(n < max_exact, n, large)
    far = rel_ref[(N_BUCKETS - 1) * DIFF_HEADS + h]
    out = jnp.zeros((tk, tq), F32)
    for j in range(N_BUCKETS - 1):
        out = jnp.where(bucket == j, rel_ref[j * DIFF_HEADS + h] - far, out)
    return out


def _tree(xs, op):
    while len(xs) > 1:
        nxt = [op(xs[i], xs[i + 1]) for i in range(0, len(xs) - 1, 2)]
        if len(xs) % 2:
            nxt.append(xs[-1])
        xs = nxt
    return xs[0]


def _rows_to_8(x, op):
    return _tree([x[i:i + 8] for i in range(0, x.shape[0], 8)], op)


_FAR, _NEAR, _DIAG = 0, 1, 2


def _diff_attn_kernel(rel_ref, lam_ref, qT_ref, k_ref, vT_ref, g_ref, o_ref,
                      bias_sc, s_a, s_b, pb_sc, m_sc, l_sc, acc_sc, *, lam_init):
    h = pl.program_id(1)
    qi = pl.program_id(2)
    tq = qT_ref.shape[-1]
    tk = vT_ref.shape[-1]
    ch = ATT_CHUNK
    nch = tk // ch

    @pl.when(qi == 0)
    def _():
        for c in range(nch):
            bias_sc[0, c * ch:(c + 1) * ch, :] = _t5_bias_tile(rel_ref, h, -c * ch, ch, tq)
            bias_sc[1, c * ch:(c + 1) * ch, :] = _t5_bias_tile(rel_ref, h, tk - c * ch, ch, tq)

    qs = qT_ref[0, 0] * jnp.asarray(DIFF_QK_DIM ** -0.5, BF16)
    row = lax.broadcasted_iota(jnp.int32, qs.shape, 0)
    zero = jnp.zeros_like(qs)
    qmaps = (jnp.where(row < DIFF_QK_DIM, qs, zero), jnp.where(row >= DIFF_QK_DIM, qs, zero))

    m_sc[...] = jnp.full(m_sc.shape, NEG_INF, F32)
    l_sc[...] = jnp.zeros(l_sc.shape, F32)
    acc_sc[...] = jnp.zeros(acc_sc.shape, F32)

    def qk(j, dst):
        base = pl.multiple_of(j * tk, tk)
        kt = k_ref[0, 0, pl.ds(base, tk), :]
        for mi in range(2):
            dst[mi] = jnp.dot(kt, qmaps[mi], preferred_element_type=F32)

    def softmax_pv(src, j, kind):
        vt = vT_ref[0, 0, j]
        for mi in range(2):
            parts = []
            for c in range(nch):
                rows = slice(c * ch, (c + 1) * ch)
                x = src[mi, rows, :]
                if kind != _FAR:
                    x = x + bias_sc[0 if kind == _DIAG else 1, rows, :]
                    if kind == _DIAG:
                        r = lax.broadcasted_iota(jnp.int32, (ch, tq), 0) + c * ch
                        cc = lax.broadcasted_iota(jnp.int32, (ch, tq), 1)
                        x = jnp.where(r <= cc, x, NEG_INF)
                    src[mi, rows, :] = x
                parts.append(_rows_to_8(x, jnp.maximum))
            m_old = m_sc[mi]
            m_new = jnp.maximum(m_old, jnp.max(_tree(parts, jnp.maximum), axis=0, keepdims=True))
            alpha = jnp.exp(m_old - m_new)
            parts = []
            for c in range(nch):
                rows = slice(c * ch, (c + 1) * ch)
                p = jnp.exp(src[mi, rows, :] - m_new)
                parts.append(_rows_to_8(p, jnp.add))
                pb_sc[mi, rows, :] = p.astype(BF16)
            l_sc[mi] = alpha * l_sc[mi] + jnp.sum(_tree(parts, jnp.add), axis=0, keepdims=True)
            pv = jnp.dot(vt, pb_sc[mi], preferred_element_type=F32)
            acc_sc[mi] = alpha * acc_sc[mi] + pv
            m_sc[mi] = m_new

    n_far = jnp.maximum(qi - 1, 0)

    @pl.when(qi == 0)
    def _():
        qk(0, s_a)
        softmax_pv(s_a, 0, _DIAG)

    @pl.when(qi >= 1)
    def _():
        qk(0, s_a)

        def far_pair(i, carry):
            qk(2 * i + 1, s_b)
            softmax_pv(s_a, 2 * i, _FAR)
            qk(2 * i + 2, s_a)
            softmax_pv(s_b, 2 * i + 1, _FAR)
            return carry

        lax.fori_loop(0, n_far // 2, far_pair, 0)

        @pl.when(n_far % 2 == 1)
        def _():
            qk(qi - 1, s_b)
            softmax_pv(s_a, qi - 2, _FAR)
            qk(qi, s_a)
            softmax_pv(s_b, qi - 1, _NEAR)
            softmax_pv(s_a, qi, _DIAG)

        @pl.when(n_far % 2 == 0)
        def _():
            qk(qi, s_b)
            softmax_pv(s_a, qi - 1, _NEAR)
            softmax_pv(s_b, qi, _DIAG)

    lam = lam_ref[0]
    o = acc_sc[0] / l_sc[0] - lam * (acc_sc[1] / l_sc[1])
    ms = jnp.mean(o * o, axis=0, keepdims=True)
    y = o * lax.rsqrt(ms + NORM_EPS) * g_ref[...]
    o_ref[0, 0] = (y * (1.0 - lam_init)).astype(o_ref.dtype)


def _diff_attention(qT, k, vT5, rel_bias, lam, subln_g, lam_init):
    b, h, dqk, s = qT.shape
    tq = tk = DIFF_TILE
    dv = vT5.shape[3]
    grid_spec = pltpu.PrefetchScalarGridSpec(
        num_scalar_prefetch=2,
        grid=(b, h, s // tq),
        in_specs=[
            pl.BlockSpec((1, 1, dqk, tq), lambda bi, hi, qi, *_: (bi, hi, 0, qi)),
            pl.BlockSpec((1, 1, s, dqk), lambda bi, hi, qi, *_: (bi, hi, 0, 0)),
            pl.BlockSpec((1, 1, s // tk, dv, tk), lambda bi, hi, qi, *_: (bi, hi, 0, 0, 0)),
            pl.BlockSpec((dv, 1), lambda bi, hi, qi, *_: (0, 0)),
        ],
        out_specs=pl.BlockSpec((1, 1, dv, tq), lambda bi, hi, qi, *_: (bi, hi, 0, qi)),
        scratch_shapes=[
            pltpu.VMEM((2, tk, tq), F32),
            pltpu.VMEM((2, tk, tq), F32),
            pltpu.VMEM((2, tk, tq), F32),
            pltpu.VMEM((2, tk, tq), BF16),
            pltpu.VMEM((2, 1, tq), F32),
            pltpu.VMEM((2, 1, tq), F32),
            pltpu.VMEM((2, dv, tq), F32),
        ],
    )
    return pl.pallas_call(
        functools.partial(_diff_attn_kernel, lam_init=lam_init),
        grid_spec=grid_spec,
        out_shape=jax.ShapeDtypeStruct((b, h, dv, s), BF16),
        compiler_params=_cparams(("arbitrary", "arbitrary", "arbitrary")),
        name="diff_attention",
    )(rel_bias.reshape(-1).astype(F32), lam.reshape(1).astype(F32), qT, k, vT5, subln_g.reshape(dv, 1).astype(F32))


def _sb_attn_kernel(qT_ref, k_ref, vT_ref, o_ref, z_a, z_b, hl_sc, a_sc, acc_sc, carry_sc):
    qi = pl.program_id(2)
    tq = qT_ref.shape[-1]
    gk = vT_ref.shape[-1]
    tk = SB_TK
    nt = gk // tk
    ch = ATT_CHUNK
    d = qT_ref.shape[2]
    qs = (qT_ref[0, 0].astype(F32) * (d ** -0.5)).astype(BF16)

    sr = lax.broadcasted_iota(jnp.int32, (tk, 2 * tk), 0)
    sc = lax.broadcasted_iota(jnp.int32, (tk, 2 * tk), 1)
    suffix = jnp.where((sc & (tk - 1)) > sr, 1.0, 0.0).astype(BF16)

    acc_sc[...] = jnp.zeros(acc_sc.shape, F32)
    carry_sc[...] = jnp.zeros(carry_sc.shape, F32)

    def qk(g, dst):
        base = pl.multiple_of(g * gk, gk)
        dst[...] = jnp.dot(k_ref[0, 0, pl.ds(base, gk), :], qs, preferred_element_type=F32)

    def strict_mask(row0):
        r = lax.broadcasted_iota(jnp.int32, (ch, tq), 0) + row0
        c = lax.broadcasted_iota(jnp.int32, (ch, tq), 1)
        return r < c

    def process(src, g, diag):
        lk0 = []
        for t in range(nt):
            for c in range(tk // ch):
                row0 = t * tk + c * ch
                z = src[row0:row0 + ch, :]
                lg = jnp.log(1.0 + jnp.exp(-jnp.abs(z)))
                lk = -(jnp.maximum(z, 0.0) + lg)
                if diag:
                    lk = jnp.where(strict_mask(row0), lk, 0.0)
                hi = lk.astype(BF16)
                hl_sc[t, c * ch:(c + 1) * ch, :] = hi
                hl_sc[t, tk + c * ch:tk + (c + 1) * ch, :] = (lk - hi.astype(F32)).astype(BF16)
                src[row0:row0 + ch, :] = z + lk
                if c == 0:
                    lk0.append(lk[0:1, :])
        carry = carry_sc[...]
        for t in reversed(range(nt)):
            after = jnp.dot(suffix, hl_sc[t], preferred_element_type=F32)
            for c in range(tk // ch):
                row0 = t * tk + c * ch
                a = jnp.exp(src[row0:row0 + ch, :] + after[c * ch:(c + 1) * ch, :] + carry)
                if diag:
                    a = jnp.where(strict_mask(row0), a, 0.0)
                a_sc[row0:row0 + ch, :] = a.astype(BF16)
            carry = carry + after[0:1, :] + lk0[t]
        carry_sc[...] = carry
        acc_sc[...] += jnp.dot(vT_ref[0, 0, g], a_sc[...], preferred_element_type=F32)

    qk(qi, z_a)

    @pl.when(qi == 0)
    def _():
        process(z_a, qi, True)

    @pl.when(qi >= 1)
    def _():
        qk(qi - 1, z_b)
        process(z_a, qi, True)
        n_pairs = (qi - 1) // 2

        def pair(i, carry):
            g = qi - 1 - 2 * i
            qk(g - 1, z_a)
            process(z_b, g, False)
            qk(g - 2, z_b)
            process(z_a, g - 1, False)
            return carry

        lax.fori_loop(0, n_pairs, pair, 0)

        @pl.when(qi % 2 == 0)
        def _():
            qk(0, z_a)
            process(z_b, 1, False)
            process(z_a, 0, False)

        @pl.when(qi % 2 == 1)
        def _():
            process(z_b, 0, False)

    o_ref[0, 0] = acc_sc[...].astype(o_ref.dtype)


def _sb_attention(qT, k, vT5):
    b, h, d, s = qT.shape
    tq = gk = SB_GROUP_KEYS
    return pl.pallas_call(
        _sb_attn_kernel,
        grid=(b, h, s // tq),
        in_specs=[
            pl.BlockSpec((1, 1, d, tq), lambda bi, hi, qi: (bi, hi, 0, qi)),
            pl.BlockSpec((1, 1, s, d), lambda bi, hi, qi: (bi, hi, 0, 0)),
            pl.BlockSpec((1, 1, s // gk, d, gk), lambda bi, hi, qi: (bi, hi, 0, 0, 0)),
        ],
        out_specs=pl.BlockSpec((1, 1, d, tq), lambda bi, hi, qi: (bi, hi, 0, qi)),
        out_shape=jax.ShapeDtypeStruct((b, h, d, s), BF16),
        scratch_shapes=[
            pltpu.VMEM((gk, tq), F32),
            pltpu.VMEM((gk, tq), F32),
            pltpu.VMEM((gk // SB_TK, 2 * SB_TK, tq), BF16),
            pltpu.VMEM((gk, tq), BF16),
            pltpu.VMEM((d, tq), F32),
            pltpu.VMEM((1, tq), F32),
        ],
        compiler_params=_cparams(("arbitrary", "arbitrary", "arbitrary")),
        name="sb_attention",
    )(qT, k, vT5)


def _mem_attn_kernel(q_ref, kv_ref, o_ref):
    hd = q_ref.shape[-1]
    d = MEM_DIM
    outs = []
    for hh in range(hd // d):
        q = q_ref[0, :, hh * d:(hh + 1) * d]
        kk = kv_ref[0, :, hh * d:(hh + 1) * d]
        vv = kv_ref[0, :, hd + hh * d:hd + (hh + 1) * d]
        s = lax.dot_general(q, kk, (((1,), (1,)), ((), ())), preferred_element_type=F32) * (d ** -0.5)
        m = jnp.max(s, axis=-1, keepdims=True)
        p = jnp.exp(s - m)
        p = p / jnp.sum(p, axis=-1, keepdims=True)
        outs.append(jnp.dot(p.astype(BF16), vv, preferred_element_type=F32))
    o_ref[0] = jnp.concatenate(outs, axis=-1).astype(o_ref.dtype)


def _mem_attention(q, kv, tm):
    b, s, hd = q.shape
    mlen = kv.shape[1]
    return pl.pallas_call(
        _mem_attn_kernel,
        grid=(b, s // tm),
        in_specs=[pl.BlockSpec((1, tm, hd), lambda bi, i: (bi, i, 0)),
                  pl.BlockSpec((1, mlen, 2 * hd), lambda bi, i: (bi, 0, 0))],
        out_specs=pl.BlockSpec((1, tm, hd), lambda bi, i: (bi, i, 0)),
        out_shape=jax.ShapeDtypeStruct((b, s, hd), BF16),
        compiler_params=_cparams(("arbitrary", "arbitrary")),
        name="mem_attention",
    )(q, kv)


def _merge_kernel(ya_ref, yb_ref, yc_ref, wa_ref, wb_ref, wc_ref, ga_ref, gb_ref, gc_ref, o_ref):
    pa = jnp.dot(ya_ref[...], wa_ref[...], preferred_element_type=F32)
    pb = jnp.dot(yb_ref[...], wb_ref[...], preferred_element_type=F32)
    pc = jnp.dot(yc_ref[...], wc_ref[...], preferred_element_type=F32)
    merged = (ga_ref[...].astype(F32) * pa + gb_ref[...].astype(F32) * pb + gc_ref[...].astype(F32) * pc)
    o_ref[...] = merged.astype(o_ref.dtype)


def _merge(ya, yb, yc, wa, wb, wc, gates, d_model, tm, tn):
    m = ya.shape[0]
    nblk = d_model // tn
    y_spec = lambda arr: pl.BlockSpec((tm, arr.shape[1]), lambda j, i: (i, 0))
    w_spec = lambda arr: pl.BlockSpec((arr.shape[0], tn), lambda j, i: (0, j))
    g_spec = lambda br: pl.BlockSpec((tm, tn), lambda j, i, br=br: (i, br * nblk + j))
    return pl.pallas_call(
        _merge_kernel,
        grid=(nblk, m // tm),
        in_specs=[y_spec(ya), y_spec(yb), y_spec(yc), w_spec(wa), w_spec(wb), w_spec(wc),
                  g_spec(0), g_spec(1), g_spec(2)],
        out_specs=pl.BlockSpec((tm, tn), lambda j, i: (i, j)),
        out_shape=jax.ShapeDtypeStruct((m, d_model), BF16),
        compiler_params=_cparams(("arbitrary", "arbitrary")),
        name="gated_merge",
    )(ya, yb, yc, wa, wb, wc, gates, gates, gates)


def _outproj_kernel(mg_ref, w_ref, x_ref, g_ref, wrh_ref, wrl_ref, br_ref, x1_ref, h2_ref, lg_ref):
    x1 = x_ref[...] + jnp.dot(mg_ref[...], w_ref[...], preferred_element_type=F32)
    x1_ref[...] = x1
    ms = jnp.mean(x1 * x1, axis=-1, keepdims=True)
    h2 = x1 * lax.rsqrt(ms + NORM_EPS) * g_ref[...]
    h_hi = h2.astype(BF16)
    h2_ref[...] = h_hi
    h_lo = (h2 - h_hi.astype(F32)).astype(BF16)
    lg = (jnp.dot(h_hi, wrh_ref[...], preferred_element_type=F32)
          + jnp.dot(h_lo, wrh_ref[...], preferred_element_type=F32)
          + jnp.dot(h_hi, wrl_ref[...], preferred_element_type=F32))
    lg_ref[...] = lg + br_ref[...]


def _outproj(merged, w_out, x, g_ffn, wr_hi, wr_lo, br_pad, tm):
    m, d = x.shape
    row = lambda n: pl.BlockSpec((tm, n), lambda i: (i, 0))
    full = lambda arr: pl.BlockSpec(arr.shape, lambda i: (0, 0))
    return pl.pallas_call(
        _outproj_kernel,
        grid=(m // tm,),
        in_specs=[row(d), full(w_out), row(d), full(g_ffn), full(wr_hi), full(wr_lo), full(br_pad)],
        out_specs=[row(d), row(d), row(ROUTER_PAD)],
        out_shape=[jax.ShapeDtypeStruct((m, d), F32), jax.ShapeDtypeStruct((m, d), BF16),
                   jax.ShapeDtypeStruct((m, ROUTER_PAD), F32)],
        compiler_params=_cparams(("arbitrary",)),
        name="outproj_norm_router",
    )(merged, w_out, x, g_ffn, wr_hi, wr_lo, br_pad)


def _topk_kernel(lg_ref, idx_ref, gate_ref):
    l = lg_ref[...]
    lane = lax.broadcasted_iota(jnp.int32, l.shape, 1)
    vals, ids = [], []
    for _ in range(TOP_K):
        m = jnp.max(l, axis=-1, keepdims=True)
        idx = jnp.min(jnp.where(l == m, lane, l.shape[-1]), axis=-1, keepdims=True)
        vals.append(m)
        ids.append(idx)
        l = jnp.where(lane == idx, -jnp.inf, l)
    es = [jnp.exp(v - vals[0]) for v in vals]
    den = es[0]
    for e in es[1:]:
        den = den + e
    idx_out = jnp.zeros(l.shape, jnp.int32)
    gate_out = jnp.zeros(l.shape, F32)
    for kk in range(TOP_K):
        idx_out = jnp.where(lane == kk, ids[kk], idx_out)
        gate_out = jnp.where(lane == kk, es[kk] / den, gate_out)
    idx_ref[...] = idx_out
    gate_ref[...] = gate_out


def _topk(logits, tm):
    m, n = logits.shape
    spec = pl.BlockSpec((tm, n), lambda i: (i, 0))
    return pl.pallas_call(
        _topk_kernel,
        grid=(m // tm,),
        in_specs=[spec],
        out_specs=[spec, spec],
        out_shape=[jax.ShapeDtypeStruct((m, n), jnp.int32), jax.ShapeDtypeStruct((m, n), F32)],
        compiler_params=_cparams(("arbitrary",)),
        name="router_topk",
    )(logits)


def _ffn1_kernel(sb_ref, sj_ref, se_ref, sfirst_ref, svalid_ref,
                 x_ref, wg_ref, wl_ref, bg_ref, bl_ref, o_ref, wg_sc, wl_sc):
    s = pl.program_id(0)

    @pl.when(sfirst_ref[s] == 1)
    def _():
        wg_sc[...] = wg_ref[0].astype(BF16)
        wl_sc[...] = wl_ref[0].astype(BF16)

    @pl.when(svalid_ref[s] == 1)
    def _():
        x = x_ref[...]
        a_glu = jnp.dot(x, wg_sc[...], preferred_element_type=F32) + bg_ref[0]
        a_lin = jnp.dot(x, wl_sc[...], preferred_element_type=F32) + bl_ref[0]
        a_glu = jnp.minimum(a_glu, SWIGLU_LIMIT)
        a_lin = jnp.clip(a_lin, -SWIGLU_LIMIT, SWIGLU_LIMIT)
        act = a_glu * jax.nn.sigmoid(SWIGLU_ALPHA * a_glu) * (a_lin + 1.0)
        o_ref[...] = act.astype(o_ref.dtype)


def _ffn2_kernel(sb_ref, sj_ref, se_ref, sfirst_ref, svalid_ref,
                 h_ref, w_ref, b_ref, gate_ref, o_ref, w_sc):
    s = pl.program_id(0)

    @pl.when(sfirst_ref[s] == 1)
    def _():
        w_sc[...] = w_ref[0].astype(BF16)

    @pl.when(svalid_ref[s] == 1)
    def _():
        y = jnp.dot(h_ref[...], w_sc[...], preferred_element_type=F32) + b_ref[0]
        o_ref[...] = (y * gate_ref[...]).astype(o_ref.dtype)


def _step_tables(nb, n_tiles, n_steps):
    per_e = nb * n_tiles
    cum_end = jnp.cumsum(per_e)
    total = cum_end[-1]
    blk_start = jnp.cumsum(nb) - nb
    s = jnp.minimum(jnp.arange(n_steps, dtype=jnp.int32), total - 1)
    e = jnp.minimum(jnp.searchsorted(cum_end, s, side="right"), N_EXPERTS - 1).astype(jnp.int32)
    r = s - (cum_end[e] - per_e[e])
    nbe = jnp.maximum(nb[e], 1)
    j = r // nbe
    bi = r % nbe
    valid = jnp.arange(n_steps, dtype=jnp.int32) < total
    first = jnp.logical_and(bi == 0, valid)
    i32 = lambda a: a.astype(jnp.int32)
    return i32(blk_start[e] + bi), i32(j), e, i32(first), i32(valid)


def _expert_ffn(xs, row_gate, nb, w1, b1, w2, b2):
    p, d = xs.shape
    ff = w2.shape[1]
    n_blocks = p // MOE_BLK
    nj1 = ff // FFN1_TN
    nj2 = d // FFN2_TN
    b1r = b1.reshape(N_EXPERTS, 1, 2 * ff)
    b2r = b2.reshape(N_EXPERTS, 1, d)

    t1 = _step_tables(nb, nj1, n_blocks * nj1)
    h = pl.pallas_call(
        _ffn1_kernel,
        grid_spec=pltpu.PrefetchScalarGridSpec(
            num_scalar_prefetch=5,
            grid=(n_blocks * nj1,),
            in_specs=[
                pl.BlockSpec((MOE_BLK, d), lambda s, sb, sj, se, sf, sv: (sb[s], 0)),
                pl.BlockSpec((1, d, FFN1_TN), lambda s, sb, sj, se, sf, sv: (se[s], 0, sj[s])),
                pl.BlockSpec((1, d, FFN1_TN), lambda s, sb, sj, se, sf, sv: (se[s], 0, nj1 + sj[s])),
                pl.BlockSpec((1, 1, FFN1_TN), lambda s, sb, sj, se, sf, sv: (se[s], 0, sj[s])),
                pl.BlockSpec((1, 1, FFN1_TN), lambda s, sb, sj, se, sf, sv: (se[s], 0, nj1 + sj[s])),
            ],
            out_specs=pl.BlockSpec((MOE_BLK, FFN1_TN), lambda s, sb, sj, se, sf, sv: (sb[s], sj[s])),
            scratch_shapes=[pltpu.VMEM((d, FFN1_TN), BF16), pltpu.VMEM((d, FFN1_TN), BF16)],
        ),
        out_shape=jax.ShapeDtypeStruct((p, ff), BF16),
        compiler_params=_cparams(("arbitrary",)),
        name="expert_ffn_up",
    )(*t1, xs, w1, w1, b1r, b1r)

    t2 = _step_tables(nb, nj2, n_blocks * nj2)
    ys = pl.pallas_call(
        _ffn2_kernel,
        grid_spec=pltpu.PrefetchScalarGridSpec(
            num_scalar_prefetch=5,
            grid=(n_blocks * nj2,),
            in_specs=[
                pl.BlockSpec((MOE_BLK, ff), lambda s, sb, sj, se, sf, sv: (sb[s], 0)),
                pl.BlockSpec((1, ff, FFN2_TN), lambda s, sb, sj, se, sf, sv: (se[s], 0, sj[s])),
                pl.BlockSpec((1, 1, FFN2_TN), lambda s, sb, sj, se, sf, sv: (se[s], 0, sj[s])),
                pl.BlockSpec((MOE_BLK, 1), lambda s, sb, sj, se, sf, sv: (sb[s], 0)),
            ],
            out_specs=pl.BlockSpec((MOE_BLK, FFN2_TN), lambda s, sb, sj, se, sf, sv: (sb[s], sj[s])),
            scratch_shapes=[pltpu.VMEM((ff, FFN2_TN), BF16)],
        ),
        out_shape=jax.ShapeDtypeStruct((p, d), F32),
        compiler_params=_cparams(("arbitrary",)),
        name="expert_ffn_down",
    )(*t2, h, w2, b2r, row_gate.reshape(p, 1))
    return ys


def _final_kernel(x_ref, y_ref, g_ref, o_ref):
    x = x_ref[...] + y_ref[...]
    ms = jnp.mean(x * x, axis=-1, keepdims=True)
    o_ref[...] = (x * lax.rsqrt(ms + NORM_EPS) * g_ref[...]).astype(o_ref.dtype)


def _final(x1, y, g, tm):
    m, d = x1.shape
    row = pl.BlockSpec((tm, d), lambda i: (i, 0))
    return pl.pallas_call(
        _final_kernel,
        grid=(m // tm,),
        in_specs=[row, row, pl.BlockSpec((1, d), lambda i: (0, 0))],
        out_specs=row,
        out_shape=jax.ShapeDtypeStruct((m, d), F32),
        compiler_params=_cparams(("arbitrary",)),
        name="residual_final_norm",
    )(x1, y, g.reshape(1, d).astype(F32))


def _heads_T(a, b, s, h, d):
    return a.reshape(b, s, h, d).transpose(0, 2, 3, 1)


def _heads(a, b, s, h, d):
    return a.reshape(b, s, h, d).transpose(0, 2, 1, 3)


def _heads_T_tiled(a, b, s, h, d, tk):
    return a.reshape(b, s // tk, tk, h, d).transpose(0, 3, 1, 4, 2)


def kernel(x, mem, g_mix, w_in, b_gate, rel_bias, lambda_q1, lambda_k1, lambda_q2, lambda_k2, diff_subln_g, g_mem, w_mem_kv, w_br_diff, w_br_sb, w_br_mem, w_out, g_ffn, w_router, b_router, w_exp1, b_exp1, w_exp2, b_exp2, g_final):
    b, s, d = x.shape
    t = b * s
    depth = g_mix.shape[0]
    dqk_w = DIFF_HEADS * 2 * DIFF_QK_DIM
    dv_w = DIFF_HEADS * DIFF_V_DIM
    sb_w = SB_HEADS * SB_DIM
    mem_w = MEM_HEADS * MEM_DIM
    qkv_w = 2 * dqk_w + dv_w + 3 * sb_w + mem_w
    mlen = mem.shape[1]
    tm = min(ROW_TILE, s)

    xf = x.reshape(t, d)
    for l in range(depth):
        w_in_b = w_in[l].astype(BF16)
        hmix = _rmsnorm(xf, g_mix[l], BF16, tm)
        qkv = _matmul(hmix, w_in_b, 0, qkv_w, name="in_proj_qkv")
        gates = _matmul(hmix, w_in_b, qkv_w, 3 * d, bias=b_gate[l], name="in_proj_gates")

        c0 = 0
        dq = qkv[:, c0:c0 + dqk_w]; c0 += dqk_w
        dk = qkv[:, c0:c0 + dqk_w]; c0 += dqk_w
        dvv = qkv[:, c0:c0 + dv_w]; c0 += dv_w
        sq = qkv[:, c0:c0 + sb_w]; c0 += sb_w
        sk = qkv[:, c0:c0 + sb_w]; c0 += sb_w
        sv = qkv[:, c0:c0 + sb_w]; c0 += sb_w
        mq = qkv[:, c0:c0 + mem_w]

        lam_init = 0.8 - 0.6 * math.exp(-0.3 * l)
        lam = (jnp.exp(jnp.sum(lambda_q1[l].astype(F32) * lambda_k1[l].astype(F32)))
               - jnp.exp(jnp.sum(lambda_q2[l].astype(F32) * lambda_k2[l].astype(F32))) + lam_init)
        ya_t = _diff_attention(_heads_T(dq, b, s, DIFF_HEADS, 2 * DIFF_QK_DIM),
                               _heads(dk, b, s, DIFF_HEADS, 2 * DIFF_QK_DIM),
                               _heads_T_tiled(dvv, b, s, DIFF_HEADS, DIFF_V_DIM, DIFF_TILE),
                               rel_bias, lam, diff_subln_g[l], lam_init)
        yb_t = _sb_attention(_heads_T(sq, b, s, SB_HEADS, SB_DIM),
                             _heads(sk, b, s, SB_HEADS, SB_DIM),
                             _heads_T_tiled(sv, b, s, SB_HEADS, SB_DIM, SB_GROUP_KEYS))
        ya = ya_t.transpose(0, 3, 1, 2).reshape(t, dv_w)
        yb = yb_t.transpose(0, 3, 1, 2).reshape(t, sb_w)

        hmem = _rmsnorm(mem.reshape(b * mlen, d), g_mem[l], BF16, min(ROW_TILE, b * mlen))
        kv = _matmul(hmem, w_mem_kv[l].astype(BF16), 0, 2 * mem_w, name="mem_kv_proj")
        yc = _mem_attention(mq.reshape(b, s, mem_w), kv.reshape(b, mlen, 2 * mem_w), tm).reshape(t, mem_w)

        merged = _merge(ya, yb, yc, w_br_diff[l].astype(BF16), w_br_sb[l].astype(BF16),
                        w_br_mem[l].astype(BF16), gates, d, tm, MM_TN)

        wr = jnp.pad(w_router[l].astype(F32), ((0, 0), (0, ROUTER_PAD - N_EXPERTS)))
        wr_hi = wr.astype(BF16)
        wr_lo = (wr - wr_hi.astype(F32)).astype(BF16)
        br_pad = jnp.pad(b_router[l].astype(F32), (0, ROUTER_PAD - N_EXPERTS),
                         constant_values=-jnp.inf).reshape(1, ROUTER_PAD)
        x1, h2, logits = _outproj(merged, w_out[l].astype(BF16), xf, g_ffn[l].reshape(1, d).astype(F32),
                                  wr_hi, wr_lo, br_pad, tm)
        top_i, top_g = _topk(logits, tm)

        n = t * TOP_K
        e_flat = top_i[:, :TOP_K].reshape(n)
        g_flat = top_g[:, :TOP_K].reshape(n)
        tok_flat = (jnp.arange(n, dtype=jnp.int32) // TOP_K)
        onehot = (e_flat[:, None] == jnp.arange(N_EXPERTS, dtype=jnp.int32)[None, :]).astype(jnp.int32)
        csum = jnp.cumsum(onehot, axis=0)
        rank = jnp.take_along_axis(csum, e_flat[:, None], axis=1)[:, 0] - 1
        counts = csum[-1]
        nb = (counts + MOE_BLK - 1) // MOE_BLK
        pad_start = (jnp.cumsum(nb) - nb) * MOE_BLK
        dest = pad_start[e_flat] + rank
        n_blocks = n // MOE_BLK + N_EXPERTS
        p = n_blocks * MOE_BLK
        row_tok = jnp.full((p,), t, jnp.int32).at[dest].set(tok_flat)
        row_gate = jnp.zeros((p,), F32).at[dest].set(g_flat)
        h_pad = jnp.concatenate([h2, jnp.zeros((1, d), h2.dtype)], axis=0)
        xs = h_pad[row_tok]

        ys = _expert_ffn(xs, row_gate, nb, w_exp1[l], b_exp1[l], w_exp2[l], b_exp2[l])
        y = ys[dest.reshape(t, TOP_K)].sum(axis=1)
        if l + 1 < depth:
            xf = x1 + y
        else:
            return _final(x1, y, g_final, tm).reshape(b, s, d)
```

```python
import functools
import math

import jax
import jax.numpy as jnp
from jax import lax
from jax.experimental import pallas as pl
from jax.experimental.pallas import tpu as pltpu

F32 = jnp.float32
BF16 = jnp.bfloat16

DIFF_HEADS = 8
DIFF_QK_DIM = 64
DIFF_V_DIM = 128
SB_HEADS = 8
SB_DIM = 128
MEM_HEADS = 4
MEM_DIM = 256
N_BUCKETS = 32
MAX_DISTANCE = 128
N_EXPERTS = 32
TOP_K = 4
SWIGLU_LIMIT = 7.0
SWIGLU_ALPHA = 1.702
NORM_EPS = 1e-6
NEG_INF = -1e30

LANES = 128
VMEM_LIMIT_BYTES = 56 * 1024 * 1024

ROW_TILE = 512
MM_TM = 1024
MM_TN = 1024
DIFF_TILE = 512
ATT_CHUNK = 64
SB_TK = 128
SB_GROUP_KEYS = 512
MOE_BLK = 512
FFN1_TN = 512
FFN2_TN = 1024
ROUTER_PAD = LANES
DMA_ROWS = 256
DMA_UNROLL = 8


def _cparams(sem):
    return pltpu.CompilerParams(dimension_semantics=sem, vmem_limit_bytes=VMEM_LIMIT_BYTES)


def _rmsnorm_kernel(x_ref, g_ref, o_ref):
    x = x_ref[...].astype(F32)
    ms = jnp.mean(x * x, axis=-1, keepdims=True)
    o_ref[...] = (x * lax.rsqrt(ms + NORM_EPS) * g_ref[...]).astype(o_ref.dtype)


def _rmsnorm(x, g, out_dtype, tm):
    m, d = x.shape
    return pl.pallas_call(
        _rmsnorm_kernel,
        grid=(m // tm,),
        in_specs=[pl.BlockSpec((tm, d), lambda i: (i, 0)),
                  pl.BlockSpec((1, d), lambda i: (0, 0))],
        out_specs=pl.BlockSpec((tm, d), lambda i: (i, 0)),
        out_shape=jax.ShapeDtypeStruct((m, d), out_dtype),
        compiler_params=_cparams(("arbitrary",)),
        name="rmsnorm",
    )(x, g.reshape(1, d).astype(F32))


def _mm_kernel(a_ref, w_ref, o_ref):
    o_ref[...] = jnp.dot(a_ref[...], w_ref[...], preferred_element_type=F32).astype(o_ref.dtype)


def _mm_sigmoid_kernel(a_ref, w_ref, b_ref, o_ref):
    acc = jnp.dot(a_ref[...], w_ref[...], preferred_element_type=F32)
    o_ref[...] = jax.nn.sigmoid(acc + b_ref[...]).astype(o_ref.dtype)


def _matmul(a, w, col_off, n_out, bias=None, tm=MM_TM, tn=MM_TN, name="matmul"):
    m, k = a.shape
    tm = min(tm, m)
    off = col_off // tn
    in_specs = [pl.BlockSpec((tm, k), lambda j, i: (i, 0)),
                pl.BlockSpec((k, tn), lambda j, i: (0, j + off))]
    args = [a, w]
    body = _mm_kernel
    if bias is not None:
        in_specs.append(pl.BlockSpec((1, tn), lambda j, i: (0, j)))
        args.append(bias.reshape(1, n_out).astype(F32))
        body = _mm_sigmoid_kernel
    return pl.pallas_call(
        body,
        grid=(n_out // tn, m // tm),
        in_specs=in_specs,
        out_specs=pl.BlockSpec((tm, tn), lambda j, i: (i, j)),
        out_shape=jax.ShapeDtypeStruct((m, n_out), BF16),
        compiler_params=_cparams(("arbitrary", "arbitrary")),
        name=name,
    )(*args)


def _t5_bias_tile(rel_ref, h, offset, tk, tq):
    r = lax.broadcasted_iota(jnp.int32, (tk, tq), 0)
    c = lax.broadcasted_iota(jnp.int32, (tk, tq), 1)
    n = jnp.maximum(c - r + offset, 0)
    max_exact = N_BUCKETS // 2
    nf = jnp.maximum(n, 1).astype(F32)
    large = max_exact + (jnp.log(nf / max_exact) / math.log(MAX_DISTANCE / max_exact)
                         * (N_BUCKETS - max_exact)).astype(jnp.int32)
    large = jnp.minimum(large, N_BUCKETS - 1)
    bucket = jnp.where(n < max_exact, n, large)
    far = rel_ref[(N_BUCKETS - 1) * DIFF_HEADS + h]
    out = jnp.zeros((tk, tq), F32)
    for j in range(N_BUCKETS - 1):
        out = jnp.where(bucket == j, rel_ref[j * DIFF_HEADS + h] - far, out)
    return out


def _tree(xs, op):
    while len(xs) > 1:
        nxt = [op(xs[i], xs[i + 1]) for i in range(0, len(xs) - 1, 2)]
        if len(xs) % 2:
            nxt.append(xs[-1])
        xs = nxt
    return xs[0]


def _rows_to_8(x, op):
    return _tree([x[i:i + 8] for i in range(0, x.shape[0], 8)], op)


_FAR, _NEAR, _DIAG = 0, 1, 2


def _diff_attn_kernel(rel_ref, lam_ref, qT_ref, k_ref, vT_ref, g_ref, o_ref,
                      bias_sc, s_a, s_b, pb_sc, m_sc, l_sc, acc_sc, *, lam_init):
    h = pl.program_id(1)
    qi = pl.program_id(2)
    tq = qT_ref.shape[-1]
    tk = vT_ref.shape[-1]
    ch = ATT_CHUNK
    nch = tk // ch

    @pl.when(qi == 0)
    def _():
        for c in range(nch):
            bias_sc[0, c * ch:(c + 1) * ch, :] = _t5_bias_tile(rel_ref, h, -c * ch, ch, tq)
            bias_sc[1, c * ch:(c + 1) * ch, :] = _t5_bias_tile(rel_ref, h, tk - c * ch, ch, tq)

    qs = qT_ref[0, 0] * jnp.asarray(DIFF_QK_DIM ** -0.5, BF16)
    row = lax.broadcasted_iota(jnp.int32, qs.shape, 0)
    zero = jnp.zeros_like(qs)
    qmaps = (jnp.where(row < DIFF_QK_DIM, qs, zero), jnp.where(row >= DIFF_QK_DIM, qs, zero))

    m_sc[...] = jnp.full(m_sc.shape, NEG_INF, F32)
    l_sc[...] = jnp.zeros(l_sc.shape, F32)
    acc_sc[...] = jnp.zeros(acc_sc.shape, F32)

    def qk(j, dst):
        base = pl.multiple_of(j * tk, tk)
        kt = k_ref[0, 0, pl.ds(base, tk), :]
        for mi in range(2):
            dst[mi] = jnp.dot(kt, qmaps[mi], preferred_element_type=F32)

    def softmax_pv(src, j, kind):
        vt = vT_ref[0, 0, j]
        for mi in range(2):
            parts = []
            for c in range(nch):
                rows = slice(c * ch, (c + 1) * ch)
                x = src[mi, rows, :]
                if kind != _FAR:
                    x = x + bias_sc[0 if kind == _DIAG else 1, rows, :]
                    if kind == _DIAG:
                        r = lax.broadcasted_iota(jnp.int32, (ch, tq), 0) + c * ch
                        cc = lax.broadcasted_iota(jnp.int32, (ch, tq), 1)
                        x = jnp.where(r <= cc, x, NEG_INF)
                    src[mi, rows, :] = x
                parts.append(_rows_to_8(x, jnp.maximum))
            m_old = m_sc[mi]
            m_new = jnp.maximum(m_old, jnp.max(_tree(parts, jnp.maximum), axis=0, keepdims=True))
            alpha = jnp.exp(m_old - m_new)
            parts = []
            for c in range(nch):
                rows = slice(c * ch, (c + 1) * ch)
                p = jnp.exp(src[mi, rows, :] - m_new)
                parts.append(_rows_to_8(p, jnp.add))
                pb_sc[mi, rows, :] = p.astype(BF16)
            l_sc[mi] = alpha * l_sc[mi] + jnp.sum(_tree(parts, jnp.add), axis=0, keepdims=True)
            pv = jnp.dot(vt, pb_sc[mi], preferred_element_type=F32)
            acc_sc[mi] = alpha * acc_sc[mi] + pv
            m_sc[mi] = m_new

    n_far = jnp.maximum(qi - 1, 0)

    @pl.when(qi == 0)
    def _():
        qk(0, s_a)
        softmax_pv(s_a, 0, _DIAG)

    @pl.when(qi >= 1)
    def _():
        qk(0, s_a)

        def far_pair(i, carry):
            qk(2 * i + 1, s_b)
            softmax_pv(s_a, 2 * i, _FAR)
            qk(2 * i + 2, s_a)
            softmax_pv(s_b, 2 * i + 1, _FAR)
            return carry

        lax.fori_loop(0, n_far // 2, far_pair, 0)

        @pl.when(n_far % 2 == 1)
        def _():
            qk(qi - 1, s_b)
            softmax_pv(s_a, qi - 2, _FAR)
            qk(qi, s_a)
            softmax_pv(s_b, qi - 1, _NEAR)
            softmax_pv(s_a, qi, _DIAG)

        @pl.when(n_far % 2 == 0)
        def _():
            qk(qi, s_b)
            softmax_pv(s_a, qi - 1, _NEAR)
            softmax_pv(s_b, qi, _DIAG)

    lam = lam_ref[0]
    o = acc_sc[0] / l_sc[0] - lam * (acc_sc[1] / l_sc[1])
    ms = jnp.mean(o * o, axis=0, keepdims=True)
    y = o * lax.rsqrt(ms + NORM_EPS) * g_ref[...]
    o_ref[0, 0] = (y * (1.0 - lam_init)).astype(o_ref.dtype)


def _diff_attention(qT, k, vT5, rel_bias, lam, subln_g, lam_init):
    b, h, dqk, s = qT.shape
    tq = tk = DIFF_TILE
    dv = vT5.shape[3]
    grid_spec = pltpu.PrefetchScalarGridSpec(
        num_scalar_prefetch=2,
        grid=(b, h, s // tq),
        in_specs=[
            pl.BlockSpec((1, 1, dqk, tq), lambda bi, hi, qi, *_: (bi, hi, 0, qi)),
            pl.BlockSpec((1, 1, s, dqk), lambda bi, hi, qi, *_: (bi, hi, 0, 0)),
            pl.BlockSpec((1, 1, s // tk, dv, tk), lambda bi, hi, qi, *_: (bi, hi, 0, 0, 0)),
            pl.BlockSpec((dv, 1), lambda bi, hi, qi, *_: (0, 0)),
        ],
        out_specs=pl.BlockSpec((1, 1, dv, tq), lambda bi, hi, qi, *_: (bi, hi, 0, qi)),
        scratch_shapes=[
            pltpu.VMEM((2, tk, tq), F32),
            pltpu.VMEM((2, tk, tq), F32),
            pltpu.VMEM((2, tk, tq), F32),
            pltpu.VMEM((2, tk, tq), BF16),
            pltpu.VMEM((2, 1, tq), F32),
            pltpu.VMEM((2, 1, tq), F32),
            pltpu.VMEM((2, dv, tq), F32),
        ],
    )
    return pl.pallas_call(
        functools.partial(_diff_attn_kernel, lam_init=lam_init),
        grid_spec=grid_spec,
        out_shape=jax.ShapeDtypeStruct((b, h, dv, s), BF16),
        compiler_params=_cparams(("arbitrary", "arbitrary", "arbitrary")),
        name="diff_attention",
    )(rel_bias.reshape(-1).astype(F32), lam.reshape(1).astype(F32), qT, k, vT5, subln_g.reshape(dv, 1).astype(F32))


def _sb_attn_kernel(qT_ref, k_ref, vT_ref, o_ref, z_a, z_b, hl_sc, a_sc, acc_sc, carry_sc):
    qi = pl.program_id(2)
    tq = qT_ref.shape[-1]
    gk = vT_ref.shape[-1]
    tk = SB_TK
    nt = gk // tk
    ch = ATT_CHUNK
    d = qT_ref.shape[2]
    qs = (qT_ref[0, 0].astype(F32) * (d ** -0.5)).astype(BF16)

    sr = lax.broadcasted_iota(jnp.int32, (tk, 2 * tk), 0)
    sc = lax.broadcasted_iota(jnp.int32, (tk, 2 * tk), 1)
    suffix = jnp.where((sc & (tk - 1)) > sr, 1.0, 0.0).astype(BF16)

    acc_sc[...] = jnp.zeros(acc_sc.shape, F32)
    carry_sc[...] = jnp.zeros(carry_sc.shape, F32)

    def qk(g, dst):
        base = pl.multiple_of(g * gk, gk)
        dst[...] = jnp.dot(k_ref[0, 0, pl.ds(base, gk), :], qs, preferred_element_type=F32)

    def strict_mask(row0):
        r = lax.broadcasted_iota(jnp.int32, (ch, tq), 0) + row0
        c = lax.broadcasted_iota(jnp.int32, (ch, tq), 1)
        return r < c

    def process(src, g, diag):
        lk0 = []
        for t in range(nt):
            for c in range(tk // ch):
                row0 = t * tk + c * ch
                z = src[row0:row0 + ch, :]
                lg = jnp.log(1.0 + jnp.exp(-jnp.abs(z)))
                lk = -(jnp.maximum(z, 0.0) + lg)
                if diag:
                    lk = jnp.where(strict_mask(row0), lk, 0.0)
                hi = lk.astype(BF16)
                hl_sc[t, c * ch:(c + 1) * ch, :] = hi
                hl_sc[t, tk + c * ch:tk + (c + 1) * ch, :] = (lk - hi.astype(F32)).astype(BF16)
                src[row0:row0 + ch, :] = z + lk
                if c == 0:
                    lk0.append(lk[0:1, :])
        carry = carry_sc[...]
        for t in reversed(range(nt)):
            after = jnp.dot(suffix, hl_sc[t], preferred_element_type=F32)
            for c in range(tk // ch):
                row0 = t * tk + c * ch
                a = jnp.exp(src[row0:row0 + ch, :] + after[c * ch:(c + 1) * ch, :] + carry)
                if diag:
                    a = jnp.where(strict_mask(row0), a, 0.0)
                a_sc[row0:row0 + ch, :] = a.astype(BF16)
            carry = carry + after[0:1, :] + lk0[t]
        carry_sc[...] = carry
        acc_sc[...] += jnp.dot(vT_ref[0, 0, g], a_sc[...], preferred_element_type=F32)

    qk(qi, z_a)

    @pl.when(qi == 0)
    def _():
        process(z_a, qi, True)

    @pl.when(qi >= 1)
    def _():
        qk(qi - 1, z_b)
        process(z_a, qi, True)
        n_pairs = (qi - 1) // 2

        def pair(i, carry):
            g = qi - 1 - 2 * i
            qk(g - 1, z_a)
            process(z_b, g, False)
            qk(g - 2, z_b)
            process(z_a, g - 1, False)
            return carry

        lax.fori_loop(0, n_pairs, pair, 0)

        @pl.when(qi % 2 == 0)
        def _():
            qk(0, z_a)
            process(z_b, 1, False)
            process(z_a, 0, False)

        @pl.when(qi % 2 == 1)
        def _():
            process(z_b, 0, False)

    o_ref[0, 0] = acc_sc[...].astype(o_ref.dtype)


def _sb_attention(qT, k, vT5):
    b, h, d, s = qT.shape
    tq = gk = SB_GROUP_KEYS
    return pl.pallas_call(
        _sb_attn_kernel,
        grid=(b, h, s // tq),
        in_specs=[
            pl.BlockSpec((1, 1, d, tq), lambda bi, hi, qi: (bi, hi, 0, qi)),
            pl.BlockSpec((1, 1, s, d), lambda bi, hi, qi: (bi, hi, 0, 0)),
            pl.BlockSpec((1, 1, s // gk, d, gk), lambda bi, hi, qi: (bi, hi, 0, 0, 0)),
        ],
        out_specs=pl.BlockSpec((1, 1, d, tq), lambda bi, hi, qi: (bi, hi, 0, qi)),
        out_shape=jax.ShapeDtypeStruct((b, h, d, s), BF16),
        scratch_shapes=[
            pltpu.VMEM((gk, tq), F32),
            pltpu.VMEM((gk, tq), F32),
            pltpu.VMEM((gk // SB_TK, 2 * SB_TK, tq), BF16),
            pltpu.VMEM((gk, tq), BF16),
            pltpu.VMEM((d, tq), F32),
            pltpu.VMEM((1, tq), F32),
        ],
        compiler_params=_cparams(("arbitrary", "arbitrary", "arbitrary")),
        name="sb_attention",
    )(qT, k, vT5)


def _mem_attn_kernel(q_ref, kv_ref, o_ref):
    hd = q_ref.shape[-1]
    d = MEM_DIM
    outs = []
    for hh in range(hd // d):
        q = q_ref[0, :, hh * d:(hh + 1) * d]
        kk = kv_ref[0, :, hh * d:(hh + 1) * d]
        vv = kv_ref[0, :, hd + hh * d:hd + (hh + 1) * d]
        s = lax.dot_general(q, kk, (((1,), (1,)), ((), ())), preferred_element_type=F32) * (d ** -0.5)
        m = jnp.max(s, axis=-1, keepdims=True)
        p = jnp.exp(s - m)
        p = p / jnp.sum(p, axis=-1, keepdims=True)
        outs.append(jnp.dot(p.astype(BF16), vv, preferred_element_type=F32))
    o_ref[0] = jnp.concatenate(outs, axis=-1).astype(o_ref.dtype)


def _mem_attention(q, kv, tm):
    b, s, hd = q.shape
    mlen = kv.shape[1]
    return pl.pallas_call(
        _mem_attn_kernel,
        grid=(b, s // tm),
        in_specs=[pl.BlockSpec((1, tm, hd), lambda bi, i: (bi, i, 0)),
                  pl.BlockSpec((1, mlen, 2 * hd), lambda bi, i: (bi, 0, 0))],
        out_specs=pl.BlockSpec((1, tm, hd), lambda bi, i: (bi, i, 0)),
        out_shape=jax.ShapeDtypeStruct((b, s, hd), BF16),
        compiler_params=_cparams(("arbitrary", "arbitrary")),
        name="mem_attention",
    )(q, kv)


def _merge_kernel(ya_ref, yb_ref, yc_ref, wa_ref, wb_ref, wc_ref, ga_ref, gb_ref, gc_ref, o_ref):
    pa = jnp.dot(ya_ref[...], wa_ref[...], preferred_element_type=F32)
    pb = jnp.dot(yb_ref[...], wb_ref[...], preferred_element_type=F32)
    pc = jnp.dot(yc_ref[...], wc_ref[...], preferred_element_type=F32)
    merged = (ga_ref[...].astype(F32) * pa + gb_ref[...].astype(F32) * pb + gc_ref[...].astype(F32) * pc)
    o_ref[...] = merged.astype(o_ref.dtype)


def _merge(ya, yb, yc, wa, wb, wc, gates, d_model, tm, tn):
    m = ya.shape[0]
    nblk = d_model // tn
    y_spec = lambda arr: pl.BlockSpec((tm, arr.shape[1]), lambda j, i: (i, 0))
    w_spec = lambda arr: pl.BlockSpec((arr.shape[0], tn), lambda j, i: (0, j))
    g_spec = lambda br: pl.BlockSpec((tm, tn), lambda j, i, br=br: (i, br * nblk + j))
    return pl.pallas_call(
        _merge_kernel,
        grid=(nblk, m // tm),
        in_specs=[y_spec(ya), y_spec(yb), y_spec(yc), w_spec(wa), w_spec(wb), w_spec(wc),
                  g_spec(0), g_spec(1), g_spec(2)],
        out_specs=pl.BlockSpec((tm, tn), lambda j, i: (i, j)),
        out_shape=jax.ShapeDtypeStruct((m, d_model), BF16),
        compiler_params=_cparams(("arbitrary", "arbitrary")),
        name="gated_merge",
    )(ya, yb, yc, wa, wb, wc, gates, gates, gates)


def _pack_bf16_pairs(lo, hi):
    lo_bits = lax.bitcast_convert_type(lo.astype(BF16).astype(F32), jnp.uint32)
    hi_bits = lax.bitcast_convert_type(hi.astype(BF16).astype(F32), jnp.uint32)
    return (hi_bits & jnp.uint32(0xFFFF0000)) | (lo_bits >> 16)


def _unpack_bf16_pairs(words):
    lo = lax.bitcast_convert_type(words << 16, F32)
    hi = lax.bitcast_convert_type(words & jnp.uint32(0xFFFF0000), F32)
    return lo, hi


def _outproj_kernel(mg_ref, w_ref, x_ref, g_ref, wrh_ref, wrl_ref, br_ref, x1_ref, h2_ref, lg_ref):
    x1 = x_ref[...] + jnp.dot(mg_ref[...], w_ref[...], preferred_element_type=F32)
    x1_ref[...] = x1
    ms = jnp.mean(x1 * x1, axis=-1, keepdims=True)
    h2 = x1 * lax.rsqrt(ms + NORM_EPS) * g_ref[...]
    h_hi = h2.astype(BF16)
    half = h2.shape[-1] // 2
    h2_ref[...] = _pack_bf16_pairs(h2[:, :half], h2[:, half:])
    h_lo = (h2 - h_hi.astype(F32)).astype(BF16)
    lg = (jnp.dot(h_hi, wrh_ref[...], preferred_element_type=F32)
          + jnp.dot(h_lo, wrh_ref[...], preferred_element_type=F32)
          + jnp.dot(h_hi, wrl_ref[...], preferred_element_type=F32))
    lg_ref[...] = lg + br_ref[...]


def _outproj(merged, w_out, x, g_ffn, wr_hi, wr_lo, br_pad, tm):
    m, d = x.shape
    row = lambda n: pl.BlockSpec((tm, n), lambda i: (i, 0))
    full = lambda arr: pl.BlockSpec(arr.shape, lambda i: (0, 0))
    return pl.pallas_call(
        _outproj_kernel,
        grid=(m // tm,),
        in_specs=[row(d), full(w_out), row(d), full(g_ffn), full(wr_hi), full(wr_lo), full(br_pad)],
        out_specs=[row(d), row(d // 2), row(ROUTER_PAD)],
        out_shape=[jax.ShapeDtypeStruct((m, d), F32), jax.ShapeDtypeStruct((m, d // 2), jnp.uint32),
                   jax.ShapeDtypeStruct((m, ROUTER_PAD), F32)],
        compiler_params=_cparams(("arbitrary",)),
        name="outproj_norm_router",
    )(merged, w_out, x, g_ffn, wr_hi, wr_lo, br_pad)


def _topk_kernel(lg_ref, idx_ref, gate_ref, rank_ref, cnt_ref, run_sc):
    @pl.when(pl.program_id(0) == 0)
    def _():
        run_sc[...] = jnp.zeros(run_sc.shape, F32)

    l = lg_ref[...]
    tm = l.shape[0]
    lane = lax.broadcasted_iota(jnp.int32, l.shape, 1)
    vals, ids = [], []
    for _ in range(TOP_K):
        m = jnp.max(l, axis=-1, keepdims=True)
        idx = jnp.min(jnp.where(l == m, lane, l.shape[-1]), axis=-1, keepdims=True)
        vals.append(m)
        ids.append(idx)
        l = jnp.where(lane == idx, -jnp.inf, l)
    es = [jnp.exp(v - vals[0]) for v in vals]
    den = es[0]
    for e in es[1:]:
        den = den + e

    chosen = jnp.zeros(l.shape, F32)
    for kk in range(TOP_K):
        chosen = jnp.where(lane == ids[kk], 1.0, chosen)
    tr = lax.broadcasted_iota(jnp.int32, (tm, tm), 0)
    tc = lax.broadcasted_iota(jnp.int32, (tm, tm), 1)
    earlier = jnp.where(tc < tr, 1.0, 0.0).astype(BF16)
    before = jnp.dot(earlier, chosen.astype(BF16), preferred_element_type=F32) + run_sc[...]

    idx_out = jnp.zeros(l.shape, jnp.int32)
    gate_out = jnp.zeros(l.shape, F32)
    rank_out = jnp.zeros(l.shape, F32)
    for kk in range(TOP_K):
        idx_out = jnp.where(lane == kk, ids[kk], idx_out)
        gate_out = jnp.where(lane == kk, es[kk] / den, gate_out)
        rk = jnp.sum(jnp.where(lane == ids[kk], before, 0.0), axis=-1, keepdims=True)
        rank_out = jnp.where(lane == kk, rk, rank_out)
    idx_ref[...] = idx_out
    gate_ref[...] = gate_out
    rank_ref[...] = rank_out.astype(jnp.int32)
    run = run_sc[...] + jnp.sum(chosen, axis=0, keepdims=True)
    run_sc[...] = run
    cnt_ref[...] = run.astype(jnp.int32)


def _topk(logits, tm):
    m, n = logits.shape
    spec = pl.BlockSpec((tm, n), lambda i: (i, 0))
    return pl.pallas_call(
        _topk_kernel,
        grid=(m // tm,),
        in_specs=[spec],
        out_specs=[spec, spec, spec, pl.BlockSpec((1, n), lambda i: (0, 0))],
        out_shape=[jax.ShapeDtypeStruct((m, n), jnp.int32), jax.ShapeDtypeStruct((m, n), F32),
                   jax.ShapeDtypeStruct((m, n), jnp.int32), jax.ShapeDtypeStruct((1, n), jnp.int32)],
        scratch_shapes=[pltpu.VMEM((1, n), F32)],
        compiler_params=_cparams(("arbitrary",)),
        name="router_topk",
    )(logits)


def _ffn1_kernel(sb_ref, sj_ref, se_ref, sfirst_ref, svalid_ref,
                 x_ref, wg_ref, wl_ref, bg_ref, bl_ref, o_ref, wg_sc, wl_sc):
    s = pl.program_id(0)

    @pl.when(sfirst_ref[s] == 1)
    def _():
        wg_sc[...] = wg_ref[0].astype(BF16)
        wl_sc[...] = wl_ref[0].astype(BF16)

    @pl.when(svalid_ref[s] == 1)
    def _():
        x_lo, x_hi = _unpack_bf16_pairs(x_ref[...])
        x_lo = x_lo.astype(BF16)
        x_hi = x_hi.astype(BF16)
        half = x_lo.shape[-1]

        def proj(w_sc, b_ref):
            return (jnp.dot(x_lo, w_sc[0:half, :], preferred_element_type=F32)
                    + jnp.dot(x_hi, w_sc[half:, :], preferred_element_type=F32) + b_ref[0])

        a_glu = jnp.minimum(proj(wg_sc, bg_ref), SWIGLU_LIMIT)
        a_lin = jnp.clip(proj(wl_sc, bl_ref), -SWIGLU_LIMIT, SWIGLU_LIMIT)
        act = a_glu * jax.nn.sigmoid(SWIGLU_ALPHA * a_glu) * (a_lin + 1.0)
        o_ref[...] = act.astype(o_ref.dtype)

    @pl.when(svalid_ref[s] == 0)
    def _():
        o_ref[...] = jnp.zeros(o_ref.shape, o_ref.dtype)


def _ffn2_kernel(sb_ref, sj_ref, se_ref, sfirst_ref, svalid_ref,
                 h_ref, w_ref, b_ref, o_ref, w_sc):
    s = pl.program_id(0)

    @pl.when(sfirst_ref[s] == 1)
    def _():
        w_sc[...] = w_ref[0].astype(BF16)

    @pl.when(svalid_ref[s] == 1)
    def _():
        y = jnp.dot(h_ref[...], w_sc[...], preferred_element_type=F32) + b_ref[0]
        half = y.shape[-1] // 2
        o_ref[...] = _pack_bf16_pairs(y[:, :half], y[:, half:])

    @pl.when(svalid_ref[s] == 0)
    def _():
        o_ref[...] = jnp.zeros(o_ref.shape, o_ref.dtype)


def _step_tables(nb, n_tiles, n_blocks):
    n_steps = n_blocks * n_tiles
    per_e = nb * n_tiles
    cum_end = jnp.cumsum(per_e)
    total = cum_end[-1]
    used_blocks = jnp.sum(nb)
    blk_start = jnp.cumsum(nb) - nb
    s_raw = jnp.arange(n_steps, dtype=jnp.int32)
    valid = s_raw < total
    s = jnp.minimum(s_raw, total - 1)
    e = jnp.minimum(jnp.searchsorted(cum_end, s, side="right"), N_EXPERTS - 1).astype(jnp.int32)
    r = s - (cum_end[e] - per_e[e])
    nbe = jnp.maximum(nb[e], 1)
    j = r // nbe
    bi = r % nbe
    first = jnp.logical_and(bi == 0, valid)
    spare = jnp.maximum(n_blocks - used_blocks, 1)
    r_pad = jnp.maximum(s_raw - total, 0)
    blk = jnp.where(valid, blk_start[e] + bi, used_blocks + r_pad % spare)
    j = jnp.where(valid, j, r_pad // spare)
    i32 = lambda a: a.astype(jnp.int32)
    return i32(blk), i32(j), e, i32(first), i32(valid)


def _expert_ffn(xs, nb, w1, b1, w2, b2):
    p = xs.shape[0]
    d = 2 * xs.shape[1]
    ff = w2.shape[1]
    n_blocks = p // MOE_BLK
    nj1 = ff // FFN1_TN
    nj2 = d // FFN2_TN
    b1r = b1.reshape(N_EXPERTS, 1, 2 * ff)
    b2r = b2.reshape(N_EXPERTS, 1, d)

    t1 = _step_tables(nb, nj1, n_blocks)
    h = pl.pallas_call(
        _ffn1_kernel,
        grid_spec=pltpu.PrefetchScalarGridSpec(
            num_scalar_prefetch=5,
            grid=(n_blocks * nj1,),
            in_specs=[
                pl.BlockSpec((MOE_BLK, d // 2), lambda s, sb, sj, se, sf, sv: (sb[s], 0)),
                pl.BlockSpec((1, d, FFN1_TN), lambda s, sb, sj, se, sf, sv: (se[s], 0, sj[s])),
                pl.BlockSpec((1, d, FFN1_TN), lambda s, sb, sj, se, sf, sv: (se[s], 0, nj1 + sj[s])),
                pl.BlockSpec((1, 1, FFN1_TN), lambda s, sb, sj, se, sf, sv: (se[s], 0, sj[s])),
                pl.BlockSpec((1, 1, FFN1_TN), lambda s, sb, sj, se, sf, sv: (se[s], 0, nj1 + sj[s])),
            ],
            out_specs=pl.BlockSpec((MOE_BLK, FFN1_TN), lambda s, sb, sj, se, sf, sv: (sb[s], sj[s])),
            scratch_shapes=[pltpu.VMEM((d, FFN1_TN), BF16), pltpu.VMEM((d, FFN1_TN), BF16)],
        ),
        out_shape=jax.ShapeDtypeStruct((p, ff), BF16),
        compiler_params=_cparams(("arbitrary",)),
        name="expert_ffn_up",
    )(*t1, xs, w1, w1, b1r, b1r)

    t2 = _step_tables(nb, nj2, n_blocks)
    ys = pl.pallas_call(
        _ffn2_kernel,
        grid_spec=pltpu.PrefetchScalarGridSpec(
            num_scalar_prefetch=5,
            grid=(n_blocks * nj2,),
            in_specs=[
                pl.BlockSpec((MOE_BLK, ff), lambda s, sb, sj, se, sf, sv: (sb[s], 0)),
                pl.BlockSpec((1, ff, FFN2_TN), lambda s, sb, sj, se, sf, sv: (se[s], 0, sj[s])),
                pl.BlockSpec((1, 1, FFN2_TN), lambda s, sb, sj, se, sf, sv: (se[s], 0, sj[s])),
            ],
            out_specs=pl.BlockSpec((MOE_BLK, FFN2_TN // 2), lambda s, sb, sj, se, sf, sv: (sb[s], sj[s])),
            scratch_shapes=[pltpu.VMEM((ff, FFN2_TN), BF16)],
        ),
        out_shape=jax.ShapeDtypeStruct((p, d // 2), jnp.uint32),
        compiler_params=_cparams(("arbitrary",)),
        name="expert_ffn_down",
    )(*t2, h, w2, b2r)
    return ys


def _dispatch_kernel(dest_ref, h_ref, xs_in_ref, xs_ref, sem):
    del xs_in_ref
    tm = h_ref.shape[0]
    base = pl.program_id(0) * (tm * TOP_K)

    def row_copy(r, kk):
        dst = dest_ref[base + r * TOP_K + kk]
        return pltpu.make_async_copy(h_ref.at[pl.ds(r, 1), :], xs_ref.at[pl.ds(dst, 1), :], sem)

    def issue(r, carry):
        for kk in range(TOP_K):
            row_copy(r, kk).start()
        return carry

    lax.fori_loop(0, tm, issue, 0, unroll=DMA_UNROLL)

    def drain(r, carry):
        for kk in range(TOP_K):
            row_copy(r, kk).wait()
        return carry

    lax.fori_loop(0, tm, drain, 0, unroll=DMA_UNROLL)


def _dispatch(h2p, dest_flat, p, tm):
    t, w = h2p.shape
    xs0 = jnp.zeros((p, w), h2p.dtype)
    return pl.pallas_call(
        _dispatch_kernel,
        grid_spec=pltpu.PrefetchScalarGridSpec(
            num_scalar_prefetch=1,
            grid=(t // tm,),
            in_specs=[pl.BlockSpec((tm, w), lambda i, dest: (i, 0)),
                      pl.BlockSpec(memory_space=pl.ANY)],
            out_specs=pl.BlockSpec(memory_space=pl.ANY),
            scratch_shapes=[pltpu.SemaphoreType.DMA(())],
        ),
        out_shape=jax.ShapeDtypeStruct((p, w), h2p.dtype),
        input_output_aliases={2: 0},
        compiler_params=_cparams(("arbitrary",)),
        name="moe_dispatch",
    )(dest_flat, h2p, xs0)


def _combine_kernel(dest_ref, x1_ref, gate_ref, g_ref, ys_ref, o_ref, buf, sem):
    i = pl.program_id(0)
    n = pl.num_programs(0)
    tm = x1_ref.shape[0]
    tn2 = FFN2_TN // 2

    def row_copy(blk, slot, r, kk):
        src = dest_ref[(blk * tm + r) * TOP_K + kk]
        return pltpu.make_async_copy(ys_ref.at[pl.ds(src, 1), :], buf.at[slot, pl.ds(kk * tm + r, 1), :],
                                     sem.at[slot])

    def issue_block(blk, slot):
        def body(r, carry):
            for kk in range(TOP_K):
                row_copy(blk, slot, r, kk).start()
            return carry
        lax.fori_loop(0, tm, body, 0, unroll=DMA_UNROLL)

    @pl.when(i == 0)
    def _():
        issue_block(0, 0)

    @pl.when(i + 1 < n)
    def _():
        issue_block(i + 1, (i + 1) % 2)

    slot = i % 2

    def drain(r, carry):
        for kk in range(TOP_K):
            row_copy(i, slot, r, kk).wait()
        return carry

    lax.fori_loop(0, tm, drain, 0, unroll=DMA_UNROLL)

    gates = gate_ref[...]
    pieces = [None] * (2 * (2 * buf.shape[-1] // FFN2_TN))
    for kk in range(TOP_K):
        lo, hi = _unpack_bf16_pairs(buf[slot, kk * tm:(kk + 1) * tm, :])
        g = gates[:, kk:kk + 1]
        for j in range(len(pieces) // 2):
            for half, part in enumerate((lo, hi)):
                term = g * part[:, j * tn2:(j + 1) * tn2]
                idx = 2 * j + half
                pieces[idx] = term if pieces[idx] is None else pieces[idx] + term
    x = x1_ref[...] + jnp.concatenate(pieces, axis=-1)
    ms = jnp.mean(x * x, axis=-1, keepdims=True)
    o_ref[...] = (x * lax.rsqrt(ms + NORM_EPS) * g_ref[...]).astype(o_ref.dtype)


def _combine(x1, gates, dest_flat, ys, g_final, tm):
    t, d = x1.shape
    w = ys.shape[1]
    return pl.pallas_call(
        _combine_kernel,
        grid_spec=pltpu.PrefetchScalarGridSpec(
            num_scalar_prefetch=1,
            grid=(t // tm,),
            in_specs=[pl.BlockSpec((tm, d), lambda i, dest: (i, 0)),
                      pl.BlockSpec((tm, gates.shape[1]), lambda i, dest: (i, 0)),
                      pl.BlockSpec((1, d), lambda i, dest: (0, 0)),
                      pl.BlockSpec(memory_space=pl.ANY)],
            out_specs=pl.BlockSpec((tm, d), lambda i, dest: (i, 0)),
            scratch_shapes=[pltpu.VMEM((2, TOP_K * tm, w), ys.dtype), pltpu.SemaphoreType.DMA((2,))],
        ),
        out_shape=jax.ShapeDtypeStruct((t, d), F32),
        compiler_params=_cparams(("arbitrary",)),
        name="moe_combine_final_norm",
    )(dest_flat, x1, gates, g_final.reshape(1, d).astype(F32), ys)


def _heads_T(a, b, s, h, d):
    return a.reshape(b, s, h, d).transpose(0, 2, 3, 1)


def _heads(a, b, s, h, d):
    return a.reshape(b, s, h, d).transpose(0, 2, 1, 3)


def _heads_T_tiled(a, b, s, h, d, tk):
    return a.reshape(b, s // tk, tk, h, d).transpose(0, 3, 1, 4, 2)


def kernel(x, mem, g_mix, w_in, b_gate, rel_bias, lambda_q1, lambda_k1, lambda_q2, lambda_k2, diff_subln_g, g_mem, w_mem_kv, w_br_diff, w_br_sb, w_br_mem, w_out, g_ffn, w_router, b_router, w_exp1, b_exp1, w_exp2, b_exp2, g_final):
    b, s, d = x.shape
    t = b * s
    depth = g_mix.shape[0]
    dqk_w = DIFF_HEADS * 2 * DIFF_QK_DIM
    dv_w = DIFF_HEADS * DIFF_V_DIM
    sb_w = SB_HEADS * SB_DIM
    mem_w = MEM_HEADS * MEM_DIM
    qkv_w = 2 * dqk_w + dv_w + 3 * sb_w + mem_w
    mlen = mem.shape[1]
    tm = min(ROW_TILE, s)

    xf = x.reshape(t, d)
    for l in range(depth):
        w_in_b = w_in[l].astype(BF16)
        hmix = _rmsnorm(xf, g_mix[l], BF16, tm)
        qkv = _matmul(hmix, w_in_b, 0, qkv_w, name="in_proj_qkv")
        gates = _matmul(hmix, w_in_b, qkv_w, 3 * d, bias=b_gate[l], name="in_proj_gates")

        c0 = 0
        dq = qkv[:, c0:c0 + dqk_w]; c0 += dqk_w
        dk = qkv[:, c0:c0 + dqk_w]; c0 += dqk_w
        dvv = qkv[:, c0:c0 + dv_w]; c0 += dv_w
        sq = qkv[:, c0:c0 + sb_w]; c0 += sb_w
        sk = qkv[:, c0:c0 + sb_w]; c0 += sb_w
        sv = qkv[:, c0:c0 + sb_w]; c0 += sb_w
        mq = qkv[:, c0:c0 + mem_w]

        lam_init = 0.8 - 0.6 * math.exp(-0.3 * l)
        lam = (jnp.exp(jnp.sum(lambda_q1[l].astype(F32) * lambda_k1[l].astype(F32)))
               - jnp.exp(jnp.sum(lambda_q2[l].astype(F32) * lambda_k2[l].astype(F32))) + lam_init)
        ya_t = _diff_attention(_heads_T(dq, b, s, DIFF_HEADS, 2 * DIFF_QK_DIM),
                               _heads(dk, b, s, DIFF_HEADS, 2 * DIFF_QK_DIM),
                               _heads_T_tiled(dvv, b, s, DIFF_HEADS, DIFF_V_DIM, DIFF_TILE),
                               rel_bias, lam, diff_subln_g[l], lam_init)
        yb_t = _sb_attention(_heads_T(sq, b, s, SB_HEADS, SB_DIM),
                             _heads(sk, b, s, SB_HEADS, SB_DIM),
                             _heads_T_tiled(sv, b, s, SB_HEADS, SB_DIM, SB_GROUP_KEYS))
        ya = ya_t.transpose(0, 3, 1, 2).reshape(t, dv_w)
        yb = yb_t.transpose(0, 3, 1, 2).reshape(t, sb_w)

        hmem = _rmsnorm(mem.reshape(b * mlen, d), g_mem[l], BF16, min(ROW_TILE, b * mlen))
        kv = _matmul(hmem, w_mem_kv[l].astype(BF16), 0, 2 * mem_w, name="mem_kv_proj")
        yc = _mem_attention(mq.reshape(b, s, mem_w), kv.reshape(b, mlen, 2 * mem_w), tm).reshape(t, mem_w)

        merged = _merge(ya, yb, yc, w_br_diff[l].astype(BF16), w_br_sb[l].astype(BF16),
                        w_br_mem[l].astype(BF16), gates, d, tm, MM_TN)

        wr = jnp.pad(w_router[l].astype(F32), ((0, 0), (0, ROUTER_PAD - N_EXPERTS)))
        wr_hi = wr.astype(BF16)
        wr_lo = (wr - wr_hi.astype(F32)).astype(BF16)
        br_pad = jnp.pad(b_router[l].astype(F32), (0, ROUTER_PAD - N_EXPERTS),
                         constant_values=-jnp.inf).reshape(1, ROUTER_PAD)
        x1, h2, logits = _outproj(merged, w_out[l].astype(BF16), xf, g_ffn[l].reshape(1, d).astype(F32),
                                  wr_hi, wr_lo, br_pad, tm)
        top_i, top_g, rank, counts = _topk(logits, tm)

        n = t * TOP_K
        counts = counts[0, :N_EXPERTS]
        nb = (counts + MOE_BLK - 1) // MOE_BLK
        pad_start = (jnp.cumsum(nb) - nb) * MOE_BLK
        dest = (pad_start[top_i[:, :TOP_K]] + rank[:, :TOP_K]).reshape(n)
        n_blocks = n // MOE_BLK + N_EXPERTS
        p = n_blocks * MOE_BLK

        xs = _dispatch(h2, dest, p, min(DMA_ROWS, s))
        ys = _expert_ffn(xs, nb, w_exp1[l], b_exp1[l], w_exp2[l], b_exp2[l])
        assert l + 1 == depth, "combine is fused with the final norm: single-layer trunk"
        return _combine(x1, top_g, dest, ys, g_final, min(DMA_ROWS, s)).reshape(b, s, d)
```

```python
import functools
import math

import jax
import jax.numpy as jnp
from jax import lax
from jax.experimental import pallas as pl
from jax.experimental.pallas import tpu as pltpu

F32 = jnp.float32
BF16 = jnp.bfloat16

DIFF_HEADS = 8
DIFF_QK_DIM = 64
DIFF_V_DIM = 128
SB_HEADS = 8
SB_DIM = 128
MEM_HEADS = 4
MEM_DIM = 256
N_BUCKETS = 32
MAX_DISTANCE = 128
N_EXPERTS = 32
TOP_K = 4
SWIGLU_LIMIT = 7.0
SWIGLU_ALPHA = 1.702
NORM_EPS = 1e-6
NEG_INF = -1e30
LOG2E = math.log2(math.e)

LANES = 128
VMEM_LIMIT_BYTES = 56 * 1024 * 1024

ROW_TILE = 512
MM_TM = 1024
MM_TN = 1024
DIFF_TILE = 512
ATT_CHUNK = 64
SB_TK = 128
SB_GROUP_KEYS = 512
MOE_BLK = 512
FFN1_TN = 1024
FFN2_TN = 1024
ROUTER_PAD = LANES
DMA_ROWS = 256
DMA_UNROLL = 8


def _cparams(sem):
    return pltpu.CompilerParams(dimension_semantics=sem, vmem_limit_bytes=VMEM_LIMIT_BYTES)


def _rmsnorm_kernel(x_ref, g_ref, o_ref):
    x = x_ref[...].astype(F32)
    ms = jnp.mean(x * x, axis=-1, keepdims=True)
    o_ref[...] = (x * lax.rsqrt(ms + NORM_EPS) * g_ref[...]).astype(o_ref.dtype)


def _rmsnorm(x, g, out_dtype, tm):
    m, d = x.shape
    return pl.pallas_call(
        _rmsnorm_kernel,
        grid=(m // tm,),
        in_specs=[pl.BlockSpec((tm, d), lambda i: (i, 0)),
                  pl.BlockSpec((1, d), lambda i: (0, 0))],
        out_specs=pl.BlockSpec((tm, d), lambda i: (i, 0)),
        out_shape=jax.ShapeDtypeStruct((m, d), out_dtype),
        compiler_params=_cparams(("arbitrary",)),
        name="rmsnorm",
    )(x, g.reshape(1, d).astype(F32))


def _mm_kernel(a_ref, w_ref, o_ref):
    o_ref[...] = jnp.dot(a_ref[...], w_ref[...], preferred_element_type=F32).astype(o_ref.dtype)


def _mm_sigmoid_kernel(a_ref, w_ref, b_ref, o_ref):
    acc = jnp.dot(a_ref[...], w_ref[...], preferred_element_type=F32)
    o_ref[...] = jax.nn.sigmoid(acc + b_ref[...]).astype(o_ref.dtype)


def _matmul(a, w, col_off, n_out, bias=None, tm=MM_TM, tn=MM_TN, name="matmul"):
    m, k = a.shape
    tm = min(tm, m)
    off = col_off // tn
    in_specs = [pl.BlockSpec((tm, k), lambda j, i: (i, 0)),
                pl.BlockSpec((k, tn), lambda j, i: (0, j + off))]
    args = [a, w]
    body = _mm_kernel
    if bias is not None:
        in_specs.append(pl.BlockSpec((1, tn), lambda j, i: (0, j)))
        args.append(bias.reshape(1, n_out).astype(F32))
        body = _mm_sigmoid_kernel
    return pl.pallas_call(
        body,
        grid=(n_out // tn, m // tm),
        in_specs=in_specs,
        out_specs=pl.BlockSpec((tm, tn), lambda j, i: (i, j)),
        out_shape=jax.ShapeDtypeStruct((m, n_out), BF16),
        compiler_params=_cparams(("arbitrary", "arbitrary")),
        name=name,
    )(*args)


def _t5_bias_tile(rel_ref, h, offset, tk, tq):
    r = lax.broadcasted_iota(jnp.int32, (tk, tq), 0)
    c = lax.broadcasted_iota(jnp.int32, (tk, tq), 1)
    n = jnp.maximum(c - r + offset, 0)
    max_exact = N_BUCKETS // 2
    nf = jnp.maximum(n, 1).astype(F32)
    large = max_exact + (jnp.log(nf / max_exact) / math.log(MAX_DISTANCE / max_exact)
                         * (N_BUCKETS - max_exact)).astype(jnp.int32)
    large = jnp.minimum(large, N_BUCKETS - 1)
    bucket = jnp.where(n < max_exact, n, large)
    far = rel_ref[(N_BUCKETS - 1) * DIFF_HEADS + h]
    out = jnp.zeros((tk, tq), F32)
    for j in range(N_BUCKETS - 1):
        out = jnp.where(bucket == j, (rel_ref[j * DIFF_HEADS + h] - far) * LOG2E, out)
    return out


def _tree(xs, op):
    while len(xs) > 1:
        nxt = [op(xs[i], xs[i + 1]) for i in range(0, len(xs) - 1, 2)]
        if len(xs) % 2:
            nxt.append(xs[-1])
        xs = nxt
    return xs[0]


def _rows_to_8(x, op):
    return _tree([x[i:i + 8] for i in range(0, x.shape[0], 8)], op)


_FAR, _NEAR, _DIAG = 0, 1, 2


def _diff_attn_kernel(rel_ref, lam_ref, qT_ref, k_ref, vT_ref, g_ref, o_ref,
                      bias_sc, s_a, s_b, pb_sc, m_sc, l_sc, acc_sc, *, lam_init):
    h = pl.program_id(1)
    qi = pl.program_id(2)
    tq = qT_ref.shape[-1]
    tk = vT_ref.shape[-1]
    ch = ATT_CHUNK
    nch = tk // ch

    @pl.when(qi == 0)
    def _():
        for c in range(nch):
            bias_sc[0, c * ch:(c + 1) * ch, :] = _t5_bias_tile(rel_ref, h, -c * ch, ch, tq)
            bias_sc[1, c * ch:(c + 1) * ch, :] = _t5_bias_tile(rel_ref, h, tk - c * ch, ch, tq)

    qs = (qT_ref[0, 0].astype(F32) * (DIFF_QK_DIM ** -0.5 * LOG2E)).astype(BF16)
    row = lax.broadcasted_iota(jnp.int32, qs.shape, 0)
    zero = jnp.zeros_like(qs)
    qmaps = (jnp.where(row < DIFF_QK_DIM, qs, zero), jnp.where(row >= DIFF_QK_DIM, qs, zero))

    m_sc[...] = jnp.full(m_sc.shape, NEG_INF, F32)
    l_sc[...] = jnp.zeros(l_sc.shape, F32)
    acc_sc[...] = jnp.zeros(acc_sc.shape, F32)

    def qk(j, dst):
        base = pl.multiple_of(j * tk, tk)
        kt = k_ref[0, 0, pl.ds(base, tk), :]
        for mi in range(2):
            dst[mi] = jnp.dot(kt, qmaps[mi], preferred_element_type=F32)

    def softmax_pv(src, j, kind):
        vt = vT_ref[0, 0, j]
        for mi in range(2):
            parts = []
            for c in range(nch):
                rows = slice(c * ch, (c + 1) * ch)
                x = src[mi, rows, :]
                if kind != _FAR:
                    x = x + bias_sc[0 if kind == _DIAG else 1, rows, :]
                    if kind == _DIAG:
                        r = lax.broadcasted_iota(jnp.int32, (ch, tq), 0) + c * ch
                        cc = lax.broadcasted_iota(jnp.int32, (ch, tq), 1)
                        x = jnp.where(r <= cc, x, NEG_INF)
                    src[mi, rows, :] = x
                parts.append(_rows_to_8(x, jnp.maximum))
            m_old = m_sc[mi]
            m_new = jnp.maximum(m_old, jnp.max(_tree(parts, jnp.maximum), axis=0, keepdims=True))
            alpha = jnp.exp2(m_old - m_new)
            parts = []
            for c in range(nch):
                rows = slice(c * ch, (c + 1) * ch)
                p = jnp.exp2(src[mi, rows, :] - m_new)
                parts.append(_rows_to_8(p, jnp.add))
                pb_sc[mi, rows, :] = p.astype(BF16)
            l_sc[mi] = alpha * l_sc[mi] + jnp.sum(_tree(parts, jnp.add), axis=0, keepdims=True)
            pv = jnp.dot(vt, pb_sc[mi], preferred_element_type=F32)
            acc_sc[mi] = alpha * acc_sc[mi] + pv
            m_sc[mi] = m_new

    n_far = jnp.maximum(qi - 1, 0)

    @pl.when(qi == 0)
    def _():
        qk(0, s_a)
        softmax_pv(s_a, 0, _DIAG)

    @pl.when(qi >= 1)
    def _():
        qk(0, s_a)

        def far_pair(i, carry):
            qk(2 * i + 1, s_b)
            softmax_pv(s_a, 2 * i, _FAR)
            qk(2 * i + 2, s_a)
            softmax_pv(s_b, 2 * i + 1, _FAR)
            return carry

        lax.fori_loop(0, n_far // 2, far_pair, 0)

        @pl.when(n_far % 2 == 1)
        def _():
            qk(qi - 1, s_b)
            softmax_pv(s_a, qi - 2, _FAR)
            qk(qi, s_a)
            softmax_pv(s_b, qi - 1, _NEAR)
            softmax_pv(s_a, qi, _DIAG)

        @pl.when(n_far % 2 == 0)
        def _():
            qk(qi, s_b)
            softmax_pv(s_a, qi - 1, _NEAR)
            softmax_pv(s_b, qi, _DIAG)

    lam = lam_ref[0]
    o = acc_sc[0] / l_sc[0] - lam * (acc_sc[1] / l_sc[1])
    ms = jnp.mean(o * o, axis=0, keepdims=True)
    y = o * lax.rsqrt(ms + NORM_EPS) * g_ref[...]
    o_ref[0, 0] = (y * (1.0 - lam_init)).astype(o_ref.dtype)


def _diff_attention(qT, k, vT5, rel_bias, lam, subln_g, lam_init):
    b, h, dqk, s = qT.shape
    tq = tk = DIFF_TILE
    dv = vT5.shape[3]
    grid_spec = pltpu.PrefetchScalarGridSpec(
        num_scalar_prefetch=2,
        grid=(b, h, s // tq),
        in_specs=[
            pl.BlockSpec((1, 1, dqk, tq), lambda bi, hi, qi, *_: (bi, hi, 0, qi)),
            pl.BlockSpec((1, 1, s, dqk), lambda bi, hi, qi, *_: (bi, hi, 0, 0)),
            pl.BlockSpec((1, 1, s // tk, dv, tk), lambda bi, hi, qi, *_: (bi, hi, 0, 0, 0)),
            pl.BlockSpec((dv, 1), lambda bi, hi, qi, *_: (0, 0)),
        ],
        out_specs=pl.BlockSpec((1, 1, dv, tq), lambda bi, hi, qi, *_: (bi, hi, 0, qi)),
        scratch_shapes=[
            pltpu.VMEM((2, tk, tq), F32),
            pltpu.VMEM((2, tk, tq), F32),
            pltpu.VMEM((2, tk, tq), F32),
            pltpu.VMEM((2, tk, tq), BF16),
            pltpu.VMEM((2, 1, tq), F32),
            pltpu.VMEM((2, 1, tq), F32),
            pltpu.VMEM((2, dv, tq), F32),
        ],
    )
    return pl.pallas_call(
        functools.partial(_diff_attn_kernel, lam_init=lam_init),
        grid_spec=grid_spec,
        out_shape=jax.ShapeDtypeStruct((b, h, dv, s), BF16),
        compiler_params=_cparams(("arbitrary", "arbitrary", "arbitrary")),
        name="diff_attention",
    )(rel_bias.reshape(-1).astype(F32), lam.reshape(1).astype(F32), qT, k, vT5, subln_g.reshape(dv, 1).astype(F32))


def _sb_attn_kernel(qT_ref, k_ref, vT_ref, o_ref, z_a, z_b, hl_sc, a_sc, acc_sc, carry_sc):
    qi = pl.program_id(2)
    tq = qT_ref.shape[-1]
    gk = vT_ref.shape[-1]
    tk = SB_TK
    nt = gk // tk
    ch = ATT_CHUNK
    d = qT_ref.shape[2]
    qs = (qT_ref[0, 0].astype(F32) * (d ** -0.5)).astype(BF16)

    sr = lax.broadcasted_iota(jnp.int32, (tk, 2 * tk), 0)
    sc = lax.broadcasted_iota(jnp.int32, (tk, 2 * tk), 1)
    suffix = jnp.where((sc & (tk - 1)) > sr, 1.0, 0.0).astype(BF16)

    acc_sc[...] = jnp.zeros(acc_sc.shape, F32)
    carry_sc[...] = jnp.zeros(carry_sc.shape, F32)

    def qk(g, dst):
        base = pl.multiple_of(g * gk, gk)
        dst[...] = jnp.dot(k_ref[0, 0, pl.ds(base, gk), :], qs, preferred_element_type=F32)

    def strict_mask(row0):
        r = lax.broadcasted_iota(jnp.int32, (ch, tq), 0) + row0
        c = lax.broadcasted_iota(jnp.int32, (ch, tq), 1)
        return r < c

    def process(src, g, diag):
        lk0 = []
        for t in range(nt):
            for c in range(tk // ch):
                row0 = t * tk + c * ch
                z = src[row0:row0 + ch, :]
                neg_abs = lax.bitcast_convert_type(
                    lax.bitcast_convert_type(z, jnp.uint32) | jnp.uint32(0x80000000), F32)
                lg = jnp.log(1.0 + jnp.exp(neg_abs))
                w = jnp.minimum(z, 0.0) - lg
                lk = w - z
                if diag:
                    lk = jnp.where(strict_mask(row0), lk, 0.0)
                hi = lax.bitcast_convert_type(
                    lax.bitcast_convert_type(lk, jnp.uint32) & jnp.uint32(0xFFFF0000), F32)
                hl_sc[t, c * ch:(c + 1) * ch, :] = hi.astype(BF16)
                hl_sc[t, tk + c * ch:tk + (c + 1) * ch, :] = (lk - hi).astype(BF16)
                src[row0:row0 + ch, :] = w
                if c == 0:
                    lk0.append(lk[0:1, :])
        carry = carry_sc[...]
        for t in reversed(range(nt)):
            after = jnp.dot(suffix, hl_sc[t], preferred_element_type=F32)
            for c in range(tk // ch):
                row0 = t * tk + c * ch
                a = jnp.exp(src[row0:row0 + ch, :] + after[c * ch:(c + 1) * ch, :] + carry)
                if diag:
                    a = jnp.where(strict_mask(row0), a, 0.0)
                a_sc[row0:row0 + ch, :] = a.astype(BF16)
            carry = carry + after[0:1, :] + lk0[t]
        carry_sc[...] = carry
        acc_sc[...] += jnp.dot(vT_ref[0, 0, g], a_sc[...], preferred_element_type=F32)

    qk(qi, z_a)

    @pl.when(qi == 0)
    def _():
        process(z_a, qi, True)

    @pl.when(qi >= 1)
    def _():
        qk(qi - 1, z_b)
        process(z_a, qi, True)
        n_pairs = (qi - 1) // 2

        def pair(i, carry):
            g = qi - 1 - 2 * i
            qk(g - 1, z_a)
            process(z_b, g, False)
            qk(g - 2, z_b)
            process(z_a, g - 1, False)
            return carry

        lax.fori_loop(0, n_pairs, pair, 0)

        @pl.when(qi % 2 == 0)
        def _():
            qk(0, z_a)
            process(z_b, 1, False)
            process(z_a, 0, False)

        @pl.when(qi % 2 == 1)
        def _():
            process(z_b, 0, False)

    o_ref[0, 0] = acc_sc[...].astype(o_ref.dtype)


def _sb_attention(qT, k, vT5):
    b, h, d, s = qT.shape
    tq = gk = SB_GROUP_KEYS
    return pl.pallas_call(
        _sb_attn_kernel,
        grid=(b, h, s // tq),
        in_specs=[
            pl.BlockSpec((1, 1, d, tq), lambda bi, hi, qi: (bi, hi, 0, qi)),
            pl.BlockSpec((1, 1, s, d), lambda bi, hi, qi: (bi, hi, 0, 0)),
            pl.BlockSpec((1, 1, s // gk, d, gk), lambda bi, hi, qi: (bi, hi, 0, 0, 0)),
        ],
        out_specs=pl.BlockSpec((1, 1, d, tq), lambda bi, hi, qi: (bi, hi, 0, qi)),
        out_shape=jax.ShapeDtypeStruct((b, h, d, s), BF16),
        scratch_shapes=[
            pltpu.VMEM((gk, tq), F32),
            pltpu.VMEM((gk, tq), F32),
            pltpu.VMEM((gk // SB_TK, 2 * SB_TK, tq), BF16),
            pltpu.VMEM((gk, tq), BF16),
            pltpu.VMEM((d, tq), F32),
            pltpu.VMEM((1, tq), F32),
        ],
        compiler_params=_cparams(("arbitrary", "arbitrary", "arbitrary")),
        name="sb_attention",
    )(qT, k, vT5)


def _mem_attn_kernel(q_ref, kv_ref, o_ref):
    hd = q_ref.shape[-1]
    d = MEM_DIM
    outs = []
    for hh in range(hd // d):
        q = q_ref[0, :, hh * d:(hh + 1) * d]
        kk = kv_ref[0, :, hh * d:(hh + 1) * d]
        vv = kv_ref[0, :, hd + hh * d:hd + (hh + 1) * d]
        s = lax.dot_general(q, kk, (((1,), (1,)), ((), ())), preferred_element_type=F32) * (d ** -0.5)
        m = jnp.max(s, axis=-1, keepdims=True)
        p = jnp.exp(s - m)
        p = p / jnp.sum(p, axis=-1, keepdims=True)
        outs.append(jnp.dot(p.astype(BF16), vv, preferred_element_type=F32))
    o_ref[0] = jnp.concatenate(outs, axis=-1).astype(o_ref.dtype)


def _mem_attention(q, kv, tm):
    b, s, hd = q.shape
    mlen = kv.shape[1]
    return pl.pallas_call(
        _mem_attn_kernel,
        grid=(b, s // tm),
        in_specs=[pl.BlockSpec((1, tm, hd), lambda bi, i: (bi, i, 0)),
                  pl.BlockSpec((1, mlen, 2 * hd), lambda bi, i: (bi, 0, 0))],
        out_specs=pl.BlockSpec((1, tm, hd), lambda bi, i: (bi, i, 0)),
        out_shape=jax.ShapeDtypeStruct((b, s, hd), BF16),
        compiler_params=_cparams(("arbitrary", "arbitrary")),
        name="mem_attention",
    )(q, kv)


def _merge_kernel(ya_ref, yb_ref, yc_ref, wa_ref, wb_ref, wc_ref, ga_ref, gb_ref, gc_ref, o_ref):
    pa = jnp.dot(ya_ref[...], wa_ref[...], preferred_element_type=F32)
    pb = jnp.dot(yb_ref[...], wb_ref[...], preferred_element_type=F32)
    pc = jnp.dot(yc_ref[...], wc_ref[...], preferred_element_type=F32)
    merged = (ga_ref[...].astype(F32) * pa + gb_ref[...].astype(F32) * pb + gc_ref[...].astype(F32) * pc)
    o_ref[...] = merged.astype(o_ref.dtype)


def _merge(ya, yb, yc, wa, wb, wc, gates, d_model, tm, tn):
    m = ya.shape[0]
    nblk = d_model // tn
    y_spec = lambda arr: pl.BlockSpec((tm, arr.shape[1]), lambda j, i: (i, 0))
    w_spec = lambda arr: pl.BlockSpec((arr.shape[0], tn), lambda j, i: (0, j))
    g_spec = lambda br: pl.BlockSpec((tm, tn), lambda j, i, br=br: (i, br * nblk + j))
    return pl.pallas_call(
        _merge_kernel,
        grid=(nblk, m // tm),
        in_specs=[y_spec(ya), y_spec(yb), y_spec(yc), w_spec(wa), w_spec(wb), w_spec(wc),
                  g_spec(0), g_spec(1), g_spec(2)],
        out_specs=pl.BlockSpec((tm, tn), lambda j, i: (i, j)),
        out_shape=jax.ShapeDtypeStruct((m, d_model), BF16),
        compiler_params=_cparams(("arbitrary", "arbitrary")),
        name="gated_merge",
    )(ya, yb, yc, wa, wb, wc, gates, gates, gates)


def _pack_bf16_pairs(lo, hi):
    lo_bits = lax.bitcast_convert_type(lo.astype(BF16).astype(F32), jnp.uint32)
    hi_bits = lax.bitcast_convert_type(hi.astype(BF16).astype(F32), jnp.uint32)
    return (hi_bits & jnp.uint32(0xFFFF0000)) | (lo_bits >> 16)


def _unpack_bf16_pairs(words):
    lo = lax.bitcast_convert_type(words << 16, F32)
    hi = lax.bitcast_convert_type(words & jnp.uint32(0xFFFF0000), F32)
    return lo, hi


def _outproj_kernel(mg_ref, w_ref, x_ref, g_ref, wrh_ref, wrl_ref, br_ref, x1_ref, h2_ref, lg_ref):
    x1 = x_ref[...] + jnp.dot(mg_ref[...], w_ref[...], preferred_element_type=F32)
    x1_ref[...] = x1
    ms = jnp.mean(x1 * x1, axis=-1, keepdims=True)
    h2 = x1 * lax.rsqrt(ms + NORM_EPS) * g_ref[...]
    h_hi = h2.astype(BF16)
    half = h2.shape[-1] // 2
    h2_ref[...] = _pack_bf16_pairs(h2[:, :half], h2[:, half:])
    h_lo = (h2 - h_hi.astype(F32)).astype(BF16)
    lg = (jnp.dot(h_hi, wrh_ref[...], preferred_element_type=F32)
          + jnp.dot(h_lo, wrh_ref[...], preferred_element_type=F32)
          + jnp.dot(h_hi, wrl_ref[...], preferred_element_type=F32))
    lg_ref[...] = lg + br_ref[...]


def _outproj(merged, w_out, x, g_ffn, wr_hi, wr_lo, br_pad, tm):
    m, d = x.shape
    row = lambda n: pl.BlockSpec((tm, n), lambda i: (i, 0))
    full = lambda arr: pl.BlockSpec(arr.shape, lambda i: (0, 0))
    return pl.pallas_call(
        _outproj_kernel,
        grid=(m // tm,),
        in_specs=[row(d), full(w_out), row(d), full(g_ffn), full(wr_hi), full(wr_lo), full(br_pad)],
        out_specs=[row(d), row(d // 2), row(ROUTER_PAD)],
        out_shape=[jax.ShapeDtypeStruct((m, d), F32), jax.ShapeDtypeStruct((m, d // 2), jnp.uint32),
                   jax.ShapeDtypeStruct((m, ROUTER_PAD), F32)],
        compiler_params=_cparams(("arbitrary",)),
        name="outproj_norm_router",
    )(merged, w_out, x, g_ffn, wr_hi, wr_lo, br_pad)


def _topk_kernel(lg_ref, idx_ref, gate_ref, rank_ref, cnt_ref, run_sc):
    @pl.when(pl.program_id(0) == 0)
    def _():
        run_sc[...] = jnp.zeros(run_sc.shape, F32)

    l = lg_ref[...]
    tm = l.shape[0]
    lane = lax.broadcasted_iota(jnp.int32, l.shape, 1)
    vals, ids = [], []
    for _ in range(TOP_K):
        m = jnp.max(l, axis=-1, keepdims=True)
        idx = jnp.min(jnp.where(l == m, lane, l.shape[-1]), axis=-1, keepdims=True)
        vals.append(m)
        ids.append(idx)
        l = jnp.where(lane == idx, -jnp.inf, l)
    es = [jnp.exp(v - vals[0]) for v in vals]
    den = es[0]
    for e in es[1:]:
        den = den + e

    chosen = jnp.zeros(l.shape, F32)
    for kk in range(TOP_K):
        chosen = jnp.where(lane == ids[kk], 1.0, chosen)
    tr = lax.broadcasted_iota(jnp.int32, (tm, tm), 0)
    tc = lax.broadcasted_iota(jnp.int32, (tm, tm), 1)
    earlier = jnp.where(tc < tr, 1.0, 0.0).astype(BF16)
    before = jnp.dot(earlier, chosen.astype(BF16), preferred_element_type=F32) + run_sc[...]

    idx_out = jnp.zeros(l.shape, jnp.int32)
    gate_out = jnp.zeros(l.shape, F32)
    rank_out = jnp.zeros(l.shape, F32)
    for kk in range(TOP_K):
        idx_out = jnp.where(lane == kk, ids[kk], idx_out)
        gate_out = jnp.where(lane == kk, es[kk] / den, gate_out)
        rk = jnp.sum(jnp.where(lane == ids[kk], before, 0.0), axis=-1, keepdims=True)
        rank_out = jnp.where(lane == kk, rk, rank_out)
    idx_ref[...] = idx_out
    gate_ref[...] = gate_out
    rank_ref[...] = rank_out.astype(jnp.int32)
    run = run_sc[...] + jnp.sum(chosen, axis=0, keepdims=True)
    run_sc[...] = run
    cnt_ref[...] = run.astype(jnp.int32)


def _topk(logits, tm):
    m, n = logits.shape
    spec = pl.BlockSpec((tm, n), lambda i: (i, 0))
    return pl.pallas_call(
        _topk_kernel,
        grid=(m // tm,),
        in_specs=[spec],
        out_specs=[spec, spec, spec, pl.BlockSpec((1, n), lambda i: (0, 0))],
        out_shape=[jax.ShapeDtypeStruct((m, n), jnp.int32), jax.ShapeDtypeStruct((m, n), F32),
                   jax.ShapeDtypeStruct((m, n), jnp.int32), jax.ShapeDtypeStruct((1, n), jnp.int32)],
        scratch_shapes=[pltpu.VMEM((1, n), F32)],
        compiler_params=_cparams(("arbitrary",)),
        name="router_topk",
    )(logits)


def _ffn1_kernel(sb_ref, sj_ref, se_ref, sfirst_ref, svalid_ref,
                 x_ref, wg_ref, wl_ref, bg_ref, bl_ref, o_ref, wg_sc, wl_sc):
    s = pl.program_id(0)

    @pl.when(sfirst_ref[s] == 1)
    def _():
        wg_sc[...] = wg_ref[0].astype(BF16)
        wl_sc[...] = wl_ref[0].astype(BF16)

    @pl.when(svalid_ref[s] == 1)
    def _():
        x_lo, x_hi = _unpack_bf16_pairs(x_ref[...])
        x_lo = x_lo.astype(BF16)
        x_hi = x_hi.astype(BF16)
        half = x_lo.shape[-1]

        def proj(w_sc, b_ref):
            return (jnp.dot(x_lo, w_sc[0:half, :], preferred_element_type=F32)
                    + jnp.dot(x_hi, w_sc[half:, :], preferred_element_type=F32) + b_ref[0])

        a_glu = jnp.minimum(proj(wg_sc, bg_ref), SWIGLU_LIMIT)
        a_lin = jnp.clip(proj(wl_sc, bl_ref), -SWIGLU_LIMIT, SWIGLU_LIMIT)
        act = a_glu * jax.nn.sigmoid(SWIGLU_ALPHA * a_glu) * (a_lin + 1.0)
        o_ref[...] = act.astype(o_ref.dtype)

    @pl.when(svalid_ref[s] == 0)
    def _():
        o_ref[...] = jnp.zeros(o_ref.shape, o_ref.dtype)


def _ffn2_kernel(sb_ref, sj_ref, se_ref, sfirst_ref, svalid_ref,
                 h_ref, w_ref, b_ref, o_ref, w_sc):
    s = pl.program_id(0)

    @pl.when(sfirst_ref[s] == 1)
    def _():
        w_sc[...] = w_ref[0].astype(BF16)

    @pl.when(svalid_ref[s] == 1)
    def _():
        y = jnp.dot(h_ref[...], w_sc[...], preferred_element_type=F32) + b_ref[0]
        half = y.shape[-1] // 2
        o_ref[...] = _pack_bf16_pairs(y[:, :half], y[:, half:])

    @pl.when(svalid_ref[s] == 0)
    def _():
        o_ref[...] = jnp.zeros(o_ref.shape, o_ref.dtype)


def _step_tables(nb, n_tiles, n_blocks):
    n_steps = n_blocks * n_tiles
    per_e = nb * n_tiles
    cum_end = jnp.cumsum(per_e)
    total = cum_end[-1]
    used_blocks = jnp.sum(nb)
    blk_start = jnp.cumsum(nb) - nb
    s_raw = jnp.arange(n_steps, dtype=jnp.int32)
    valid = s_raw < total
    s = jnp.minimum(s_raw, total - 1)
    e = jnp.minimum(jnp.sum((s[:, None] >= cum_end[None, :]).astype(jnp.int32), axis=1), N_EXPERTS - 1)
    r = s - (cum_end[e] - per_e[e])
    nbe = jnp.maximum(nb[e], 1)
    j = r // nbe
    bi = r % nbe
    first = jnp.logical_and(bi == 0, valid)
    spare = jnp.maximum(n_blocks - used_blocks, 1)
    r_pad = jnp.maximum(s_raw - total, 0)
    blk = jnp.where(valid, blk_start[e] + bi, used_blocks + r_pad % spare)
    j = jnp.where(valid, j, r_pad // spare)
    i32 = lambda a: a.astype(jnp.int32)
    return i32(blk), i32(j), e, i32(first), i32(valid)


def _expert_ffn(xs, nb, w1, b1, w2, b2):
    p = xs.shape[0]
    d = 2 * xs.shape[1]
    ff = w2.shape[1]
    n_blocks = p // MOE_BLK
    nj1 = ff // FFN1_TN
    nj2 = d // FFN2_TN
    b1r = b1.reshape(N_EXPERTS, 1, 2 * ff)
    b2r = b2.reshape(N_EXPERTS, 1, d)

    t1 = _step_tables(nb, nj1, n_blocks)
    h = pl.pallas_call(
        _ffn1_kernel,
        grid_spec=pltpu.PrefetchScalarGridSpec(
            num_scalar_prefetch=5,
            grid=(n_blocks * nj1,),
            in_specs=[
                pl.BlockSpec((MOE_BLK, d // 2), lambda s, sb, sj, se, sf, sv: (sb[s], 0)),
                pl.BlockSpec((1, d, FFN1_TN), lambda s, sb, sj, se, sf, sv: (se[s], 0, sj[s])),
                pl.BlockSpec((1, d, FFN1_TN), lambda s, sb, sj, se, sf, sv: (se[s], 0, nj1 + sj[s])),
                pl.BlockSpec((1, 1, FFN1_TN), lambda s, sb, sj, se, sf, sv: (se[s], 0, sj[s])),
                pl.BlockSpec((1, 1, FFN1_TN), lambda s, sb, sj, se, sf, sv: (se[s], 0, nj1 + sj[s])),
            ],
            out_specs=pl.BlockSpec((MOE_BLK, FFN1_TN), lambda s, sb, sj, se, sf, sv: (sb[s], sj[s])),
            scratch_shapes=[pltpu.VMEM((d, FFN1_TN), BF16), pltpu.VMEM((d, FFN1_TN), BF16)],
        ),
        out_shape=jax.ShapeDtypeStruct((p, ff), BF16),
        compiler_params=_cparams(("arbitrary",)),
        name="expert_ffn_up",
    )(*t1, xs, w1, w1, b1r, b1r)

    t2 = _step_tables(nb, nj2, n_blocks)
    ys = pl.pallas_call(
        _ffn2_kernel,
        grid_spec=pltpu.PrefetchScalarGridSpec(
            num_scalar_prefetch=5,
            grid=(n_blocks * nj2,),
            in_specs=[
                pl.BlockSpec((MOE_BLK, ff), lambda s, sb, sj, se, sf, sv: (sb[s], 0)),
                pl.BlockSpec((1, ff, FFN2_TN), lambda s, sb, sj, se, sf, sv: (se[s], 0, sj[s])),
                pl.BlockSpec((1, 1, FFN2_TN), lambda s, sb, sj, se, sf, sv: (se[s], 0, sj[s])),
            ],
            out_specs=pl.BlockSpec((MOE_BLK, FFN2_TN // 2), lambda s, sb, sj, se, sf, sv: (sb[s], sj[s])),
            scratch_shapes=[pltpu.VMEM((ff, FFN2_TN), BF16)],
        ),
        out_shape=jax.ShapeDtypeStruct((p, d // 2), jnp.uint32),
        compiler_params=_cparams(("arbitrary",)),
        name="expert_ffn_down",
    )(*t2, h, w2, b2r)
    return ys


def _dispatch_kernel(dest_ref, h_ref, xs_in_ref, xs_ref, sem):
    del xs_in_ref
    tm = h_ref.shape[0]
    base = pl.program_id(0) * (tm * TOP_K)

    def row_copy(r, kk):
        dst = dest_ref[base + r * TOP_K + kk]
        return pltpu.make_async_copy(h_ref.at[pl.ds(r, 1), :], xs_ref.at[pl.ds(dst, 1), :], sem)

    def issue(r, carry):
        for kk in range(TOP_K):
            row_copy(r, kk).start()
        return carry

    lax.fori_loop(0, tm, issue, 0, unroll=DMA_UNROLL)

    def drain(r, carry):
        for kk in range(TOP_K):
            row_copy(r, kk).wait()
        return carry

    lax.fori_loop(0, tm, drain, 0, unroll=DMA_UNROLL)


def _dispatch(h2p, dest_flat, p, tm):
    t, w = h2p.shape
    xs0 = jnp.zeros((p, w), h2p.dtype)
    return pl.pallas_call(
        _dispatch_kernel,
        grid_spec=pltpu.PrefetchScalarGridSpec(
            num_scalar_prefetch=1,
            grid=(t // tm,),
            in_specs=[pl.BlockSpec((tm, w), lambda i, dest: (i, 0)),
                      pl.BlockSpec(memory_space=pl.ANY)],
            out_specs=pl.BlockSpec(memory_space=pl.ANY),
            scratch_shapes=[pltpu.SemaphoreType.DMA(())],
        ),
        out_shape=jax.ShapeDtypeStruct((p, w), h2p.dtype),
        input_output_aliases={2: 0},
        compiler_params=_cparams(("arbitrary",)),
        name="moe_dispatch",
    )(dest_flat, h2p, xs0)


def _combine_kernel(dest_ref, x1_ref, gate_ref, g_ref, ys_ref, o_ref, buf, sem):
    i = pl.program_id(0)
    n = pl.num_programs(0)
    tm = x1_ref.shape[0]
    tn2 = FFN2_TN // 2

    def row_copy(blk, slot, r, kk):
        src = dest_ref[(blk * tm + r) * TOP_K + kk]
        return pltpu.make_async_copy(ys_ref.at[pl.ds(src, 1), :], buf.at[slot, pl.ds(kk * tm + r, 1), :],
                                     sem.at[slot])

    def issue_block(blk, slot):
        def body(r, carry):
            for kk in range(TOP_K):
                row_copy(blk, slot, r, kk).start()
            return carry
        lax.fori_loop(0, tm, body, 0, unroll=DMA_UNROLL)

    @pl.when(i == 0)
    def _():
        issue_block(0, 0)

    @pl.when(i + 1 < n)
    def _():
        issue_block(i + 1, (i + 1) % 2)

    slot = i % 2

    def drain(r, carry):
        for kk in range(TOP_K):
            row_copy(i, slot, r, kk).wait()
        return carry

    lax.fori_loop(0, tm, drain, 0, unroll=DMA_UNROLL)

    gates = gate_ref[...]
    pieces = [None] * (2 * (2 * buf.shape[-1] // FFN2_TN))
    for kk in range(TOP_K):
        lo, hi = _unpack_bf16_pairs(buf[slot, kk * tm:(kk + 1) * tm, :])
        g = gates[:, kk:kk + 1]
        for j in range(len(pieces) // 2):
            for half, part in enumerate((lo, hi)):
                term = g * part[:, j * tn2:(j + 1) * tn2]
                idx = 2 * j + half
                pieces[idx] = term if pieces[idx] is None else pieces[idx] + term
    x = x1_ref[...] + jnp.concatenate(pieces, axis=-1)
    ms = jnp.mean(x * x, axis=-1, keepdims=True)
    o_ref[...] = (x * lax.rsqrt(ms + NORM_EPS) * g_ref[...]).astype(o_ref.dtype)


def _combine(x1, gates, dest_flat, ys, g_final, tm):
    t, d = x1.shape
    w = ys.shape[1]
    return pl.pallas_call(
        _combine_kernel,
        grid_spec=pltpu.PrefetchScalarGridSpec(
            num_scalar_prefetch=1,
            grid=(t // tm,),
            in_specs=[pl.BlockSpec((tm, d), lambda i, dest: (i, 0)),
                      pl.BlockSpec((tm, gates.shape[1]), lambda i, dest: (i, 0)),
                      pl.BlockSpec((1, d), lambda i, dest: (0, 0)),
                      pl.BlockSpec(memory_space=pl.ANY)],
            out_specs=pl.BlockSpec((tm, d), lambda i, dest: (i, 0)),
            scratch_shapes=[pltpu.VMEM((2, TOP_K * tm, w), ys.dtype), pltpu.SemaphoreType.DMA((2,))],
        ),
        out_shape=jax.ShapeDtypeStruct((t, d), F32),
        compiler_params=_cparams(("arbitrary",)),
        name="moe_combine_final_norm",
    )(dest_flat, x1, gates, g_final.reshape(1, d).astype(F32), ys)


def _heads_T(a, b, s, h, d):
    return a.reshape(b, s, h, d).transpose(0, 2, 3, 1)


def _heads(a, b, s, h, d):
    return a.reshape(b, s, h, d).transpose(0, 2, 1, 3)


def _heads_T_tiled(a, b, s, h, d, tk):
    return a.reshape(b, s // tk, tk, h, d).transpose(0, 3, 1, 4, 2)


def kernel(x, mem, g_mix, w_in, b_gate, rel_bias, lambda_q1, lambda_k1, lambda_q2, lambda_k2, diff_subln_g, g_mem, w_mem_kv, w_br_diff, w_br_sb, w_br_mem, w_out, g_ffn, w_router, b_router, w_exp1, b_exp1, w_exp2, b_exp2, g_final):
    b, s, d = x.shape
    t = b * s
    depth = g_mix.shape[0]
    dqk_w = DIFF_HEADS * 2 * DIFF_QK_DIM
    dv_w = DIFF_HEADS * DIFF_V_DIM
    sb_w = SB_HEADS * SB_DIM
    mem_w = MEM_HEADS * MEM_DIM
    qkv_w = 2 * dqk_w + dv_w + 3 * sb_w + mem_w
    mlen = mem.shape[1]
    tm = min(ROW_TILE, s)

    xf = x.reshape(t, d)
    for l in range(depth):
        w_in_b = w_in[l].astype(BF16)
        hmix = _rmsnorm(xf, g_mix[l], BF16, tm)
        qkv = _matmul(hmix, w_in_b, 0, qkv_w, name="in_proj_qkv")
        gates = _matmul(hmix, w_in_b, qkv_w, 3 * d, bias=b_gate[l], name="in_proj_gates")

        c0 = 0
        dq = qkv[:, c0:c0 + dqk_w]; c0 += dqk_w
        dk = qkv[:, c0:c0 + dqk_w]; c0 += dqk_w
        dvv = qkv[:, c0:c0 + dv_w]; c0 += dv_w
        sq = qkv[:, c0:c0 + sb_w]; c0 += sb_w
        sk = qkv[:, c0:c0 + sb_w]; c0 += sb_w
        sv = qkv[:, c0:c0 + sb_w]; c0 += sb_w
        mq = qkv[:, c0:c0 + mem_w]

        lam_init = 0.8 - 0.6 * math.exp(-0.3 * l)
        lam = (jnp.exp(jnp.sum(lambda_q1[l].astype(F32) * lambda_k1[l].astype(F32)))
               - jnp.exp(jnp.sum(lambda_q2[l].astype(F32) * lambda_k2[l].astype(F32))) + lam_init)
        ya_t = _diff_attention(_heads_T(dq, b, s, DIFF_HEADS, 2 * DIFF_QK_DIM),
                               _heads(dk, b, s, DIFF_HEADS, 2 * DIFF_QK_DIM),
                               _heads_T_tiled(dvv, b, s, DIFF_HEADS, DIFF_V_DIM, DIFF_TILE),
                               rel_bias, lam, diff_subln_g[l], lam_init)
        yb_t = _sb_attention(_heads_T(sq, b, s, SB_HEADS, SB_DIM),
                             _heads(sk, b, s, SB_HEADS, SB_DIM),
                             _heads_T_tiled(sv, b, s, SB_HEADS, SB_DIM, SB_GROUP_KEYS))
        ya = ya_t.transpose(0, 3, 1, 2).reshape(t, dv_w)
        yb = yb_t.transpose(0, 3, 1, 2).reshape(t, sb_w)

        hmem = _rmsnorm(mem.reshape(b * mlen, d), g_mem[l], BF16, min(ROW_TILE, b * mlen))
        kv = _matmul(hmem, w_mem_kv[l].astype(BF16), 0, 2 * mem_w, name="mem_kv_proj")
        yc = _mem_attention(mq.reshape(b, s, mem_w), kv.reshape(b, mlen, 2 * mem_w), tm).reshape(t, mem_w)

        merged = _merge(ya, yb, yc, w_br_diff[l].astype(BF16), w_br_sb[l].astype(BF16),
                        w_br_mem[l].astype(BF16), gates, d, tm, MM_TN)

        wr = jnp.pad(w_router[l].astype(F32), ((0, 0), (0, ROUTER_PAD - N_EXPERTS)))
        wr_hi = wr.astype(BF16)
        wr_lo = (wr - wr_hi.astype(F32)).astype(BF16)
        br_pad = jnp.pad(b_router[l].astype(F32), (0, ROUTER_PAD - N_EXPERTS),
                         constant_values=-jnp.inf).reshape(1, ROUTER_PAD)
        x1, h2, logits = _outproj(merged, w_out[l].astype(BF16), xf, g_ffn[l].reshape(1, d).astype(F32),
                                  wr_hi, wr_lo, br_pad, tm)
        top_i, top_g, rank, counts = _topk(logits, tm)

        n = t * TOP_K
        counts = counts[0, :N_EXPERTS]
        nb = (counts + MOE_BLK - 1) // MOE_BLK
        pad_start = (jnp.cumsum(nb) - nb) * MOE_BLK
        dest = (pad_start[top_i[:, :TOP_K]] + rank[:, :TOP_K]).reshape(n)
        n_blocks = n // MOE_BLK + N_EXPERTS
        p = n_blocks * MOE_BLK

        xs = _dispatch(h2, dest, p, min(DMA_ROWS, s))
        ys = _expert_ffn(xs, nb, w_exp1[l], b_exp1[l], w_exp2[l], b_exp2[l])
        assert l + 1 == depth, "combine is fused with the final norm: single-layer trunk"
        return _combine(x1, top_g, dest, ys, g_final, min(DMA_ROWS, s)).reshape(b, s, d)
```

```python
import functools
import math

import jax
import jax.numpy as jnp
from jax import lax
from jax.experimental import pallas as pl
from jax.experimental.pallas import tpu as pltpu

F32 = jnp.float32
BF16 = jnp.bfloat16

DIFF_HEADS = 8
DIFF_QK_DIM = 64
DIFF_V_DIM = 128
SB_HEADS = 8
SB_DIM = 128
MEM_HEADS = 4
MEM_DIM = 256
N_BUCKETS = 32
MAX_DISTANCE = 128
N_EXPERTS = 32
TOP_K = 4
SWIGLU_LIMIT = 7.0
SWIGLU_ALPHA = 1.702
NORM_EPS = 1e-6
NEG_INF = -1e30
LOG2E = math.log2(math.e)

LANES = 128
VMEM_LIMIT_BYTES = 56 * 1024 * 1024

ROW_TILE = 512
MM_TM = 1024
MM_TN = 1024
DIFF_TILE = 512
ATT_CHUNK = 64
SB_TK = 128
SB_GROUP_KEYS = 512
MOE_BLK = 512
FFN1_TN = 1024
FFN2_TN = 2048
ROUTER_PAD = LANES
DMA_ROWS = 256
DMA_UNROLL = 8


def _cparams(sem):
    return pltpu.CompilerParams(dimension_semantics=sem, vmem_limit_bytes=VMEM_LIMIT_BYTES)


def _rmsnorm_kernel(x_ref, g_ref, o_ref):
    x = x_ref[...].astype(F32)
    ms = jnp.mean(x * x, axis=-1, keepdims=True)
    o_ref[...] = (x * lax.rsqrt(ms + NORM_EPS) * g_ref[...]).astype(o_ref.dtype)


def _rmsnorm(x, g, out_dtype, tm):
    m, d = x.shape
    return pl.pallas_call(
        _rmsnorm_kernel,
        grid=(m // tm,),
        in_specs=[pl.BlockSpec((tm, d), lambda i: (i, 0)),
                  pl.BlockSpec((1, d), lambda i: (0, 0))],
        out_specs=pl.BlockSpec((tm, d), lambda i: (i, 0)),
        out_shape=jax.ShapeDtypeStruct((m, d), out_dtype),
        compiler_params=_cparams(("arbitrary",)),
        name="rmsnorm",
    )(x, g.reshape(1, d).astype(F32))


def _mm_kernel(a_ref, w_ref, o_ref):
    o_ref[...] = jnp.dot(a_ref[...], w_ref[...], preferred_element_type=F32).astype(o_ref.dtype)


def _mm_sigmoid_kernel(a_ref, w_ref, b_ref, o_ref):
    acc = jnp.dot(a_ref[...], w_ref[...], preferred_element_type=F32)
    o_ref[...] = jax.nn.sigmoid(acc + b_ref[...]).astype(o_ref.dtype)


_HEADS, _HEADS_T, _HEADS_T_TILED = "heads", "heads_T", "heads_T_tiled"


def _mm_heads_kernel(a_ref, w_ref, o_ref, *, layout, hd):
    acc = jnp.dot(a_ref[...], w_ref[...], preferred_element_type=F32)
    nh = acc.shape[1] // hd
    if layout == _HEADS:
        for hh in range(nh):
            o_ref[0, hh] = acc[:, hh * hd:(hh + 1) * hd].astype(o_ref.dtype)
        return
    for hh in range(nh):
        acc_t = acc[:, hh * hd:(hh + 1) * hd].T
        if layout == _HEADS_T:
            o_ref[0, hh] = acc_t.astype(o_ref.dtype)
        else:
            tk = o_ref.shape[-1]
            for c in range(o_ref.shape[2]):
                o_ref[0, hh, c] = acc_t[:, c * tk:(c + 1) * tk].astype(o_ref.dtype)


def _matmul_heads(a, w, col_off, b, s, nh, hd, layout, tk=None, tm=MM_TM, name="proj_heads"):
    m, k = a.shape
    tm = min(tm, s)
    n = nh * hd
    off = col_off // n
    per_b = s // tm
    if layout == _HEADS:
        shape, blk = (b, nh, s, hd), (1, nh, tm, hd)
        omap = lambda i: (i // per_b, 0, i % per_b, 0)
    elif layout == _HEADS_T:
        shape, blk = (b, nh, hd, s), (1, nh, hd, tm)
        omap = lambda i: (i // per_b, 0, 0, i % per_b)
    else:
        shape, blk = (b, nh, s // tk, hd, tk), (1, nh, tm // tk, hd, tk)
        omap = lambda i: (i // per_b, 0, i % per_b, 0, 0)
    return pl.pallas_call(
        functools.partial(_mm_heads_kernel, layout=layout, hd=hd),
        grid=(m // tm,),
        in_specs=[pl.BlockSpec((tm, k), lambda i: (i, 0)),
                  pl.BlockSpec((k, n), lambda i: (0, off))],
        out_specs=pl.BlockSpec(blk, omap),
        out_shape=jax.ShapeDtypeStruct(shape, BF16),
        compiler_params=_cparams(("arbitrary",)),
        name=name,
    )(a, w)


def _matmul(a, w, col_off, n_out, bias=None, tm=MM_TM, tn=MM_TN, name="matmul"):
    m, k = a.shape
    tm = min(tm, m)
    off = col_off // tn
    in_specs = [pl.BlockSpec((tm, k), lambda j, i: (i, 0)),
                pl.BlockSpec((k, tn), lambda j, i: (0, j + off))]
    args = [a, w]
    body = _mm_kernel
    if bias is not None:
        in_specs.append(pl.BlockSpec((1, tn), lambda j, i: (0, j)))
        args.append(bias.reshape(1, n_out).astype(F32))
        body = _mm_sigmoid_kernel
    return pl.pallas_call(
        body,
        grid=(n_out // tn, m // tm),
        in_specs=in_specs,
        out_specs=pl.BlockSpec((tm, tn), lambda j, i: (i, j)),
        out_shape=jax.ShapeDtypeStruct((m, n_out), BF16),
        compiler_params=_cparams(("arbitrary", "arbitrary")),
        name=name,
    )(*args)


def _t5_bias_tile(rel_ref, h, offset, tk, tq):
    r = lax.broadcasted_iota(jnp.int32, (tk, tq), 0)
    c = lax.broadcasted_iota(jnp.int32, (tk, tq), 1)
    n = jnp.maximum(c - r + offset, 0)
    max_exact = N_BUCKETS // 2
    nf = jnp.maximum(n, 1).astype(F32)
    large = max_exact + (jnp.log(nf / max_exact) / math.log(MAX_DISTANCE / max_exact)
                         * (N_BUCKETS - max_exact)).astype(jnp.int32)
    large = jnp.minimum(large, N_BUCKETS - 1)
    bucket = jnp.where(n < max_exact, n, large)
    far = rel_ref[(N_BUCKETS - 1) * DIFF_HEADS + h]
    out = jnp.zeros((tk, tq), F32)
    for j in range(N_BUCKETS - 1):
        out = jnp.where(bucket == j, (rel_ref[j * DIFF_HEADS + h] - far) * LOG2E, out)
    return out


def _tree(xs, op):
    while len(xs) > 1:
        nxt = [op(xs[i], xs[i + 1]) for i in range(0, len(xs) - 1, 2)]
        if len(xs) % 2:
            nxt.append(xs[-1])
        xs = nxt
    return xs[0]


def _rows_to_8(x, op):
    return _tree([x[i:i + 8] for i in range(0, x.shape[0], 8)], op)


_FAR, _NEAR, _DIAG = 0, 1, 2


def _diff_attn_kernel(rel_ref, lam_ref, qT_ref, k_ref, vT_ref, g_ref, o_ref,
                      bias_sc, s_a, s_b, pb_sc, m_sc, l_sc, acc_sc, *, lam_init):
    h = pl.program_id(1)
    qi = pl.program_id(2)
    tq = qT_ref.shape[-1]
    tk = vT_ref.shape[-1]
    ch = ATT_CHUNK
    nch = tk // ch

    @pl.when(qi == 0)
    def _():
        for c in range(nch):
            bias_sc[0, c * ch:(c + 1) * ch, :] = _t5_bias_tile(rel_ref, h, -c * ch, ch, tq)
            bias_sc[1, c * ch:(c + 1) * ch, :] = _t5_bias_tile(rel_ref, h, tk - c * ch, ch, tq)

    qs = (qT_ref[0, 0].astype(F32) * (DIFF_QK_DIM ** -0.5 * LOG2E)).astype(BF16)
    row = lax.broadcasted_iota(jnp.int32, qs.shape, 0)
    zero = jnp.zeros_like(qs)
    qmaps = (jnp.where(row < DIFF_QK_DIM, qs, zero), jnp.where(row >= DIFF_QK_DIM, qs, zero))

    m_sc[...] = jnp.full(m_sc.shape, NEG_INF, F32)
    l_sc[...] = jnp.zeros(l_sc.shape, F32)
    acc_sc[...] = jnp.zeros(acc_sc.shape, F32)

    def qk(j, dst):
        base = pl.multiple_of(j * tk, tk)
        kt = k_ref[0, 0, pl.ds(base, tk), :]
        for mi in range(2):
            dst[mi] = jnp.dot(kt, qmaps[mi], preferred_element_type=F32)

    def softmax_pv(src, j, kind):
        vt = vT_ref[0, 0, j]
        for mi in range(2):
            parts = []
            for c in range(nch):
                rows = slice(c * ch, (c + 1) * ch)
                x = src[mi, rows, :]
                if kind != _FAR:
                    x = x + bias_sc[0 if kind == _DIAG else 1, rows, :]
                    if kind == _DIAG:
                        r = lax.broadcasted_iota(jnp.int32, (ch, tq), 0) + c * ch
                        cc = lax.broadcasted_iota(jnp.int32, (ch, tq), 1)
                        x = jnp.where(r <= cc, x, NEG_INF)
                    src[mi, rows, :] = x
                parts.append(_rows_to_8(x, jnp.maximum))
            m_old = m_sc[mi]
            m_new = jnp.maximum(m_old, jnp.max(_tree(parts, jnp.maximum), axis=0, keepdims=True))
            alpha = jnp.exp2(m_old - m_new)
            parts = []
            for c in range(nch):
                rows = slice(c * ch, (c + 1) * ch)
                p = jnp.exp2(src[mi, rows, :] - m_new)
                parts.append(_rows_to_8(p, jnp.add))
                pb_sc[mi, rows, :] = p.astype(BF16)
            l_sc[mi] = alpha * l_sc[mi] + jnp.sum(_tree(parts, jnp.add), axis=0, keepdims=True)
            pv = jnp.dot(vt, pb_sc[mi], preferred_element_type=F32)
            acc_sc[mi] = alpha * acc_sc[mi] + pv
            m_sc[mi] = m_new

    n_far = jnp.maximum(qi - 1, 0)

    @pl.when(qi == 0)
    def _():
        qk(0, s_a)
        softmax_pv(s_a, 0, _DIAG)

    @pl.when(qi >= 1)
    def _():
        qk(0, s_a)

        def far_pair(i, carry):
            qk(2 * i + 1, s_b)
            softmax_pv(s_a, 2 * i, _FAR)
            qk(2 * i + 2, s_a)
            softmax_pv(s_b, 2 * i + 1, _FAR)
            return carry

        lax.fori_loop(0, n_far // 2, far_pair, 0)

        @pl.when(n_far % 2 == 1)
        def _():
            qk(qi - 1, s_b)
            softmax_pv(s_a, qi - 2, _FAR)
            qk(qi, s_a)
            softmax_pv(s_b, qi - 1, _NEAR)
            softmax_pv(s_a, qi, _DIAG)

        @pl.when(n_far % 2 == 0)
        def _():
            qk(qi, s_b)
            softmax_pv(s_a, qi - 1, _NEAR)
            softmax_pv(s_b, qi, _DIAG)

    lam = lam_ref[0]
    o = acc_sc[0] / l_sc[0] - lam * (acc_sc[1] / l_sc[1])
    ms = jnp.mean(o * o, axis=0, keepdims=True)
    y = o * lax.rsqrt(ms + NORM_EPS) * g_ref[...]
    o_ref[0] = (y * (1.0 - lam_init)).T.astype(o_ref.dtype)


def _diff_attention(qT, k, vT5, rel_bias, lam, subln_g, lam_init):
    b, h, dqk, s = qT.shape
    tq = tk = DIFF_TILE
    dv = vT5.shape[3]
    grid_spec = pltpu.PrefetchScalarGridSpec(
        num_scalar_prefetch=2,
        grid=(b, h, s // tq),
        in_specs=[
            pl.BlockSpec((1, 1, dqk, tq), lambda bi, hi, qi, *_: (bi, hi, 0, qi)),
            pl.BlockSpec((1, 1, s, dqk), lambda bi, hi, qi, *_: (bi, hi, 0, 0)),
            pl.BlockSpec((1, 1, s // tk, dv, tk), lambda bi, hi, qi, *_: (bi, hi, 0, 0, 0)),
            pl.BlockSpec((dv, 1), lambda bi, hi, qi, *_: (0, 0)),
        ],
        out_specs=pl.BlockSpec((1, tq, dv), lambda bi, hi, qi, *_: (bi, qi, hi)),
        scratch_shapes=[
            pltpu.VMEM((2, tk, tq), F32),
            pltpu.VMEM((2, tk, tq), F32),
            pltpu.VMEM((2, tk, tq), F32),
            pltpu.VMEM((2, tk, tq), BF16),
            pltpu.VMEM((2, 1, tq), F32),
            pltpu.VMEM((2, 1, tq), F32),
            pltpu.VMEM((2, dv, tq), F32),
        ],
    )
    return pl.pallas_call(
        functools.partial(_diff_attn_kernel, lam_init=lam_init),
        grid_spec=grid_spec,
        out_shape=jax.ShapeDtypeStruct((b, s, h * dv), BF16),
        compiler_params=_cparams(("arbitrary", "arbitrary", "arbitrary")),
        name="diff_attention",
    )(rel_bias.reshape(-1).astype(F32), lam.reshape(1).astype(F32), qT, k, vT5, subln_g.reshape(dv, 1).astype(F32))


def _sb_attn_kernel(qT_ref, k_ref, vT_ref, o_ref, z_a, z_b, hl_sc, a_sc, acc_sc, carry_sc):
    qi = pl.program_id(2)
    tq = qT_ref.shape[-1]
    gk = vT_ref.shape[-1]
    tk = SB_TK
    nt = gk // tk
    ch = ATT_CHUNK
    d = qT_ref.shape[2]
    qs = (qT_ref[0, 0].astype(F32) * (d ** -0.5)).astype(BF16)

    sr = lax.broadcasted_iota(jnp.int32, (tk, 2 * tk), 0)
    sc = lax.broadcasted_iota(jnp.int32, (tk, 2 * tk), 1)
    suffix = jnp.where((sc & (tk - 1)) > sr, 1.0, 0.0).astype(BF16)

    acc_sc[...] = jnp.zeros(acc_sc.shape, F32)
    carry_sc[...] = jnp.zeros(carry_sc.shape, F32)

    def qk(g, dst):
        base = pl.multiple_of(g * gk, gk)
        dst[...] = jnp.dot(k_ref[0, 0, pl.ds(base, gk), :], qs, preferred_element_type=F32)

    def strict_mask(row0):
        r = lax.broadcasted_iota(jnp.int32, (ch, tq), 0) + row0
        c = lax.broadcasted_iota(jnp.int32, (ch, tq), 1)
        return r < c

    def process(src, g, diag):
        lk0 = []
        for t in range(nt):
            for c in range(tk // ch):
                row0 = t * tk + c * ch
                z = src[row0:row0 + ch, :]
                neg_abs = lax.bitcast_convert_type(
                    lax.bitcast_convert_type(z, jnp.uint32) | jnp.uint32(0x80000000), F32)
                lg = jnp.log(1.0 + jnp.exp(neg_abs))
                w = jnp.minimum(z, 0.0) - lg
                lk = w - z
                if diag:
                    lk = jnp.where(strict_mask(row0), lk, 0.0)
                hi = lax.bitcast_convert_type(
                    lax.bitcast_convert_type(lk, jnp.uint32) & jnp.uint32(0xFFFF0000), F32)
                hl_sc[t, c * ch:(c + 1) * ch, :] = hi.astype(BF16)
                hl_sc[t, tk + c * ch:tk + (c + 1) * ch, :] = (lk - hi).astype(BF16)
                src[row0:row0 + ch, :] = w
                if c == 0:
                    lk0.append(lk[0:1, :])
        carry = carry_sc[...]
        for t in reversed(range(nt)):
            after = jnp.dot(suffix, hl_sc[t], preferred_element_type=F32)
            for c in range(tk // ch):
                row0 = t * tk + c * ch
                a = jnp.exp(src[row0:row0 + ch, :] + after[c * ch:(c + 1) * ch, :] + carry)
                if diag:
                    a = jnp.where(strict_mask(row0), a, 0.0)
                a_sc[row0:row0 + ch, :] = a.astype(BF16)
            carry = carry + after[0:1, :] + lk0[t]
        carry_sc[...] = carry
        acc_sc[...] += jnp.dot(vT_ref[0, 0, g], a_sc[...], preferred_element_type=F32)

    qk(qi, z_a)

    @pl.when(qi == 0)
    def _():
        process(z_a, qi, True)

    @pl.when(qi >= 1)
    def _():
        qk(qi - 1, z_b)
        process(z_a, qi, True)
        n_pairs = (qi - 1) // 2

        def pair(i, carry):
            g = qi - 1 - 2 * i
            qk(g - 1, z_a)
            process(z_b, g, False)
            qk(g - 2, z_b)
            process(z_a, g - 1, False)
            return carry

        lax.fori_loop(0, n_pairs, pair, 0)

        @pl.when(qi % 2 == 0)
        def _():
            qk(0, z_a)
            process(z_b, 1, False)
            process(z_a, 0, False)

        @pl.when(qi % 2 == 1)
        def _():
            process(z_b, 0, False)

    o_ref[0] = acc_sc[...].T.astype(o_ref.dtype)


def _sb_attention(qT, k, vT5):
    b, h, d, s = qT.shape
    tq = gk = SB_GROUP_KEYS
    return pl.pallas_call(
        _sb_attn_kernel,
        grid=(b, h, s // tq),
        in_specs=[
            pl.BlockSpec((1, 1, d, tq), lambda bi, hi, qi: (bi, hi, 0, qi)),
            pl.BlockSpec((1, 1, s, d), lambda bi, hi, qi: (bi, hi, 0, 0)),
            pl.BlockSpec((1, 1, s // gk, d, gk), lambda bi, hi, qi: (bi, hi, 0, 0, 0)),
        ],
        out_specs=pl.BlockSpec((1, tq, d), lambda bi, hi, qi: (bi, qi, hi)),
        out_shape=jax.ShapeDtypeStruct((b, s, h * d), BF16),
        scratch_shapes=[
            pltpu.VMEM((gk, tq), F32),
            pltpu.VMEM((gk, tq), F32),
            pltpu.VMEM((gk // SB_TK, 2 * SB_TK, tq), BF16),
            pltpu.VMEM((gk, tq), BF16),
            pltpu.VMEM((d, tq), F32),
            pltpu.VMEM((1, tq), F32),
        ],
        compiler_params=_cparams(("arbitrary", "arbitrary", "arbitrary")),
        name="sb_attention",
    )(qT, k, vT5)


def _mem_attn_kernel(q_ref, kv_ref, o_ref):
    hd = q_ref.shape[-1]
    d = MEM_DIM
    outs = []
    for hh in range(hd // d):
        q = q_ref[0, :, hh * d:(hh + 1) * d]
        kk = kv_ref[0, :, hh * d:(hh + 1) * d]
        vv = kv_ref[0, :, hd + hh * d:hd + (hh + 1) * d]
        s = lax.dot_general(q, kk, (((1,), (1,)), ((), ())), preferred_element_type=F32) * (d ** -0.5)
        m = jnp.max(s, axis=-1, keepdims=True)
        p = jnp.exp(s - m)
        p = p / jnp.sum(p, axis=-1, keepdims=True)
        outs.append(jnp.dot(p.astype(BF16), vv, preferred_element_type=F32))
    o_ref[0] = jnp.concatenate(outs, axis=-1).astype(o_ref.dtype)


def _mem_attention(q, kv, tm):
    b, s, hd = q.shape
    mlen = kv.shape[1]
    return pl.pallas_call(
        _mem_attn_kernel,
        grid=(b, s // tm),
        in_specs=[pl.BlockSpec((1, tm, hd), lambda bi, i: (bi, i, 0)),
                  pl.BlockSpec((1, mlen, 2 * hd), lambda bi, i: (bi, 0, 0))],
        out_specs=pl.BlockSpec((1, tm, hd), lambda bi, i: (bi, i, 0)),
        out_shape=jax.ShapeDtypeStruct((b, s, hd), BF16),
        compiler_params=_cparams(("arbitrary", "arbitrary")),
        name="mem_attention",
    )(q, kv)


def _merge_kernel(ya_ref, yb_ref, yc_ref, wa_ref, wb_ref, wc_ref, ga_ref, gb_ref, gc_ref, o_ref):
    pa = jnp.dot(ya_ref[...], wa_ref[...], preferred_element_type=F32)
    pb = jnp.dot(yb_ref[...], wb_ref[...], preferred_element_type=F32)
    pc = jnp.dot(yc_ref[...], wc_ref[...], preferred_element_type=F32)
    merged = (ga_ref[...].astype(F32) * pa + gb_ref[...].astype(F32) * pb + gc_ref[...].astype(F32) * pc)
    o_ref[...] = merged.astype(o_ref.dtype)


def _merge(ya, yb, yc, wa, wb, wc, gates, d_model, tm, tn):
    m = ya.shape[0]
    nblk = d_model // tn
    y_spec = lambda arr: pl.BlockSpec((tm, arr.shape[1]), lambda j, i: (i, 0))
    w_spec = lambda arr: pl.BlockSpec((arr.shape[0], tn), lambda j, i: (0, j))
    g_spec = lambda br: pl.BlockSpec((tm, tn), lambda j, i, br=br: (i, br * nblk + j))
    return pl.pallas_call(
        _merge_kernel,
        grid=(nblk, m // tm),
        in_specs=[y_spec(ya), y_spec(yb), y_spec(yc), w_spec(wa), w_spec(wb), w_spec(wc),
                  g_spec(0), g_spec(1), g_spec(2)],
        out_specs=pl.BlockSpec((tm, tn), lambda j, i: (i, j)),
        out_shape=jax.ShapeDtypeStruct((m, d_model), BF16),
        compiler_params=_cparams(("arbitrary", "arbitrary")),
        name="gated_merge",
    )(ya, yb, yc, wa, wb, wc, gates, gates, gates)


def _pack_bf16_pairs(lo, hi):
    lo_bits = lax.bitcast_convert_type(lo.astype(BF16).astype(F32), jnp.uint32)
    hi_bits = lax.bitcast_convert_type(hi.astype(BF16).astype(F32), jnp.uint32)
    return (hi_bits & jnp.uint32(0xFFFF0000)) | (lo_bits >> 16)


def _unpack_bf16_pairs(words):
    lo = lax.bitcast_convert_type(words << 16, F32)
    hi = lax.bitcast_convert_type(words & jnp.uint32(0xFFFF0000), F32)
    return lo, hi


def _outproj_kernel(mg_ref, w_ref, x_ref, g_ref, wrh_ref, wrl_ref, br_ref, x1_ref, h2_ref, lg_ref):
    x1 = x_ref[...] + jnp.dot(mg_ref[...], w_ref[...], preferred_element_type=F32)
    x1_ref[...] = x1
    ms = jnp.mean(x1 * x1, axis=-1, keepdims=True)
    h2 = x1 * lax.rsqrt(ms + NORM_EPS) * g_ref[...]
    h_hi = h2.astype(BF16)
    half = h2.shape[-1] // 2
    h2_ref[...] = _pack_bf16_pairs(h2[:, :half], h2[:, half:])
    h_lo = (h2 - h_hi.astype(F32)).astype(BF16)
    lg = (jnp.dot(h_hi, wrh_ref[...], preferred_element_type=F32)
          + jnp.dot(h_lo, wrh_ref[...], preferred_element_type=F32)
          + jnp.dot(h_hi, wrl_ref[...], preferred_element_type=F32))
    lg_ref[...] = lg + br_ref[...]


def _outproj(merged, w_out, x, g_ffn, wr_hi, wr_lo, br_pad, tm):
    m, d = x.shape
    row = lambda n: pl.BlockSpec((tm, n), lambda i: (i, 0))
    full = lambda arr: pl.BlockSpec(arr.shape, lambda i: (0, 0))
    return pl.pallas_call(
        _outproj_kernel,
        grid=(m // tm,),
        in_specs=[row(d), full(w_out), row(d), full(g_ffn), full(wr_hi), full(wr_lo), full(br_pad)],
        out_specs=[row(d), row(d // 2), row(ROUTER_PAD)],
        out_shape=[jax.ShapeDtypeStruct((m, d), F32), jax.ShapeDtypeStruct((m, d // 2), jnp.uint32),
                   jax.ShapeDtypeStruct((m, ROUTER_PAD), F32)],
        compiler_params=_cparams(("arbitrary",)),
        name="outproj_norm_router",
    )(merged, w_out, x, g_ffn, wr_hi, wr_lo, br_pad)


def _topk_kernel(lg_ref, idx_ref, gate_ref, rank_ref, cnt_ref, run_sc):
    @pl.when(pl.program_id(0) == 0)
    def _():
        run_sc[...] = jnp.zeros(run_sc.shape, F32)

    l = lg_ref[...]
    tm = l.shape[0]
    lane = lax.broadcasted_iota(jnp.int32, l.shape, 1)
    vals, ids = [], []
    for _ in range(TOP_K):
        m = jnp.max(l, axis=-1, keepdims=True)
        idx = jnp.min(jnp.where(l == m, lane, l.shape[-1]), axis=-1, keepdims=True)
        vals.append(m)
        ids.append(idx)
        l = jnp.where(lane == idx, -jnp.inf, l)
    es = [jnp.exp(v - vals[0]) for v in vals]
    den = es[0]
    for e in es[1:]:
        den = den + e

    chosen = jnp.zeros(l.shape, F32)
    for kk in range(TOP_K):
        chosen = jnp.where(lane == ids[kk], 1.0, chosen)
    tr = lax.broadcasted_iota(jnp.int32, (tm, tm), 0)
    tc = lax.broadcasted_iota(jnp.int32, (tm, tm), 1)
    earlier = jnp.where(tc < tr, 1.0, 0.0).astype(BF16)
    before = jnp.dot(earlier, chosen.astype(BF16), preferred_element_type=F32) + run_sc[...]

    idx_out = jnp.zeros(l.shape, jnp.int32)
    gate_out = jnp.zeros(l.shape, F32)
    rank_out = jnp.zeros(l.shape, F32)
    for kk in range(TOP_K):
        idx_out = jnp.where(lane == kk, ids[kk], idx_out)
        gate_out = jnp.where(lane == kk, es[kk] / den, gate_out)
        rk = jnp.sum(jnp.where(lane == ids[kk], before, 0.0), axis=-1, keepdims=True)
        rank_out = jnp.where(lane == kk, rk, rank_out)
    idx_ref[...] = idx_out
    gate_ref[...] = gate_out
    rank_ref[...] = rank_out.astype(jnp.int32)
    run = run_sc[...] + jnp.sum(chosen, axis=0, keepdims=True)
    run_sc[...] = run
    cnt_ref[...] = run.astype(jnp.int32)


def _topk(logits, tm):
    m, n = logits.shape
    spec = pl.BlockSpec((tm, n), lambda i: (i, 0))
    return pl.pallas_call(
        _topk_kernel,
        grid=(m // tm,),
        in_specs=[spec],
        out_specs=[spec, spec, spec, pl.BlockSpec((1, n), lambda i: (0, 0))],
        out_shape=[jax.ShapeDtypeStruct((m, n), jnp.int32), jax.ShapeDtypeStruct((m, n), F32),
                   jax.ShapeDtypeStruct((m, n), jnp.int32), jax.ShapeDtypeStruct((1, n), jnp.int32)],
        scratch_shapes=[pltpu.VMEM((1, n), F32)],
        compiler_params=_cparams(("arbitrary",)),
        name="router_topk",
    )(logits)


def _ffn1_kernel(sb_ref, sj_ref, se_ref, sfirst_ref, svalid_ref,
                 x_ref, wg_ref, wl_ref, bg_ref, bl_ref, o_ref, wg_sc, wl_sc):
    s = pl.program_id(0)

    @pl.when(sfirst_ref[s] == 1)
    def _():
        wg_sc[...] = wg_ref[0].astype(BF16)
        wl_sc[...] = wl_ref[0].astype(BF16)

    @pl.when(svalid_ref[s] == 1)
    def _():
        x_lo, x_hi = _unpack_bf16_pairs(x_ref[...])
        x_lo = x_lo.astype(BF16)
        x_hi = x_hi.astype(BF16)
        half = x_lo.shape[-1]

        def proj(w_sc, b_ref):
            return (jnp.dot(x_lo, w_sc[0:half, :], preferred_element_type=F32)
                    + jnp.dot(x_hi, w_sc[half:, :], preferred_element_type=F32) + b_ref[0])

        a_glu = jnp.minimum(proj(wg_sc, bg_ref), SWIGLU_LIMIT)
        a_lin = jnp.clip(proj(wl_sc, bl_ref), -SWIGLU_LIMIT, SWIGLU_LIMIT)
        act = a_glu * jax.nn.sigmoid(SWIGLU_ALPHA * a_glu) * (a_lin + 1.0)
        o_ref[...] = act.astype(o_ref.dtype)

    @pl.when(svalid_ref[s] == 0)
    def _():
        o_ref[...] = jnp.zeros(o_ref.shape, o_ref.dtype)


def _ffn2_kernel(sb_ref, sj_ref, se_ref, sfirst_ref, svalid_ref,
                 h_ref, w_ref, b_ref, o_ref, w_sc):
    s = pl.program_id(0)

    @pl.when(sfirst_ref[s] == 1)
    def _():
        w_sc[...] = w_ref[0].astype(BF16)

    @pl.when(svalid_ref[s] == 1)
    def _():
        y = jnp.dot(h_ref[...], w_sc[...], preferred_element_type=F32) + b_ref[0]
        half = y.shape[-1] // 2
        o_ref[...] = _pack_bf16_pairs(y[:, :half], y[:, half:])

    @pl.when(svalid_ref[s] == 0)
    def _():
        o_ref[...] = jnp.zeros(o_ref.shape, o_ref.dtype)


def _step_tables(nb, n_tiles, n_blocks):
    n_steps = n_blocks * n_tiles
    per_e = nb * n_tiles
    cum_end = jnp.cumsum(per_e)
    total = cum_end[-1]
    used_blocks = jnp.sum(nb)
    blk_start = jnp.cumsum(nb) - nb
    s_raw = jnp.arange(n_steps, dtype=jnp.int32)
    valid = s_raw < total
    s = jnp.minimum(s_raw, total - 1)
    e = jnp.minimum(jnp.sum((s[:, None] >= cum_end[None, :]).astype(jnp.int32), axis=1), N_EXPERTS - 1)
    r = s - (cum_end[e] - per_e[e])
    nbe = jnp.maximum(nb[e], 1)
    j = r // nbe
    bi = r % nbe
    first = jnp.logical_and(bi == 0, valid)
    spare = jnp.maximum(n_blocks - used_blocks, 1)
    r_pad = jnp.maximum(s_raw - total, 0)
    blk = jnp.where(valid, blk_start[e] + bi, used_blocks + r_pad % spare)
    j = jnp.where(valid, j, r_pad // spare)
    i32 = lambda a: a.astype(jnp.int32)
    return i32(blk), i32(j), e, i32(first), i32(valid)


def _expert_ffn(xs, nb, w1, b1, w2, b2):
    p = xs.shape[0]
    d = 2 * xs.shape[1]
    ff = w2.shape[1]
    n_blocks = p // MOE_BLK
    nj1 = ff // FFN1_TN
    nj2 = d // FFN2_TN
    b1r = b1.reshape(N_EXPERTS, 1, 2 * ff)
    b2r = b2.reshape(N_EXPERTS, 1, d)

    t1 = _step_tables(nb, nj1, n_blocks)
    h = pl.pallas_call(
        _ffn1_kernel,
        grid_spec=pltpu.PrefetchScalarGridSpec(
            num_scalar_prefetch=5,
            grid=(n_blocks * nj1,),
            in_specs=[
                pl.BlockSpec((MOE_BLK, d // 2), lambda s, sb, sj, se, sf, sv: (sb[s], 0)),
                pl.BlockSpec((1, d, FFN1_TN), lambda s, sb, sj, se, sf, sv: (se[s], 0, sj[s])),
                pl.BlockSpec((1, d, FFN1_TN), lambda s, sb, sj, se, sf, sv: (se[s], 0, nj1 + sj[s])),
                pl.BlockSpec((1, 1, FFN1_TN), lambda s, sb, sj, se, sf, sv: (se[s], 0, sj[s])),
                pl.BlockSpec((1, 1, FFN1_TN), lambda s, sb, sj, se, sf, sv: (se[s], 0, nj1 + sj[s])),
            ],
            out_specs=pl.BlockSpec((MOE_BLK, FFN1_TN), lambda s, sb, sj, se, sf, sv: (sb[s], sj[s])),
            scratch_shapes=[pltpu.VMEM((d, FFN1_TN), BF16), pltpu.VMEM((d, FFN1_TN), BF16)],
        ),
        out_shape=jax.ShapeDtypeStruct((p, ff), BF16),
        compiler_params=_cparams(("arbitrary",)),
        name="expert_ffn_up",
    )(*t1, xs, w1, w1, b1r, b1r)

    t2 = _step_tables(nb, nj2, n_blocks)
    ys = pl.pallas_call(
        _ffn2_kernel,
        grid_spec=pltpu.PrefetchScalarGridSpec(
            num_scalar_prefetch=5,
            grid=(n_blocks * nj2,),
            in_specs=[
                pl.BlockSpec((MOE_BLK, ff), lambda s, sb, sj, se, sf, sv: (sb[s], 0)),
                pl.BlockSpec((1, ff, FFN2_TN), lambda s, sb, sj, se, sf, sv: (se[s], 0, sj[s])),
                pl.BlockSpec((1, 1, FFN2_TN), lambda s, sb, sj, se, sf, sv: (se[s], 0, sj[s])),
            ],
            out_specs=pl.BlockSpec((MOE_BLK, FFN2_TN // 2), lambda s, sb, sj, se, sf, sv: (sb[s], sj[s])),
            scratch_shapes=[pltpu.VMEM((ff, FFN2_TN), BF16)],
        ),
        out_shape=jax.ShapeDtypeStruct((p, d // 2), jnp.uint32),
        compiler_params=_cparams(("arbitrary",)),
        name="expert_ffn_down",
    )(*t2, h, w2, b2r)
    return ys


def _dispatch_kernel(dest_ref, h_ref, xs_in_ref, xs_ref, sem):
    del xs_in_ref
    tm = h_ref.shape[0]
    base = pl.program_id(0) * (tm * TOP_K)

    def row_copy(r, kk):
        dst = dest_ref[base + r * TOP_K + kk]
        return pltpu.make_async_copy(h_ref.at[pl.ds(r, 1), :], xs_ref.at[pl.ds(dst, 1), :], sem)

    def issue(r, carry):
        for kk in range(TOP_K):
            row_copy(r, kk).start(priority=kk % 2)
        return carry

    lax.fori_loop(0, tm, issue, 0, unroll=DMA_UNROLL)

    def drain(r, carry):
        for kk in range(TOP_K):
            row_copy(r, kk).wait()
        return carry

    lax.fori_loop(0, tm, drain, 0, unroll=DMA_UNROLL)


def _dispatch(h2p, dest_flat, p, tm):
    t, w = h2p.shape
    xs0 = jnp.zeros((p, w), h2p.dtype)
    return pl.pallas_call(
        _dispatch_kernel,
        grid_spec=pltpu.PrefetchScalarGridSpec(
            num_scalar_prefetch=1,
            grid=(t // tm,),
            in_specs=[pl.BlockSpec((tm, w), lambda i, dest: (i, 0)),
                      pl.BlockSpec(memory_space=pl.ANY)],
            out_specs=pl.BlockSpec(memory_space=pl.ANY),
            scratch_shapes=[pltpu.SemaphoreType.DMA(())],
        ),
        out_shape=jax.ShapeDtypeStruct((p, w), h2p.dtype),
        input_output_aliases={2: 0},
        compiler_params=_cparams(("arbitrary",)),
        name="moe_dispatch",
    )(dest_flat, h2p, xs0)


def _combine_kernel(dest_ref, x1_ref, gate_ref, g_ref, ys_ref, o_ref, buf, sem):
    i = pl.program_id(0)
    n = pl.num_programs(0)
    tm = x1_ref.shape[0]
    tn2 = FFN2_TN // 2

    def row_copy(blk, slot, r, kk):
        src = dest_ref[(blk * tm + r) * TOP_K + kk]
        return pltpu.make_async_copy(ys_ref.at[pl.ds(src, 1), :], buf.at[slot, pl.ds(kk * tm + r, 1), :],
                                     sem.at[slot])

    def issue_block(blk, slot):
        def body(r, carry):
            for kk in range(TOP_K):
                row_copy(blk, slot, r, kk).start(priority=kk % 2)
            return carry
        lax.fori_loop(0, tm, body, 0, unroll=DMA_UNROLL)

    @pl.when(i == 0)
    def _():
        issue_block(0, 0)

    @pl.when(i + 1 < n)
    def _():
        issue_block(i + 1, (i + 1) % 2)

    slot = i % 2

    def drain(r, carry):
        for kk in range(TOP_K):
            row_copy(i, slot, r, kk).wait()
        return carry

    lax.fori_loop(0, tm, drain, 0, unroll=DMA_UNROLL)

    gates = gate_ref[...]
    pieces = [None] * (2 * (2 * buf.shape[-1] // FFN2_TN))
    for kk in range(TOP_K):
        lo, hi = _unpack_bf16_pairs(buf[slot, kk * tm:(kk + 1) * tm, :])
        g = gates[:, kk:kk + 1]
        for j in range(len(pieces) // 2):
            for half, part in enumerate((lo, hi)):
                term = g * part[:, j * tn2:(j + 1) * tn2]
                idx = 2 * j + half
                pieces[idx] = term if pieces[idx] is None else pieces[idx] + term
    x = x1_ref[...] + jnp.concatenate(pieces, axis=-1)
    ms = jnp.mean(x * x, axis=-1, keepdims=True)
    o_ref[...] = (x * lax.rsqrt(ms + NORM_EPS) * g_ref[...]).astype(o_ref.dtype)


def _combine(x1, gates, dest_flat, ys, g_final, tm):
    t, d = x1.shape
    w = ys.shape[1]
    return pl.pallas_call(
        _combine_kernel,
        grid_spec=pltpu.PrefetchScalarGridSpec(
            num_scalar_prefetch=1,
            grid=(t // tm,),
            in_specs=[pl.BlockSpec((tm, d), lambda i, dest: (i, 0)),
                      pl.BlockSpec((tm, gates.shape[1]), lambda i, dest: (i, 0)),
                      pl.BlockSpec((1, d), lambda i, dest: (0, 0)),
                      pl.BlockSpec(memory_space=pl.ANY)],
            out_specs=pl.BlockSpec((tm, d), lambda i, dest: (i, 0)),
            scratch_shapes=[pltpu.VMEM((2, TOP_K * tm, w), ys.dtype), pltpu.SemaphoreType.DMA((2,))],
        ),
        out_shape=jax.ShapeDtypeStruct((t, d), F32),
        compiler_params=_cparams(("arbitrary",)),
        name="moe_combine_final_norm",
    )(dest_flat, x1, gates, g_final.reshape(1, d).astype(F32), ys)


def kernel(x, mem, g_mix, w_in, b_gate, rel_bias, lambda_q1, lambda_k1, lambda_q2, lambda_k2, diff_subln_g, g_mem, w_mem_kv, w_br_diff, w_br_sb, w_br_mem, w_out, g_ffn, w_router, b_router, w_exp1, b_exp1, w_exp2, b_exp2, g_final):
    b, s, d = x.shape
    t = b * s
    depth = g_mix.shape[0]
    dqk_w = DIFF_HEADS * 2 * DIFF_QK_DIM
    dv_w = DIFF_HEADS * DIFF_V_DIM
    sb_w = SB_HEADS * SB_DIM
    mem_w = MEM_HEADS * MEM_DIM
    qkv_w = 2 * dqk_w + dv_w + 3 * sb_w + mem_w
    mlen = mem.shape[1]
    tm = min(ROW_TILE, s)

    xf = x.reshape(t, d)
    for l in range(depth):
        w_in_b = w_in[l].astype(BF16)
        hmix = _rmsnorm(xf, g_mix[l], BF16, tm)
        c0 = 0
        dq = _matmul_heads(hmix, w_in_b, c0, b, s, DIFF_HEADS, 2 * DIFF_QK_DIM, _HEADS_T, name="in_proj_dq")
        c0 += dqk_w
        dk = _matmul_heads(hmix, w_in_b, c0, b, s, DIFF_HEADS, 2 * DIFF_QK_DIM, _HEADS, name="in_proj_dk")
        c0 += dqk_w
        dvv = _matmul_heads(hmix, w_in_b, c0, b, s, DIFF_HEADS, DIFF_V_DIM, _HEADS_T_TILED, tk=DIFF_TILE,
                            name="in_proj_dv")
        c0 += dv_w
        sq = _matmul_heads(hmix, w_in_b, c0, b, s, SB_HEADS, SB_DIM, _HEADS_T, name="in_proj_sq")
        c0 += sb_w
        sk = _matmul_heads(hmix, w_in_b, c0, b, s, SB_HEADS, SB_DIM, _HEADS, name="in_proj_sk")
        c0 += sb_w
        sv = _matmul_heads(hmix, w_in_b, c0, b, s, SB_HEADS, SB_DIM, _HEADS_T_TILED, tk=SB_GROUP_KEYS,
                           name="in_proj_sv")
        c0 += sb_w
        mq = _matmul(hmix, w_in_b, c0, mem_w, name="in_proj_mq")
        gates = _matmul(hmix, w_in_b, qkv_w, 3 * d, bias=b_gate[l], name="in_proj_gates")

        lam_init = 0.8 - 0.6 * math.exp(-0.3 * l)
        lam = (jnp.exp(jnp.sum(lambda_q1[l].astype(F32) * lambda_k1[l].astype(F32)))
               - jnp.exp(jnp.sum(lambda_q2[l].astype(F32) * lambda_k2[l].astype(F32))) + lam_init)
        ya = _diff_attention(dq, dk, dvv, rel_bias, lam, diff_subln_g[l], lam_init).reshape(t, dv_w)
        yb = _sb_attention(sq, sk, sv).reshape(t, sb_w)

        hmem = _rmsnorm(mem.reshape(b * mlen, d), g_mem[l], BF16, min(ROW_TILE, b * mlen))
        kv = _matmul(hmem, w_mem_kv[l].astype(BF16), 0, 2 * mem_w, name="mem_kv_proj")
        yc = _mem_attention(mq.reshape(b, s, mem_w), kv.reshape(b, mlen, 2 * mem_w), tm).reshape(t, mem_w)

        merged = _merge(ya, yb, yc, w_br_diff[l].astype(BF16), w_br_sb[l].astype(BF16),
                        w_br_mem[l].astype(BF16), gates, d, tm, MM_TN)

        wr = jnp.pad(w_router[l].astype(F32), ((0, 0), (0, ROUTER_PAD - N_EXPERTS)))
        wr_hi = wr.astype(BF16)
        wr_lo = (wr - wr_hi.astype(F32)).astype(BF16)
        br_pad = jnp.pad(b_router[l].astype(F32), (0, ROUTER_PAD - N_EXPERTS),
                         constant_values=-jnp.inf).reshape(1, ROUTER_PAD)
        x1, h2, logits = _outproj(merged, w_out[l].astype(BF16), xf, g_ffn[l].reshape(1, d).astype(F32),
                                  wr_hi, wr_lo, br_pad, tm)
        top_i, top_g, rank, counts = _topk(logits, tm)

        n = t * TOP_K
        counts = counts[0, :N_EXPERTS]
        nb = (counts + MOE_BLK - 1) // MOE_BLK
        pad_start = (jnp.cumsum(nb) - nb) * MOE_BLK
        dest = (pad_start[top_i[:, :TOP_K]] + rank[:, :TOP_K]).reshape(n)
        n_blocks = n // MOE_BLK + N_EXPERTS
        p = n_blocks * MOE_BLK

        xs = _dispatch(h2, dest, p, min(DMA_ROWS, s))
        ys = _expert_ffn(xs, nb, w_exp1[l], b_exp1[l], w_exp2[l], b_exp2[l])
        assert l + 1 == depth, "combine is fused with the final norm: single-layer trunk"
        return _combine(x1, top_g, dest, ys, g_final, min(DMA_ROWS, s)).reshape(b, s, d)
```

```python
import functools
import math

import jax
import jax.numpy as jnp
from jax import lax
from jax.experimental import pallas as pl
from jax.experimental.pallas import tpu as pltpu

F32 = jnp.float32
BF16 = jnp.bfloat16

DIFF_HEADS = 8
DIFF_QK_DIM = 64
DIFF_V_DIM = 128
SB_HEADS = 8
SB_DIM = 128
MEM_HEADS = 4
MEM_DIM = 256
N_BUCKETS = 32
MAX_DISTANCE = 128
N_EXPERTS = 32
TOP_K = 4
SWIGLU_LIMIT = 7.0
SWIGLU_ALPHA = 1.702
NORM_EPS = 1e-6
NEG_INF = -1e30
LOG2E = math.log2(math.e)

LANES = 128
VMEM_LIMIT_BYTES = 56 * 1024 * 1024

ROW_TILE = 512
MM_TM = 1024
MM_TN = 1024
DIFF_TILE = 512
ATT_CHUNK = 64
SB_TK = 128
SB_GROUP_KEYS = 512
MOE_BLK = 512
FFN1_TN = 1024
FFN2_TN = 2048
ROUTER_PAD = LANES
DMA_ROWS = 256
DMA_UNROLL = 8


def _cparams(sem):
    return pltpu.CompilerParams(dimension_semantics=sem, vmem_limit_bytes=VMEM_LIMIT_BYTES)


def _rmsnorm_kernel(x_ref, g_ref, o_ref):
    x = x_ref[...].astype(F32)
    ms = jnp.mean(x * x, axis=-1, keepdims=True)
    o_ref[...] = (x * lax.rsqrt(ms + NORM_EPS) * g_ref[...]).astype(o_ref.dtype)


def _rmsnorm(x, g, out_dtype, tm):
    m, d = x.shape
    return pl.pallas_call(
        _rmsnorm_kernel,
        grid=(m // tm,),
        in_specs=[pl.BlockSpec((tm, d), lambda i: (i, 0)),
                  pl.BlockSpec((1, d), lambda i: (0, 0))],
        out_specs=pl.BlockSpec((tm, d), lambda i: (i, 0)),
        out_shape=jax.ShapeDtypeStruct((m, d), out_dtype),
        compiler_params=_cparams(("arbitrary",)),
        name="rmsnorm",
    )(x, g.reshape(1, d).astype(F32))


def _mm_kernel(a_ref, w_ref, o_ref):
    o_ref[...] = jnp.dot(a_ref[...], w_ref[...], preferred_element_type=F32).astype(o_ref.dtype)


def _mm_sigmoid_kernel(a_ref, w_ref, b_ref, o_ref):
    acc = jnp.dot(a_ref[...], w_ref[...], preferred_element_type=F32)
    o_ref[...] = jax.nn.sigmoid(acc + b_ref[...]).astype(o_ref.dtype)


_HEADS, _HEADS_T, _HEADS_T_TILED = "heads", "heads_T", "heads_T_tiled"


def _mm_heads_kernel(a_ref, w_ref, o_ref, *, layout, hd):
    acc = jnp.dot(a_ref[...], w_ref[...], preferred_element_type=F32)
    nh = acc.shape[1] // hd
    if layout == _HEADS:
        for hh in range(nh):
            o_ref[0, hh] = acc[:, hh * hd:(hh + 1) * hd].astype(o_ref.dtype)
        return
    for hh in range(nh):
        acc_t = acc[:, hh * hd:(hh + 1) * hd].T
        if layout == _HEADS_T:
            o_ref[0, hh] = acc_t.astype(o_ref.dtype)
        else:
            tk = o_ref.shape[-1]
            for c in range(o_ref.shape[2]):
                o_ref[0, hh, c] = acc_t[:, c * tk:(c + 1) * tk].astype(o_ref.dtype)


def _matmul_heads(a, w, col_off, b, s, nh, hd, layout, tk=None, tm=MM_TM, name="proj_heads"):
    m, k = a.shape
    tm = min(tm, s)
    n = nh * hd
    off = col_off // n
    per_b = s // tm
    if layout == _HEADS:
        shape, blk = (b, nh, s, hd), (1, nh, tm, hd)
        omap = lambda i: (i // per_b, 0, i % per_b, 0)
    elif layout == _HEADS_T:
        shape, blk = (b, nh, hd, s), (1, nh, hd, tm)
        omap = lambda i: (i // per_b, 0, 0, i % per_b)
    else:
        shape, blk = (b, nh, s // tk, hd, tk), (1, nh, tm // tk, hd, tk)
        omap = lambda i: (i // per_b, 0, i % per_b, 0, 0)
    return pl.pallas_call(
        functools.partial(_mm_heads_kernel, layout=layout, hd=hd),
        grid=(m // tm,),
        in_specs=[pl.BlockSpec((tm, k), lambda i: (i, 0)),
                  pl.BlockSpec((k, n), lambda i: (0, off))],
        out_specs=pl.BlockSpec(blk, omap),
        out_shape=jax.ShapeDtypeStruct(shape, BF16),
        compiler_params=_cparams(("arbitrary",)),
        name=name,
    )(a, w)


def _matmul(a, w, col_off, n_out, bias=None, tm=MM_TM, tn=MM_TN, name="matmul"):
    m, k = a.shape
    tm = min(tm, m)
    off = col_off // tn
    in_specs = [pl.BlockSpec((tm, k), lambda j, i: (i, 0)),
                pl.BlockSpec((k, tn), lambda j, i: (0, j + off))]
    args = [a, w]
    body = _mm_kernel
    if bias is not None:
        in_specs.append(pl.BlockSpec((1, tn), lambda j, i: (0, j)))
        args.append(bias.reshape(1, n_out).astype(F32))
        body = _mm_sigmoid_kernel
    return pl.pallas_call(
        body,
        grid=(n_out // tn, m // tm),
        in_specs=in_specs,
        out_specs=pl.BlockSpec((tm, tn), lambda j, i: (i, j)),
        out_shape=jax.ShapeDtypeStruct((m, n_out), BF16),
        compiler_params=_cparams(("arbitrary", "arbitrary")),
        name=name,
    )(*args)


def _t5_bias_tile(rel_ref, h, offset, tk, tq):
    r = lax.broadcasted_iota(jnp.int32, (tk, tq), 0)
    c = lax.broadcasted_iota(jnp.int32, (tk, tq), 1)
    n = jnp.maximum(c - r + offset, 0)
    max_exact = N_BUCKETS // 2
    nf = jnp.maximum(n, 1).astype(F32)
    large = max_exact + (jnp.log(nf / max_exact) / math.log(MAX_DISTANCE / max_exact)
                         * (N_BUCKETS - max_exact)).astype(jnp.int32)
    large = jnp.minimum(large, N_BUCKETS - 1)
    bucket = jnp.where(n < max_exact, n, large)
    far = rel_ref[(N_BUCKETS - 1) * DIFF_HEADS + h]
    out = jnp.zeros((tk, tq), F32)
    for j in range(N_BUCKETS - 1):
        out = jnp.where(bucket == j, (rel_ref[j * DIFF_HEADS + h] - far) * LOG2E, out)
    return out


def _tree(xs, op):
    while len(xs) > 1:
        nxt = [op(xs[i], xs[i + 1]) for i in range(0, len(xs) - 1, 2)]
        if len(xs) % 2:
            nxt.append(xs[-1])
        xs = nxt
    return xs[0]


def _rows_to_8(x, op):
    return _tree([x[i:i + 8] for i in range(0, x.shape[0], 8)], op)


_FAR, _NEAR, _DIAG = 0, 1, 2


def _diff_attn_kernel(rel_ref, lam_ref, qT_ref, k_ref, vT_ref, g_ref, o_ref,
                      bias_sc, s_a, s_b, pb_sc, m_sc, l_sc, acc_sc, *, lam_init):
    h = pl.program_id(1)
    qi = pl.program_id(2)
    tq = qT_ref.shape[-1]
    tk = vT_ref.shape[-1]
    ch = ATT_CHUNK
    nch = tk // ch

    @pl.when(qi == 0)
    def _():
        for c in range(nch):
            bias_sc[0, c * ch:(c + 1) * ch, :] = _t5_bias_tile(rel_ref, h, -c * ch, ch, tq)
            bias_sc[1, c * ch:(c + 1) * ch, :] = _t5_bias_tile(rel_ref, h, tk - c * ch, ch, tq)

    qs = (qT_ref[0, 0].astype(F32) * (DIFF_QK_DIM ** -0.5 * LOG2E)).astype(BF16)
    row = lax.broadcasted_iota(jnp.int32, qs.shape, 0)
    zero = jnp.zeros_like(qs)
    qmaps = (jnp.where(row < DIFF_QK_DIM, qs, zero), jnp.where(row >= DIFF_QK_DIM, qs, zero))

    m_sc[...] = jnp.full(m_sc.shape, NEG_INF, F32)
    l_sc[...] = jnp.zeros(l_sc.shape, F32)
    acc_sc[...] = jnp.zeros(acc_sc.shape, F32)

    def qk(j, dst):
        base = pl.multiple_of(j * tk, tk)
        kt = k_ref[0, 0, pl.ds(base, tk), :]
        for mi in range(2):
            dst[mi] = jnp.dot(kt, qmaps[mi], preferred_element_type=F32)

    def softmax_pv(src, j, kind):
        vt = vT_ref[0, 0, j]
        for mi in range(2):
            parts = []
            for c in range(nch):
                rows = slice(c * ch, (c + 1) * ch)
                x = src[mi, rows, :]
                if kind != _FAR:
                    x = x + bias_sc[0 if kind == _DIAG else 1, rows, :]
                    if kind == _DIAG:
                        r = lax.broadcasted_iota(jnp.int32, (ch, tq), 0) + c * ch
                        cc = lax.broadcasted_iota(jnp.int32, (ch, tq), 1)
                        x = jnp.where(r <= cc, x, NEG_INF)
                    src[mi, rows, :] = x
                parts.append(_rows_to_8(x, jnp.maximum))
            m_old = m_sc[mi]
            m_new = jnp.maximum(m_old, jnp.max(_tree(parts, jnp.maximum), axis=0, keepdims=True))
            alpha = jnp.exp2(m_old - m_new)
            parts = []
            for c in range(nch):
                rows = slice(c * ch, (c + 1) * ch)
                p = jnp.exp2(src[mi, rows, :] - m_new)
                parts.append(_rows_to_8(p, jnp.add))
                pb_sc[mi, rows, :] = p.astype(BF16)
            l_sc[mi] = alpha * l_sc[mi] + jnp.sum(_tree(parts, jnp.add), axis=0, keepdims=True)
            pv = jnp.dot(vt, pb_sc[mi], preferred_element_type=F32)
            acc_sc[mi] = alpha * acc_sc[mi] + pv
            m_sc[mi] = m_new

    n_far = jnp.maximum(qi - 1, 0)

    @pl.when(qi == 0)
    def _():
        qk(0, s_a)
        softmax_pv(s_a, 0, _DIAG)

    @pl.when(qi >= 1)
    def _():
        qk(0, s_a)

        def far_pair(i, carry):
            qk(2 * i + 1, s_b)
            softmax_pv(s_a, 2 * i, _FAR)
            qk(2 * i + 2, s_a)
            softmax_pv(s_b, 2 * i + 1, _FAR)
            return carry

        lax.fori_loop(0, n_far // 2, far_pair, 0)

        @pl.when(n_far % 2 == 1)
        def _():
            qk(qi - 1, s_b)
            softmax_pv(s_a, qi - 2, _FAR)
            qk(qi, s_a)
            softmax_pv(s_b, qi - 1, _NEAR)
            softmax_pv(s_a, qi, _DIAG)

        @pl.when(n_far % 2 == 0)
        def _():
            qk(qi, s_b)
            softmax_pv(s_a, qi - 1, _NEAR)
            softmax_pv(s_b, qi, _DIAG)

    lam = lam_ref[0]
    o = acc_sc[0] / l_sc[0] - lam * (acc_sc[1] / l_sc[1])
    ms = jnp.mean(o * o, axis=0, keepdims=True)
    y = o * lax.rsqrt(ms + NORM_EPS) * g_ref[...]
    o_ref[0] = (y * (1.0 - lam_init)).T.astype(o_ref.dtype)


def _diff_attention(qT, k, vT5, rel_bias, lam, subln_g, lam_init):
    b, h, dqk, s = qT.shape
    tq = tk = DIFF_TILE
    dv = vT5.shape[3]
    grid_spec = pltpu.PrefetchScalarGridSpec(
        num_scalar_prefetch=2,
        grid=(b, h, s // tq),
        in_specs=[
            pl.BlockSpec((1, 1, dqk, tq), lambda bi, hi, qi, *_: (bi, hi, 0, qi)),
            pl.BlockSpec((1, 1, s, dqk), lambda bi, hi, qi, *_: (bi, hi, 0, 0)),
            pl.BlockSpec((1, 1, s // tk, dv, tk), lambda bi, hi, qi, *_: (bi, hi, 0, 0, 0)),
            pl.BlockSpec((dv, 1), lambda bi, hi, qi, *_: (0, 0)),
        ],
        out_specs=pl.BlockSpec((1, tq, dv), lambda bi, hi, qi, *_: (bi, qi, hi)),
        scratch_shapes=[
            pltpu.VMEM((2, tk, tq), F32),
            pltpu.VMEM((2, tk, tq), F32),
            pltpu.VMEM((2, tk, tq), F32),
            pltpu.VMEM((2, tk, tq), BF16),
            pltpu.VMEM((2, 1, tq), F32),
            pltpu.VMEM((2, 1, tq), F32),
            pltpu.VMEM((2, dv, tq), F32),
        ],
    )
    return pl.pallas_call(
        functools.partial(_diff_attn_kernel, lam_init=lam_init),
        grid_spec=grid_spec,
        out_shape=jax.ShapeDtypeStruct((b, s, h * dv), BF16),
        compiler_params=_cparams(("arbitrary", "arbitrary", "arbitrary")),
        name="diff_attention",
    )(rel_bias.reshape(-1).astype(F32), lam.reshape(1).astype(F32), qT, k, vT5, subln_g.reshape(dv, 1).astype(F32))


def _sb_attn_kernel(qT_ref, k_ref, vT_ref, o_ref, z_a, z_b, w_a, w_b, hl_a, hl_b, af_a, af_b, a_a, a_b,
                    acc_sc, carry_sc):
    qi = pl.program_id(2)
    tq = qT_ref.shape[-1]
    gk = vT_ref.shape[-1]
    tk = SB_TK
    nt = gk // tk
    ch = ATT_CHUNK
    d = qT_ref.shape[2]
    qs = (qT_ref[0, 0].astype(F32) * (d ** -0.5)).astype(BF16)

    sr = lax.broadcasted_iota(jnp.int32, (tk, 2 * tk), 0)
    sc = lax.broadcasted_iota(jnp.int32, (tk, 2 * tk), 1)
    suffix = jnp.where((sc & (tk - 1)) > sr, 1.0, 0.0).astype(BF16)

    acc_sc[...] = jnp.zeros(acc_sc.shape, F32)
    carry_sc[...] = jnp.zeros(carry_sc.shape, F32)

    def qk(g, dst):
        base = pl.multiple_of(g * gk, gk)
        dst[...] = jnp.dot(k_ref[0, 0, pl.ds(base, gk), :], qs, preferred_element_type=F32)

    def strict_mask(row0):
        r = lax.broadcasted_iota(jnp.int32, (ch, tq), 0) + row0
        c = lax.broadcasted_iota(jnp.int32, (ch, tq), 1)
        return r < c

    def first_pass(st, diag):
        z_sc, w_sc, hl_sc, af_sc, _ = st
        lk0 = []
        for t in range(nt):
            for c in range(tk // ch):
                row0 = t * tk + c * ch
                z = z_sc[row0:row0 + ch, :]
                neg_abs = lax.bitcast_convert_type(
                    lax.bitcast_convert_type(z, jnp.uint32) | jnp.uint32(0x80000000), F32)
                lg = jnp.log(1.0 + jnp.exp(neg_abs))
                w = jnp.minimum(z, 0.0) - lg
                lk = w - z
                if diag:
                    lk = jnp.where(strict_mask(row0), lk, 0.0)
                hi = lax.bitcast_convert_type(
                    lax.bitcast_convert_type(lk, jnp.uint32) & jnp.uint32(0xFFFF0000), F32)
                hl_sc[t, c * ch:(c + 1) * ch, :] = hi.astype(BF16)
                hl_sc[t, tk + c * ch:tk + (c + 1) * ch, :] = (lk - hi).astype(BF16)
                w_sc[row0:row0 + ch, :] = w
                if c == 0:
                    lk0.append(lk[0:1, :])
            af_sc[t] = jnp.dot(suffix, hl_sc[t], preferred_element_type=F32)
        return lk0

    def second_pass(st, lk0, g, diag):
        _, w_sc, _, af_sc, a_sc = st
        carry = carry_sc[...]
        for t in reversed(range(nt)):
            for c in range(tk // ch):
                row0 = t * tk + c * ch
                a = jnp.exp(w_sc[row0:row0 + ch, :] + af_sc[t, c * ch:(c + 1) * ch, :] + carry)
                if diag:
                    a = jnp.where(strict_mask(row0), a, 0.0)
                a_sc[row0:row0 + ch, :] = a.astype(BF16)
            carry = carry + af_sc[t, 0:1, :] + lk0[t]
        carry_sc[...] = carry
        acc_sc[...] += jnp.dot(vT_ref[0, 0, g], a_sc[...], preferred_element_type=F32)

    st_a = (z_a, w_a, hl_a, af_a, a_a)
    st_b = (z_b, w_b, hl_b, af_b, a_b)
    qk(qi, z_a)

    @pl.when(qi == 0)
    def _():
        second_pass(st_a, first_pass(st_a, True), qi, True)

    @pl.when(qi >= 1)
    def _():
        qk(qi - 1, z_b)
        second_pass(st_a, first_pass(st_a, True), qi, True)
        n_pairs = (qi - 1) // 2

        def pair(i, carry):
            g = qi - 1 - 2 * i
            qk(g - 1, z_a)
            lk_b = first_pass(st_b, False)
            qk(g - 2, z_b)
            lk_a = first_pass(st_a, False)
            second_pass(st_b, lk_b, g, False)
            second_pass(st_a, lk_a, g - 1, False)
            return carry

        lax.fori_loop(0, n_pairs, pair, 0)

        @pl.when(qi % 2 == 0)
        def _():
            qk(0, z_a)
            lk_b = first_pass(st_b, False)
            lk_a = first_pass(st_a, False)
            second_pass(st_b, lk_b, 1, False)
            second_pass(st_a, lk_a, 0, False)

        @pl.when(qi % 2 == 1)
        def _():
            second_pass(st_b, first_pass(st_b, False), 0, False)

    o_ref[0] = acc_sc[...].T.astype(o_ref.dtype)


def _sb_attention(qT, k, vT5):
    b, h, d, s = qT.shape
    tq = gk = SB_GROUP_KEYS
    nt = gk // SB_TK
    two = lambda shape, dt: [pltpu.VMEM(shape, dt), pltpu.VMEM(shape, dt)]
    return pl.pallas_call(
        _sb_attn_kernel,
        grid=(b, h, s // tq),
        in_specs=[
            pl.BlockSpec((1, 1, d, tq), lambda bi, hi, qi: (bi, hi, 0, qi)),
            pl.BlockSpec((1, 1, s, d), lambda bi, hi, qi: (bi, hi, 0, 0)),
            pl.BlockSpec((1, 1, s // gk, d, gk), lambda bi, hi, qi: (bi, hi, 0, 0, 0)),
        ],
        out_specs=pl.BlockSpec((1, tq, d), lambda bi, hi, qi: (bi, qi, hi)),
        out_shape=jax.ShapeDtypeStruct((b, s, h * d), BF16),
        scratch_shapes=(two((gk, tq), F32) + two((gk, tq), F32) + two((nt, 2 * SB_TK, tq), BF16)
                        + two((nt, SB_TK, tq), F32) + two((gk, tq), BF16)
                        + [pltpu.VMEM((d, tq), F32), pltpu.VMEM((1, tq), F32)]),
        compiler_params=_cparams(("arbitrary", "arbitrary", "arbitrary")),
        name="sb_attention",
    )(qT, k, vT5)


def _mem_attn_kernel(q_ref, kv_ref, o_ref):
    hd = q_ref.shape[-1]
    d = MEM_DIM
    outs = []
    for hh in range(hd // d):
        q = q_ref[0, :, hh * d:(hh + 1) * d]
        kk = kv_ref[0, :, hh * d:(hh + 1) * d]
        vv = kv_ref[0, :, hd + hh * d:hd + (hh + 1) * d]
        s = lax.dot_general(q, kk, (((1,), (1,)), ((), ())), preferred_element_type=F32) * (d ** -0.5)
        m = jnp.max(s, axis=-1, keepdims=True)
        p = jnp.exp(s - m)
        p = p / jnp.sum(p, axis=-1, keepdims=True)
        outs.append(jnp.dot(p.astype(BF16), vv, preferred_element_type=F32))
    o_ref[0] = jnp.concatenate(outs, axis=-1).astype(o_ref.dtype)


def _mem_attention(q, kv, tm):
    b, s, hd = q.shape
    mlen = kv.shape[1]
    return pl.pallas_call(
        _mem_attn_kernel,
        grid=(b, s // tm),
        in_specs=[pl.BlockSpec((1, tm, hd), lambda bi, i: (bi, i, 0)),
                  pl.BlockSpec((1, mlen, 2 * hd), lambda bi, i: (bi, 0, 0))],
        out_specs=pl.BlockSpec((1, tm, hd), lambda bi, i: (bi, i, 0)),
        out_shape=jax.ShapeDtypeStruct((b, s, hd), BF16),
        compiler_params=_cparams(("arbitrary", "arbitrary")),
        name="mem_attention",
    )(q, kv)


def _merge_kernel(ya_ref, yb_ref, yc_ref, wa_ref, wb_ref, wc_ref, ga_ref, gb_ref, gc_ref, o_ref):
    pa = jnp.dot(ya_ref[...], wa_ref[...], preferred_element_type=F32)
    pb = jnp.dot(yb_ref[...], wb_ref[...], preferred_element_type=F32)
    pc = jnp.dot(yc_ref[...], wc_ref[...], preferred_element_type=F32)
    merged = (ga_ref[...].astype(F32) * pa + gb_ref[...].astype(F32) * pb + gc_ref[...].astype(F32) * pc)
    o_ref[...] = merged.astype(o_ref.dtype)


def _merge(ya, yb, yc, wa, wb, wc, gates, d_model, tm, tn):
    m = ya.shape[0]
    nblk = d_model // tn
    y_spec = lambda arr: pl.BlockSpec((tm, arr.shape[1]), lambda j, i: (i, 0))
    w_spec = lambda arr: pl.BlockSpec((arr.shape[0], tn), lambda j, i: (0, j))
    g_spec = lambda br: pl.BlockSpec((tm, tn), lambda j, i, br=br: (i, br * nblk + j))
    return pl.pallas_call(
        _merge_kernel,
        grid=(nblk, m // tm),
        in_specs=[y_spec(ya), y_spec(yb), y_spec(yc), w_spec(wa), w_spec(wb), w_spec(wc),
                  g_spec(0), g_spec(1), g_spec(2)],
        out_specs=pl.BlockSpec((tm, tn), lambda j, i: (i, j)),
        out_shape=jax.ShapeDtypeStruct((m, d_model), BF16),
        compiler_params=_cparams(("arbitrary", "arbitrary")),
        name="gated_merge",
    )(ya, yb, yc, wa, wb, wc, gates, gates, gates)


def _pack_bf16_pairs(lo, hi):
    lo_bits = lax.bitcast_convert_type(lo.astype(BF16).astype(F32), jnp.uint32)
    hi_bits = lax.bitcast_convert_type(hi.astype(BF16).astype(F32), jnp.uint32)
    return (hi_bits & jnp.uint32(0xFFFF0000)) | (lo_bits >> 16)


def _unpack_bf16_pairs(words):
    lo = lax.bitcast_convert_type(words << 16, F32)
    hi = lax.bitcast_convert_type(words & jnp.uint32(0xFFFF0000), F32)
    return lo, hi


def _outproj_kernel(mg_ref, w_ref, x_ref, g_ref, wrh_ref, wrl_ref, br_ref, x1_ref, h2_ref, lg_ref):
    x1 = x_ref[...] + jnp.dot(mg_ref[...], w_ref[...], preferred_element_type=F32)
    x1_ref[...] = x1
    ms = jnp.mean(x1 * x1, axis=-1, keepdims=True)
    h2 = x1 * lax.rsqrt(ms + NORM_EPS) * g_ref[...]
    h_hi = h2.astype(BF16)
    half = h2.shape[-1] // 2
    h2_ref[...] = _pack_bf16_pairs(h2[:, :half], h2[:, half:])
    h_lo = (h2 - h_hi.astype(F32)).astype(BF16)
    lg = (jnp.dot(h_hi, wrh_ref[...], preferred_element_type=F32)
          + jnp.dot(h_lo, wrh_ref[...], preferred_element_type=F32)
          + jnp.dot(h_hi, wrl_ref[...], preferred_element_type=F32))
    lg_ref[...] = lg + br_ref[...]


def _outproj(merged, w_out, x, g_ffn, wr_hi, wr_lo, br_pad, tm):
    m, d = x.shape
    row = lambda n: pl.BlockSpec((tm, n), lambda i: (i, 0))
    full = lambda arr: pl.BlockSpec(arr.shape, lambda i: (0, 0))
    return pl.pallas_call(
        _outproj_kernel,
        grid=(m // tm,),
        in_specs=[row(d), full(w_out), row(d), full(g_ffn), full(wr_hi), full(wr_lo), full(br_pad)],
        out_specs=[row(d), row(d // 2), row(ROUTER_PAD)],
        out_shape=[jax.ShapeDtypeStruct((m, d), F32), jax.ShapeDtypeStruct((m, d // 2), jnp.uint32),
                   jax.ShapeDtypeStruct((m, ROUTER_PAD), F32)],
        compiler_params=_cparams(("arbitrary",)),
        name="outproj_norm_router",
    )(merged, w_out, x, g_ffn, wr_hi, wr_lo, br_pad)


def _topk_kernel(lg_ref, idx_ref, gate_ref, rank_ref, cnt_ref, run_sc):
    @pl.when(pl.program_id(0) == 0)
    def _():
        run_sc[...] = jnp.zeros(run_sc.shape, F32)

    l = lg_ref[...]
    tm = l.shape[0]
    lane = lax.broadcasted_iota(jnp.int32, l.shape, 1)
    vals, ids = [], []
    for _ in range(TOP_K):
        m = jnp.max(l, axis=-1, keepdims=True)
        idx = jnp.min(jnp.where(l == m, lane, l.shape[-1]), axis=-1, keepdims=True)
        vals.append(m)
        ids.append(idx)
        l = jnp.where(lane == idx, -jnp.inf, l)
    es = [jnp.exp(v - vals[0]) for v in vals]
    den = es[0]
    for e in es[1:]:
        den = den + e

    chosen = jnp.zeros(l.shape, F32)
    for kk in range(TOP_K):
        chosen = jnp.where(lane == ids[kk], 1.0, chosen)
    tr = lax.broadcasted_iota(jnp.int32, (tm, tm), 0)
    tc = lax.broadcasted_iota(jnp.int32, (tm, tm), 1)
    earlier = jnp.where(tc < tr, 1.0, 0.0).astype(BF16)
    before = jnp.dot(earlier, chosen.astype(BF16), preferred_element_type=F32) + run_sc[...]

    idx_out = jnp.zeros(l.shape, jnp.int32)
    gate_out = jnp.zeros(l.shape, F32)
    rank_out = jnp.zeros(l.shape, F32)
    for kk in range(TOP_K):
        idx_out = jnp.where(lane == kk, ids[kk], idx_out)
        gate_out = jnp.where(lane == kk, es[kk] / den, gate_out)
        rk = jnp.sum(jnp.where(lane == ids[kk], before, 0.0), axis=-1, keepdims=True)
        rank_out = jnp.where(lane == kk, rk, rank_out)
    idx_ref[...] = idx_out
    gate_ref[...] = gate_out
    rank_ref[...] = rank_out.astype(jnp.int32)
    run = run_sc[...] + jnp.sum(chosen, axis=0, keepdims=True)
    run_sc[...] = run
    cnt_ref[...] = run.astype(jnp.int32)


def _topk(logits, tm):
    m, n = logits.shape
    spec = pl.BlockSpec((tm, n), lambda i: (i, 0))
    return pl.pallas_call(
        _topk_kernel,
        grid=(m // tm,),
        in_specs=[spec],
        out_specs=[spec, spec, spec, pl.BlockSpec((1, n), lambda i: (0, 0))],
        out_shape=[jax.ShapeDtypeStruct((m, n), jnp.int32), jax.ShapeDtypeStruct((m, n), F32),
                   jax.ShapeDtypeStruct((m, n), jnp.int32), jax.ShapeDtypeStruct((1, n), jnp.int32)],
        scratch_shapes=[pltpu.VMEM((1, n), F32)],
        compiler_params=_cparams(("arbitrary",)),
        name="router_topk",
    )(logits)


def _weight_group_prefetch(s, se_ref, sj_ref, sfirst_ref, ne_ref, nj_ref, hn_ref, copies, cast):
    @pl.when(s == 0)
    def _():
        for c in copies(se_ref[0], sj_ref[0]):
            c.start()

    @pl.when(sfirst_ref[s] == 1)
    def _():
        for c in copies(se_ref[s], sj_ref[s]):
            c.wait()
        cast()

        @pl.when(hn_ref[s] == 1)
        def _():
            for c in copies(ne_ref[s], nj_ref[s]):
                c.start()


def _ffn1_kernel(sb_ref, sj_ref, se_ref, sfirst_ref, svalid_ref, ne_ref, nj_ref, hn_ref,
                 x_ref, w_hbm, bg_ref, bl_ref, o_ref, stg_g, stg_l, wg_sc, wl_sc, sem):
    s = pl.program_id(0)
    tn = wg_sc.shape[1]
    ff = w_hbm.shape[2] // 2

    def copies(e, j):
        col = pl.multiple_of(j * tn, tn)
        return (pltpu.make_async_copy(w_hbm.at[e, :, pl.ds(col, tn)], stg_g, sem.at[0]),
                pltpu.make_async_copy(w_hbm.at[e, :, pl.ds(ff + col, tn)], stg_l, sem.at[1]))

    def cast():
        wg_sc[...] = stg_g[...].astype(BF16)
        wl_sc[...] = stg_l[...].astype(BF16)

    _weight_group_prefetch(s, se_ref, sj_ref, sfirst_ref, ne_ref, nj_ref, hn_ref, copies, cast)

    @pl.when(svalid_ref[s] == 1)
    def _():
        x_lo, x_hi = _unpack_bf16_pairs(x_ref[...])
        x_lo = x_lo.astype(BF16)
        x_hi = x_hi.astype(BF16)
        half = x_lo.shape[-1]

        def proj(w_sc, b_ref):
            return (jnp.dot(x_lo, w_sc[0:half, :], preferred_element_type=F32)
                    + jnp.dot(x_hi, w_sc[half:, :], preferred_element_type=F32) + b_ref[0])

        a_glu = jnp.minimum(proj(wg_sc, bg_ref), SWIGLU_LIMIT)
        a_lin = jnp.clip(proj(wl_sc, bl_ref), -SWIGLU_LIMIT, SWIGLU_LIMIT)
        act = a_glu * jax.nn.sigmoid(SWIGLU_ALPHA * a_glu) * (a_lin + 1.0)
        o_ref[...] = act.astype(o_ref.dtype)

    @pl.when(svalid_ref[s] == 0)
    def _():
        o_ref[...] = jnp.zeros(o_ref.shape, o_ref.dtype)


def _ffn2_kernel(sb_ref, sj_ref, se_ref, sfirst_ref, svalid_ref, ne_ref, nj_ref, hn_ref,
                 h_ref, w_hbm, b_ref, o_ref, stg, w_sc, sem):
    s = pl.program_id(0)
    tn = w_sc.shape[1]

    def copies(e, j):
        col = pl.multiple_of(j * tn, tn)
        return (pltpu.make_async_copy(w_hbm.at[e, :, pl.ds(col, tn)], stg, sem.at[0]),)

    def cast():
        w_sc[...] = stg[...].astype(BF16)

    _weight_group_prefetch(s, se_ref, sj_ref, sfirst_ref, ne_ref, nj_ref, hn_ref, copies, cast)

    @pl.when(svalid_ref[s] == 1)
    def _():
        y = jnp.dot(h_ref[...], w_sc[...], preferred_element_type=F32) + b_ref[0]
        half = y.shape[-1] // 2
        o_ref[...] = _pack_bf16_pairs(y[:, :half], y[:, half:])

    @pl.when(svalid_ref[s] == 0)
    def _():
        o_ref[...] = jnp.zeros(o_ref.shape, o_ref.dtype)


def _step_tables(nb, n_tiles, n_blocks):
    n_steps = n_blocks * n_tiles
    per_e = nb * n_tiles
    cum_end = jnp.cumsum(per_e)
    total = cum_end[-1]
    used_blocks = jnp.sum(nb)
    blk_start = jnp.cumsum(nb) - nb
    s_raw = jnp.arange(n_steps, dtype=jnp.int32)
    valid = s_raw < total
    s = jnp.minimum(s_raw, total - 1)
    e = jnp.minimum(jnp.sum((s[:, None] >= cum_end[None, :]).astype(jnp.int32), axis=1), N_EXPERTS - 1)
    r = s - (cum_end[e] - per_e[e])
    nbe = jnp.maximum(nb[e], 1)
    j = r // nbe
    bi = r % nbe
    first = jnp.logical_and(bi == 0, valid)
    spare = jnp.maximum(n_blocks - used_blocks, 1)
    r_pad = jnp.maximum(s_raw - total, 0)
    blk = jnp.where(valid, blk_start[e] + bi, used_blocks + r_pad % spare)
    j = jnp.where(valid, j, r_pad // spare)
    nxt = s_raw + nb[e]
    has_next = jnp.logical_and(first, nxt < total)
    nxt = jnp.minimum(nxt, n_steps - 1)
    i32 = lambda a: a.astype(jnp.int32)
    return i32(blk), i32(j), i32(e), i32(first), i32(valid), i32(e[nxt]), i32(j[nxt]), i32(has_next)


def _expert_ffn(xs, nb, w1, b1, w2, b2):
    p = xs.shape[0]
    d = 2 * xs.shape[1]
    ff = w2.shape[1]
    n_blocks = p // MOE_BLK
    nj1 = ff // FFN1_TN
    nj2 = d // FFN2_TN
    b1r = b1.reshape(N_EXPERTS, 1, 2 * ff)
    b2r = b2.reshape(N_EXPERTS, 1, d)

    t1 = _step_tables(nb, nj1, n_blocks)
    h = pl.pallas_call(
        _ffn1_kernel,
        grid_spec=pltpu.PrefetchScalarGridSpec(
            num_scalar_prefetch=len(t1),
            grid=(n_blocks * nj1,),
            in_specs=[
                pl.BlockSpec((MOE_BLK, d // 2), lambda s, sb, sj, se, *_: (sb[s], 0)),
                pl.BlockSpec(memory_space=pl.ANY),
                pl.BlockSpec((1, 1, FFN1_TN), lambda s, sb, sj, se, *_: (se[s], 0, sj[s])),
                pl.BlockSpec((1, 1, FFN1_TN), lambda s, sb, sj, se, *_: (se[s], 0, nj1 + sj[s])),
            ],
            out_specs=pl.BlockSpec((MOE_BLK, FFN1_TN), lambda s, sb, sj, se, *_: (sb[s], sj[s])),
            scratch_shapes=[pltpu.VMEM((d, FFN1_TN), F32), pltpu.VMEM((d, FFN1_TN), F32),
                            pltpu.VMEM((d, FFN1_TN), BF16), pltpu.VMEM((d, FFN1_TN), BF16),
                            pltpu.SemaphoreType.DMA((2,))],
        ),
        out_shape=jax.ShapeDtypeStruct((p, ff), BF16),
        compiler_params=_cparams(("arbitrary",)),
        name="expert_ffn_up",
    )(*t1, xs, w1, b1r, b1r)

    t2 = _step_tables(nb, nj2, n_blocks)
    ys = pl.pallas_call(
        _ffn2_kernel,
        grid_spec=pltpu.PrefetchScalarGridSpec(
            num_scalar_prefetch=len(t2),
            grid=(n_blocks * nj2,),
            in_specs=[
                pl.BlockSpec((MOE_BLK, ff), lambda s, sb, sj, se, *_: (sb[s], 0)),
                pl.BlockSpec(memory_space=pl.ANY),
                pl.BlockSpec((1, 1, FFN2_TN), lambda s, sb, sj, se, *_: (se[s], 0, sj[s])),
            ],
            out_specs=pl.BlockSpec((MOE_BLK, FFN2_TN // 2), lambda s, sb, sj, se, *_: (sb[s], sj[s])),
            scratch_shapes=[pltpu.VMEM((ff, FFN2_TN), F32), pltpu.VMEM((ff, FFN2_TN), BF16),
                            pltpu.SemaphoreType.DMA((1,))],
        ),
        out_shape=jax.ShapeDtypeStruct((p, d // 2), jnp.uint32),
        compiler_params=_cparams(("arbitrary",)),
        name="expert_ffn_down",
    )(*t2, h, w2, b2r)
    return ys


def _dispatch_kernel(dest_ref, h_ref, xs_in_ref, xs_ref, sem):
    del xs_in_ref
    tm = h_ref.shape[0]
    base = pl.program_id(0) * (tm * TOP_K)

    def row_copy(r, kk):
        dst = dest_ref[base + r * TOP_K + kk]
        return pltpu.make_async_copy(h_ref.at[pl.ds(r, 1), :], xs_ref.at[pl.ds(dst, 1), :], sem)

    def issue(r, carry):
        for kk in range(TOP_K):
            row_copy(r, kk).start(priority=kk % 2)
        return carry

    lax.fori_loop(0, tm, issue, 0, unroll=DMA_UNROLL)

    def drain(r, carry):
        for kk in range(TOP_K):
            row_copy(r, kk).wait()
        return carry

    lax.fori_loop(0, tm, drain, 0, unroll=DMA_UNROLL)


def _dispatch(h2p, dest_flat, p, tm):
    t, w = h2p.shape
    xs0 = jnp.zeros((p, w), h2p.dtype)
    return pl.pallas_call(
        _dispatch_kernel,
        grid_spec=pltpu.PrefetchScalarGridSpec(
            num_scalar_prefetch=1,
            grid=(t // tm,),
            in_specs=[pl.BlockSpec((tm, w), lambda i, dest: (i, 0)),
                      pl.BlockSpec(memory_space=pl.ANY)],
            out_specs=pl.BlockSpec(memory_space=pl.ANY),
            scratch_shapes=[pltpu.SemaphoreType.DMA(())],
        ),
        out_shape=jax.ShapeDtypeStruct((p, w), h2p.dtype),
        input_output_aliases={2: 0},
        compiler_params=_cparams(("arbitrary",)),
        name="moe_dispatch",
    )(dest_flat, h2p, xs0)


def _combine_kernel(dest_ref, x1_ref, gate_ref, g_ref, ys_ref, o_ref, buf, sem):
    i = pl.program_id(0)
    n = pl.num_programs(0)
    tm = x1_ref.shape[0]
    tn2 = FFN2_TN // 2

    def row_copy(blk, slot, r, kk):
        src = dest_ref[(blk * tm + r) * TOP_K + kk]
        return pltpu.make_async_copy(ys_ref.at[pl.ds(src, 1), :], buf.at[slot, pl.ds(kk * tm + r, 1), :],
                                     sem.at[slot])

    def issue_block(blk, slot):
        def body(r, carry):
            for kk in range(TOP_K):
                row_copy(blk, slot, r, kk).start(priority=kk % 2)
            return carry
        lax.fori_loop(0, tm, body, 0, unroll=DMA_UNROLL)

    @pl.when(i == 0)
    def _():
        issue_block(0, 0)

    @pl.when(i + 1 < n)
    def _():
        issue_block(i + 1, (i + 1) % 2)

    slot = i % 2

    def drain(r, carry):
        for kk in range(TOP_K):
            row_copy(i, slot, r, kk).wait()
        return carry

    lax.fori_loop(0, tm, drain, 0, unroll=DMA_UNROLL)

    gates = gate_ref[...]
    pieces = [None] * (2 * (2 * buf.shape[-1] // FFN2_TN))
    for kk in range(TOP_K):
        lo, hi = _unpack_bf16_pairs(buf[slot, kk * tm:(kk + 1) * tm, :])
        g = gates[:, kk:kk + 1]
        for j in range(len(pieces) // 2):
            for half, part in enumerate((lo, hi)):
                term = g * part[:, j * tn2:(j + 1) * tn2]
                idx = 2 * j + half
                pieces[idx] = term if pieces[idx] is None else pieces[idx] + term
    x = x1_ref[...] + jnp.concatenate(pieces, axis=-1)
    ms = jnp.mean(x * x, axis=-1, keepdims=True)
    o_ref[...] = (x * lax.rsqrt(ms + NORM_EPS) * g_ref[...]).astype(o_ref.dtype)


def _combine(x1, gates, dest_flat, ys, g_final, tm):
    t, d = x1.shape
    w = ys.shape[1]
    return pl.pallas_call(
        _combine_kernel,
        grid_spec=pltpu.PrefetchScalarGridSpec(
            num_scalar_prefetch=1,
            grid=(t // tm,),
            in_specs=[pl.BlockSpec((tm, d), lambda i, dest: (i, 0)),
                      pl.BlockSpec((tm, gates.shape[1]), lambda i, dest: (i, 0)),
                      pl.BlockSpec((1, d), lambda i, dest: (0, 0)),
                      pl.BlockSpec(memory_space=pl.ANY)],
            out_specs=pl.BlockSpec((tm, d), lambda i, dest: (i, 0)),
            scratch_shapes=[pltpu.VMEM((2, TOP_K * tm, w), ys.dtype), pltpu.SemaphoreType.DMA((2,))],
        ),
        out_shape=jax.ShapeDtypeStruct((t, d), F32),
        compiler_params=_cparams(("arbitrary",)),
        name="moe_combine_final_norm",
    )(dest_flat, x1, gates, g_final.reshape(1, d).astype(F32), ys)


def kernel(x, mem, g_mix, w_in, b_gate, rel_bias, lambda_q1, lambda_k1, lambda_q2, lambda_k2, diff_subln_g, g_mem, w_mem_kv, w_br_diff, w_br_sb, w_br_mem, w_out, g_ffn, w_router, b_router, w_exp1, b_exp1, w_exp2, b_exp2, g_final):
    b, s, d = x.shape
    t = b * s
    depth = g_mix.shape[0]
    dqk_w = DIFF_HEADS * 2 * DIFF_QK_DIM
    dv_w = DIFF_HEADS * DIFF_V_DIM
    sb_w = SB_HEADS * SB_DIM
    mem_w = MEM_HEADS * MEM_DIM
    qkv_w = 2 * dqk_w + dv_w + 3 * sb_w + mem_w
    mlen = mem.shape[1]
    tm = min(ROW_TILE, s)

    xf = x.reshape(t, d)
    for l in range(depth):
        w_in_b = w_in[l].astype(BF16)
        hmix = _rmsnorm(xf, g_mix[l], BF16, tm)
        c0 = 0
        dq = _matmul_heads(hmix, w_in_b, c0, b, s, DIFF_HEADS, 2 * DIFF_QK_DIM, _HEADS_T, name="in_proj_dq")
        c0 += dqk_w
        dk = _matmul_heads(hmix, w_in_b, c0, b, s, DIFF_HEADS, 2 * DIFF_QK_DIM, _HEADS, name="in_proj_dk")
        c0 += dqk_w
        dvv = _matmul_heads(hmix, w_in_b, c0, b, s, DIFF_HEADS, DIFF_V_DIM, _HEADS_T_TILED, tk=DIFF_TILE,
                            name="in_proj_dv")
        c0 += dv_w
        sq = _matmul_heads(hmix, w_in_b, c0, b, s, SB_HEADS, SB_DIM, _HEADS_T, name="in_proj_sq")
        c0 += sb_w
        sk = _matmul_heads(hmix, w_in_b, c0, b, s, SB_HEADS, SB_DIM, _HEADS, name="in_proj_sk")
        c0 += sb_w
        sv = _matmul_heads(hmix, w_in_b, c0, b, s, SB_HEADS, SB_DIM, _HEADS_T_TILED, tk=SB_GROUP_KEYS,
                           name="in_proj_sv")
        c0 += sb_w
        mq = _matmul(hmix, w_in_b, c0, mem_w, name="in_proj_mq")
        gates = _matmul(hmix, w_in_b, qkv_w, 3 * d, bias=b_gate[l], name="in_proj_gates")

        lam_init = 0.8 - 0.6 * math.exp(-0.3 * l)
        lam = (jnp.exp(jnp.sum(lambda_q1[l].astype(F32) * lambda_k1[l].astype(F32)))
               - jnp.exp(jnp.sum(lambda_q2[l].astype(F32) * lambda_k2[l].astype(F32))) + lam_init)
        ya = _diff_attention(dq, dk, dvv, rel_bias, lam, diff_subln_g[l], lam_init).reshape(t, dv_w)
        yb = _sb_attention(sq, sk, sv).reshape(t, sb_w)

        hmem = _rmsnorm(mem.reshape(b * mlen, d), g_mem[l], BF16, min(ROW_TILE, b * mlen))
        kv = _matmul(hmem, w_mem_kv[l].astype(BF16), 0, 2 * mem_w, name="mem_kv_proj")
        yc = _mem_attention(mq.reshape(b, s, mem_w), kv.reshape(b, mlen, 2 * mem_w), tm).reshape(t, mem_w)

        merged = _merge(ya, yb, yc, w_br_diff[l].astype(BF16), w_br_sb[l].astype(BF16),
                        w_br_mem[l].astype(BF16), gates, d, tm, MM_TN)

        wr = jnp.pad(w_router[l].astype(F32), ((0, 0), (0, ROUTER_PAD - N_EXPERTS)))
        wr_hi = wr.astype(BF16)
        wr_lo = (wr - wr_hi.astype(F32)).astype(BF16)
        br_pad = jnp.pad(b_router[l].astype(F32), (0, ROUTER_PAD - N_EXPERTS),
                         constant_values=-jnp.inf).reshape(1, ROUTER_PAD)
        x1, h2, logits = _outproj(merged, w_out[l].astype(BF16), xf, g_ffn[l].reshape(1, d).astype(F32),
                                  wr_hi, wr_lo, br_pad, tm)
        top_i, top_g, rank, counts = _topk(logits, tm)

        n = t * TOP_K
        counts = counts[0, :N_EXPERTS]
        nb = (counts + MOE_BLK - 1) // MOE_BLK
        pad_start = (jnp.cumsum(nb) - nb) * MOE_BLK
        dest = (pad_start[top_i[:, :TOP_K]] + rank[:, :TOP_K]).reshape(n)
        n_blocks = n // MOE_BLK + N_EXPERTS
        p = n_blocks * MOE_BLK

        xs = _dispatch(h2, dest, p, min(DMA_ROWS, s))
        ys = _expert_ffn(xs, nb, w_exp1[l], b_exp1[l], w_exp2[l], b_exp2[l])
        assert l + 1 == depth, "combine is fused with the final norm: single-layer trunk"
        return _combine(x1, top_g, dest, ys, g_final, min(DMA_ROWS, s)).reshape(b, s, d)
```

```python
import functools
import math

import jax
import jax.numpy as jnp
from jax import lax
from jax.experimental import pallas as pl
from jax.experimental.pallas import tpu as pltpu

F32 = jnp.float32
BF16 = jnp.bfloat16

DIFF_HEADS = 8
DIFF_QK_DIM = 64
DIFF_V_DIM = 128
SB_HEADS = 8
SB_DIM = 128
MEM_HEADS = 4
MEM_DIM = 256
N_BUCKETS = 32
MAX_DISTANCE = 128
N_EXPERTS = 32
TOP_K = 4
SWIGLU_LIMIT = 7.0
SWIGLU_ALPHA = 1.702
NORM_EPS = 1e-6
NEG_INF = -1e30
LOG2E = math.log2(math.e)

LANES = 128
VMEM_LIMIT_BYTES = 56 * 1024 * 1024

ROW_TILE = 512
MM_TM = 1024
MM_TN = 1024
DIFF_TILE = 512
ATT_CHUNK = 64
SB_TK = 128
SB_GROUP_KEYS = 512
MOE_BLK = 512
FFN1_TN = 1024
FFN2_TN = 2048
ROUTER_PAD = LANES
DMA_ROWS = 256
DMA_UNROLL = 8


def _cparams(sem):
    return pltpu.CompilerParams(dimension_semantics=sem, vmem_limit_bytes=VMEM_LIMIT_BYTES)


def _rmsnorm_kernel(x_ref, g_ref, o_ref):
    x = x_ref[...].astype(F32)
    ms = jnp.mean(x * x, axis=-1, keepdims=True)
    o_ref[...] = (x * lax.rsqrt(ms + NORM_EPS) * g_ref[...]).astype(o_ref.dtype)


def _rmsnorm(x, g, out_dtype, tm):
    m, d = x.shape
    return pl.pallas_call(
        _rmsnorm_kernel,
        grid=(m // tm,),
        in_specs=[pl.BlockSpec((tm, d), lambda i: (i, 0)),
                  pl.BlockSpec((1, d), lambda i: (0, 0))],
        out_specs=pl.BlockSpec((tm, d), lambda i: (i, 0)),
        out_shape=jax.ShapeDtypeStruct((m, d), out_dtype),
        compiler_params=_cparams(("arbitrary",)),
        name="rmsnorm",
    )(x, g.reshape(1, d).astype(F32))


def _mm_kernel(a_ref, w_ref, o_ref):
    o_ref[...] = jnp.dot(a_ref[...], w_ref[...], preferred_element_type=F32).astype(o_ref.dtype)


def _mm_sigmoid_kernel(a_ref, w_ref, b_ref, o_ref):
    acc = jnp.dot(a_ref[...], w_ref[...], preferred_element_type=F32)
    o_ref[...] = jax.nn.sigmoid(acc + b_ref[...]).astype(o_ref.dtype)


_HEADS, _HEADS_T, _HEADS_T_TILED = "heads", "heads_T", "heads_T_tiled"


def _mm_heads_kernel(a_ref, w_ref, o_ref, *, layout, hd):
    acc = jnp.dot(a_ref[...], w_ref[...], preferred_element_type=F32)
    nh = acc.shape[1] // hd
    if layout == _HEADS:
        for hh in range(nh):
            o_ref[0, hh] = acc[:, hh * hd:(hh + 1) * hd].astype(o_ref.dtype)
        return
    for hh in range(nh):
        acc_t = acc[:, hh * hd:(hh + 1) * hd].T
        if layout == _HEADS_T:
            o_ref[0, hh] = acc_t.astype(o_ref.dtype)
        else:
            tk = o_ref.shape[-1]
            for c in range(o_ref.shape[2]):
                o_ref[0, hh, c] = acc_t[:, c * tk:(c + 1) * tk].astype(o_ref.dtype)


def _matmul_heads(a, w, col_off, b, s, nh, hd, layout, tk=None, tm=MM_TM, name="proj_heads"):
    m, k = a.shape
    tm = min(tm, s)
    n = nh * hd
    off = col_off // n
    per_b = s // tm
    if layout == _HEADS:
        shape, blk = (b, nh, s, hd), (1, nh, tm, hd)
        omap = lambda i: (i // per_b, 0, i % per_b, 0)
    elif layout == _HEADS_T:
        shape, blk = (b, nh, hd, s), (1, nh, hd, tm)
        omap = lambda i: (i // per_b, 0, 0, i % per_b)
    else:
        shape, blk = (b, nh, s // tk, hd, tk), (1, nh, tm // tk, hd, tk)
        omap = lambda i: (i // per_b, 0, i % per_b, 0, 0)
    return pl.pallas_call(
        functools.partial(_mm_heads_kernel, layout=layout, hd=hd),
        grid=(m // tm,),
        in_specs=[pl.BlockSpec((tm, k), lambda i: (i, 0)),
                  pl.BlockSpec((k, n), lambda i: (0, off))],
        out_specs=pl.BlockSpec(blk, omap),
        out_shape=jax.ShapeDtypeStruct(shape, BF16),
        compiler_params=_cparams(("arbitrary",)),
        name=name,
    )(a, w)


def _matmul(a, w, col_off, n_out, bias=None, tm=MM_TM, tn=MM_TN, name="matmul"):
    m, k = a.shape
    tm = min(tm, m)
    off = col_off // tn
    in_specs = [pl.BlockSpec((tm, k), lambda j, i: (i, 0)),
                pl.BlockSpec((k, tn), lambda j, i: (0, j + off))]
    args = [a, w]
    body = _mm_kernel
    if bias is not None:
        in_specs.append(pl.BlockSpec((1, tn), lambda j, i: (0, j)))
        args.append(bias.reshape(1, n_out).astype(F32))
        body = _mm_sigmoid_kernel
    return pl.pallas_call(
        body,
        grid=(n_out // tn, m // tm),
        in_specs=in_specs,
        out_specs=pl.BlockSpec((tm, tn), lambda j, i: (i, j)),
        out_shape=jax.ShapeDtypeStruct((m, n_out), BF16),
        compiler_params=_cparams(("arbitrary", "arbitrary")),
        name=name,
    )(*args)


def _t5_bias_tile(rel_ref, h, offset, tk, tq):
    r = lax.broadcasted_iota(jnp.int32, (tk, tq), 0)
    c = lax.broadcasted_iota(jnp.int32, (tk, tq), 1)
    n = jnp.maximum(c - r + offset, 0)
    max_exact = N_BUCKETS // 2
    nf = jnp.maximum(n, 1).astype(F32)
    large = max_exact + (jnp.log(nf / max_exact) / math.log(MAX_DISTANCE / max_exact)
                         * (N_BUCKETS - max_exact)).astype(jnp.int32)
    large = jnp.minimum(large, N_BUCKETS - 1)
    bucket = jnp.where(n < max_exact, n, large)
    far = rel_ref[(N_BUCKETS - 1) * DIFF_HEADS + h]
    out = jnp.zeros((tk, tq), F32)
    for j in range(N_BUCKETS - 1):
        out = jnp.where(bucket == j, (rel_ref[j * DIFF_HEADS + h] - far) * LOG2E, out)
    return out


def _tree(xs, op):
    while len(xs) > 1:
        nxt = [op(xs[i], xs[i + 1]) for i in range(0, len(xs) - 1, 2)]
        if len(xs) % 2:
            nxt.append(xs[-1])
        xs = nxt
    return xs[0]


def _rows_to_8(x, op):
    return _tree([x[i:i + 8] for i in range(0, x.shape[0], 8)], op)


_FAR, _NEAR, _DIAG = 0, 1, 2


def _diff_attn_kernel(rel_ref, lam_ref, qT_ref, k_ref, vT_ref, g_ref, o_ref,
                      bias_sc, s_a, s_b, mt_a, mt_b, pb_sc, m_sc, l_sc, acc_sc, *, lam_init):
    h = pl.program_id(1)
    qi = pl.program_id(2)
    tq = qT_ref.shape[-1]
    tk = vT_ref.shape[-1]
    ch = ATT_CHUNK
    nch = tk // ch

    @pl.when(qi == 0)
    def _():
        for c in range(nch):
            bias_sc[0, c * ch:(c + 1) * ch, :] = _t5_bias_tile(rel_ref, h, -c * ch, ch, tq)
            bias_sc[1, c * ch:(c + 1) * ch, :] = _t5_bias_tile(rel_ref, h, tk - c * ch, ch, tq)

    qs = (qT_ref[0, 0].astype(F32) * (DIFF_QK_DIM ** -0.5 * LOG2E)).astype(BF16)
    row = lax.broadcasted_iota(jnp.int32, qs.shape, 0)
    zero = jnp.zeros_like(qs)
    qmaps = (jnp.where(row < DIFF_QK_DIM, qs, zero), jnp.where(row >= DIFF_QK_DIM, qs, zero))

    m_sc[...] = jnp.full(m_sc.shape, NEG_INF, F32)
    l_sc[...] = jnp.zeros(l_sc.shape, F32)
    acc_sc[...] = jnp.zeros(acc_sc.shape, F32)

    def scores_max(j, buf, kind):
        dst, mt = buf
        base = pl.multiple_of(j * tk, tk)
        kt = k_ref[0, 0, pl.ds(base, tk), :]
        for mi in range(2):
            dst[mi] = jnp.dot(kt, qmaps[mi], preferred_element_type=F32)
        for mi in range(2):
            parts = []
            for c in range(nch):
                rows = slice(c * ch, (c + 1) * ch)
                x = dst[mi, rows, :]
                if kind != _FAR:
                    x = x + bias_sc[0 if kind == _DIAG else 1, rows, :]
                    if kind == _DIAG:
                        r = lax.broadcasted_iota(jnp.int32, (ch, tq), 0) + c * ch
                        cc = lax.broadcasted_iota(jnp.int32, (ch, tq), 1)
                        x = jnp.where(r <= cc, x, NEG_INF)
                    dst[mi, rows, :] = x
                parts.append(_rows_to_8(x, jnp.maximum))
            mt[mi] = jnp.max(_tree(parts, jnp.maximum), axis=0, keepdims=True)

    def softmax_pv(buf, j):
        src, mt = buf
        vt = vT_ref[0, 0, j]
        for mi in range(2):
            m_old = m_sc[mi]
            m_new = jnp.maximum(m_old, mt[mi])
            alpha = jnp.exp2(m_old - m_new)
            parts = []
            for c in range(nch):
                rows = slice(c * ch, (c + 1) * ch)
                p = jnp.exp2(src[mi, rows, :] - m_new)
                parts.append(_rows_to_8(p, jnp.add))
                pb_sc[mi, rows, :] = p.astype(BF16)
            l_sc[mi] = alpha * l_sc[mi] + jnp.sum(_tree(parts, jnp.add), axis=0, keepdims=True)
            pv = jnp.dot(vt, pb_sc[mi], preferred_element_type=F32)
            acc_sc[mi] = alpha * acc_sc[mi] + pv
            m_sc[mi] = m_new

    buf_a, buf_b = (s_a, mt_a), (s_b, mt_b)
    n_far = jnp.maximum(qi - 1, 0)

    @pl.when(qi == 0)
    def _():
        scores_max(0, buf_a, _DIAG)
        softmax_pv(buf_a, 0)

    @pl.when(qi == 1)
    def _():
        scores_max(0, buf_a, _NEAR)
        scores_max(1, buf_b, _DIAG)
        softmax_pv(buf_a, 0)
        softmax_pv(buf_b, 1)

    @pl.when(qi >= 2)
    def _():
        scores_max(0, buf_a, _FAR)
        n_pairs = (n_far - 1) // 2

        def far_pair(i, carry):
            scores_max(2 * i + 1, buf_b, _FAR)
            softmax_pv(buf_a, 2 * i)
            scores_max(2 * i + 2, buf_a, _FAR)
            softmax_pv(buf_b, 2 * i + 1)
            return carry

        lax.fori_loop(0, n_pairs, far_pair, 0)

        @pl.when(n_far % 2 == 1)
        def _():
            scores_max(qi - 1, buf_b, _NEAR)
            softmax_pv(buf_a, qi - 2)
            scores_max(qi, buf_a, _DIAG)
            softmax_pv(buf_b, qi - 1)
            softmax_pv(buf_a, qi)

        @pl.when(n_far % 2 == 0)
        def _():
            scores_max(qi - 2, buf_b, _FAR)
            softmax_pv(buf_a, qi - 3)
            scores_max(qi - 1, buf_a, _NEAR)
            softmax_pv(buf_b, qi - 2)
            scores_max(qi, buf_b, _DIAG)
            softmax_pv(buf_a, qi - 1)
            softmax_pv(buf_b, qi)

    lam = lam_ref[0]
    o = acc_sc[0] / l_sc[0] - lam * (acc_sc[1] / l_sc[1])
    ms = jnp.mean(o * o, axis=0, keepdims=True)
    y = o * lax.rsqrt(ms + NORM_EPS) * g_ref[...]
    o_ref[0] = (y * (1.0 - lam_init)).T.astype(o_ref.dtype)


def _diff_attention(qT, k, vT5, rel_bias, lam, subln_g, lam_init):
    b, h, dqk, s = qT.shape
    tq = tk = DIFF_TILE
    dv = vT5.shape[3]
    grid_spec = pltpu.PrefetchScalarGridSpec(
        num_scalar_prefetch=2,
        grid=(b, h, s // tq),
        in_specs=[
            pl.BlockSpec((1, 1, dqk, tq), lambda bi, hi, qi, *_: (bi, hi, 0, qi)),
            pl.BlockSpec((1, 1, s, dqk), lambda bi, hi, qi, *_: (bi, hi, 0, 0)),
            pl.BlockSpec((1, 1, s // tk, dv, tk), lambda bi, hi, qi, *_: (bi, hi, 0, 0, 0)),
            pl.BlockSpec((dv, 1), lambda bi, hi, qi, *_: (0, 0)),
        ],
        out_specs=pl.BlockSpec((1, tq, dv), lambda bi, hi, qi, *_: (bi, qi, hi)),
        scratch_shapes=[
            pltpu.VMEM((2, tk, tq), F32),
            pltpu.VMEM((2, tk, tq), F32),
            pltpu.VMEM((2, tk, tq), F32),
            pltpu.VMEM((2, 1, tq), F32),
            pltpu.VMEM((2, 1, tq), F32),
            pltpu.VMEM((2, tk, tq), BF16),
            pltpu.VMEM((2, 1, tq), F32),
            pltpu.VMEM((2, 1, tq), F32),
            pltpu.VMEM((2, dv, tq), F32),
        ],
    )
    return pl.pallas_call(
        functools.partial(_diff_attn_kernel, lam_init=lam_init),
        grid_spec=grid_spec,
        out_shape=jax.ShapeDtypeStruct((b, s, h * dv), BF16),
        compiler_params=_cparams(("arbitrary", "arbitrary", "arbitrary")),
        name="diff_attention",
    )(rel_bias.reshape(-1).astype(F32), lam.reshape(1).astype(F32), qT, k, vT5, subln_g.reshape(dv, 1).astype(F32))


def _sb_attn_kernel(qT_ref, k_ref, vT_ref, o_ref, z_a, z_b, w_a, w_b, hl_a, hl_b, af_a, af_b, a_a, a_b,
                    acc_sc, carry_sc):
    qi = pl.program_id(2)
    tq = qT_ref.shape[-1]
    gk = vT_ref.shape[-1]
    tk = SB_TK
    nt = gk // tk
    ch = ATT_CHUNK
    d = qT_ref.shape[2]
    qs = (qT_ref[0, 0].astype(F32) * (d ** -0.5)).astype(BF16)

    sr = lax.broadcasted_iota(jnp.int32, (tk, tk), 0)
    sc = lax.broadcasted_iota(jnp.int32, (tk, tk), 1)
    suffix = jnp.where(sc > sr, 1.0, 0.0).astype(BF16)

    acc_sc[...] = jnp.zeros(acc_sc.shape, F32)
    carry_sc[...] = jnp.zeros(carry_sc.shape, F32)

    def qk(g, dst):
        base = pl.multiple_of(g * gk, gk)
        dst[...] = jnp.dot(k_ref[0, 0, pl.ds(base, gk), :], qs, preferred_element_type=F32)

    def strict_mask(row0):
        r = lax.broadcasted_iota(jnp.int32, (ch, tq), 0) + row0
        c = lax.broadcasted_iota(jnp.int32, (ch, tq), 1)
        return r < c

    def first_pass(st, diag):
        z_sc, w_sc, hl_sc, af_sc, _ = st
        lk0 = []
        for t in range(nt):
            for c in range(tk // ch):
                row0 = t * tk + c * ch
                z = z_sc[row0:row0 + ch, :]
                neg_abs = lax.bitcast_convert_type(
                    lax.bitcast_convert_type(z, jnp.uint32) | jnp.uint32(0x80000000), F32)
                lg = jnp.log(1.0 + jnp.exp(neg_abs))
                w = jnp.minimum(z, 0.0) - lg
                lk = w - z
                if diag:
                    lk = jnp.where(strict_mask(row0), lk, 0.0)
                hl_sc[t, c * ch:(c + 1) * ch, :] = lk.astype(BF16)
                w_sc[row0:row0 + ch, :] = w
                if c == 0:
                    lk0.append(lk[0:1, :])
            af_sc[t] = jnp.dot(suffix, hl_sc[t], preferred_element_type=F32)
        return lk0

    def second_pass(st, lk0, g, diag):
        _, w_sc, _, af_sc, a_sc = st
        carry = carry_sc[...]
        for t in reversed(range(nt)):
            for c in range(tk // ch):
                row0 = t * tk + c * ch
                a = jnp.exp(w_sc[row0:row0 + ch, :] + af_sc[t, c * ch:(c + 1) * ch, :] + carry)
                if diag:
                    a = jnp.where(strict_mask(row0), a, 0.0)
                a_sc[row0:row0 + ch, :] = a.astype(BF16)
            carry = carry + af_sc[t, 0:1, :] + lk0[t]
        carry_sc[...] = carry
        acc_sc[...] += jnp.dot(vT_ref[0, 0, g], a_sc[...], preferred_element_type=F32)

    st_a = (z_a, w_a, hl_a, af_a, a_a)
    st_b = (z_b, w_b, hl_b, af_b, a_b)
    qk(qi, z_a)

    @pl.when(qi == 0)
    def _():
        second_pass(st_a, first_pass(st_a, True), qi, True)

    @pl.when(qi >= 1)
    def _():
        qk(qi - 1, z_b)
        second_pass(st_a, first_pass(st_a, True), qi, True)
        n_pairs = (qi - 1) // 2

        def pair(i, carry):
            g = qi - 1 - 2 * i
            qk(g - 1, z_a)
            lk_b = first_pass(st_b, False)
            qk(g - 2, z_b)
            lk_a = first_pass(st_a, False)
            second_pass(st_b, lk_b, g, False)
            second_pass(st_a, lk_a, g - 1, False)
            return carry

        lax.fori_loop(0, n_pairs, pair, 0)

        @pl.when(qi % 2 == 0)
        def _():
            qk(0, z_a)
            lk_b = first_pass(st_b, False)
            lk_a = first_pass(st_a, False)
            second_pass(st_b, lk_b, 1, False)
            second_pass(st_a, lk_a, 0, False)

        @pl.when(qi % 2 == 1)
        def _():
            second_pass(st_b, first_pass(st_b, False), 0, False)

    o_ref[0] = acc_sc[...].T.astype(o_ref.dtype)


def _sb_attention(qT, k, vT5):
    b, h, d, s = qT.shape
    tq = gk = SB_GROUP_KEYS
    nt = gk // SB_TK
    two = lambda shape, dt: [pltpu.VMEM(shape, dt), pltpu.VMEM(shape, dt)]
    return pl.pallas_call(
        _sb_attn_kernel,
        grid=(b, h, s // tq),
        in_specs=[
            pl.BlockSpec((1, 1, d, tq), lambda bi, hi, qi: (bi, hi, 0, qi)),
            pl.BlockSpec((1, 1, s, d), lambda bi, hi, qi: (bi, hi, 0, 0)),
            pl.BlockSpec((1, 1, s // gk, d, gk), lambda bi, hi, qi: (bi, hi, 0, 0, 0)),
        ],
        out_specs=pl.BlockSpec((1, tq, d), lambda bi, hi, qi: (bi, qi, hi)),
        out_shape=jax.ShapeDtypeStruct((b, s, h * d), BF16),
        scratch_shapes=(two((gk, tq), F32) + two((gk, tq), F32) + two((nt, SB_TK, tq), BF16)
                        + two((nt, SB_TK, tq), F32) + two((gk, tq), BF16)
                        + [pltpu.VMEM((d, tq), F32), pltpu.VMEM((1, tq), F32)]),
        compiler_params=_cparams(("arbitrary", "arbitrary", "arbitrary")),
        name="sb_attention",
    )(qT, k, vT5)


def _mem_attn_kernel(q_ref, kv_ref, o_ref):
    hd = q_ref.shape[-1]
    d = MEM_DIM
    outs = []
    for hh in range(hd // d):
        q = q_ref[0, :, hh * d:(hh + 1) * d]
        kk = kv_ref[0, :, hh * d:(hh + 1) * d]
        vv = kv_ref[0, :, hd + hh * d:hd + (hh + 1) * d]
        s = lax.dot_general(q, kk, (((1,), (1,)), ((), ())), preferred_element_type=F32) * (d ** -0.5)
        m = jnp.max(s, axis=-1, keepdims=True)
        p = jnp.exp(s - m)
        p = p / jnp.sum(p, axis=-1, keepdims=True)
        outs.append(jnp.dot(p.astype(BF16), vv, preferred_element_type=F32))
    o_ref[0] = jnp.concatenate(outs, axis=-1).astype(o_ref.dtype)


def _mem_attention(q, kv, tm):
    b, s, hd = q.shape
    mlen = kv.shape[1]
    return pl.pallas_call(
        _mem_attn_kernel,
        grid=(b, s // tm),
        in_specs=[pl.BlockSpec((1, tm, hd), lambda bi, i: (bi, i, 0)),
                  pl.BlockSpec((1, mlen, 2 * hd), lambda bi, i: (bi, 0, 0))],
        out_specs=pl.BlockSpec((1, tm, hd), lambda bi, i: (bi, i, 0)),
        out_shape=jax.ShapeDtypeStruct((b, s, hd), BF16),
        compiler_params=_cparams(("arbitrary", "arbitrary")),
        name="mem_attention",
    )(q, kv)


def _merge_kernel(ya_ref, yb_ref, yc_ref, wa_ref, wb_ref, wc_ref, ga_ref, gb_ref, gc_ref, o_ref):
    pa = jnp.dot(ya_ref[...], wa_ref[...], preferred_element_type=F32)
    pb = jnp.dot(yb_ref[...], wb_ref[...], preferred_element_type=F32)
    pc = jnp.dot(yc_ref[...], wc_ref[...], preferred_element_type=F32)
    merged = (ga_ref[...].astype(F32) * pa + gb_ref[...].astype(F32) * pb + gc_ref[...].astype(F32) * pc)
    o_ref[...] = merged.astype(o_ref.dtype)


def _merge(ya, yb, yc, wa, wb, wc, gates, d_model, tm, tn):
    m = ya.shape[0]
    nblk = d_model // tn
    y_spec = lambda arr: pl.BlockSpec((tm, arr.shape[1]), lambda j, i: (i, 0))
    w_spec = lambda arr: pl.BlockSpec((arr.shape[0], tn), lambda j, i: (0, j))
    g_spec = lambda br: pl.BlockSpec((tm, tn), lambda j, i, br=br: (i, br * nblk + j))
    return pl.pallas_call(
        _merge_kernel,
        grid=(nblk, m // tm),
        in_specs=[y_spec(ya), y_spec(yb), y_spec(yc), w_spec(wa), w_spec(wb), w_spec(wc),
                  g_spec(0), g_spec(1), g_spec(2)],
        out_specs=pl.BlockSpec((tm, tn), lambda j, i: (i, j)),
        out_shape=jax.ShapeDtypeStruct((m, d_model), BF16),
        compiler_params=_cparams(("arbitrary", "arbitrary")),
        name="gated_merge",
    )(ya, yb, yc, wa, wb, wc, gates, gates, gates)


def _pack_bf16_pairs(lo, hi):
    lo_bits = lax.bitcast_convert_type(lo.astype(BF16).astype(F32), jnp.uint32)
    hi_bits = lax.bitcast_convert_type(hi.astype(BF16).astype(F32), jnp.uint32)
    return (hi_bits & jnp.uint32(0xFFFF0000)) | (lo_bits >> 16)


def _unpack_bf16_pairs(words):
    lo = lax.bitcast_convert_type(words << 16, F32)
    hi = lax.bitcast_convert_type(words & jnp.uint32(0xFFFF0000), F32)
    return lo, hi


def _outproj_kernel(mg_ref, w_ref, x_ref, g_ref, wrh_ref, wrl_ref, br_ref, x1_ref, h2_ref, lg_ref):
    x1 = x_ref[...] + jnp.dot(mg_ref[...], w_ref[...], preferred_element_type=F32)
    x1_ref[...] = x1
    ms = jnp.mean(x1 * x1, axis=-1, keepdims=True)
    h2 = x1 * lax.rsqrt(ms + NORM_EPS) * g_ref[...]
    h_hi = h2.astype(BF16)
    half = h2.shape[-1] // 2
    h2_ref[...] = _pack_bf16_pairs(h2[:, :half], h2[:, half:])
    h_lo = (h2 - h_hi.astype(F32)).astype(BF16)
    lg = (jnp.dot(h_hi, wrh_ref[...], preferred_element_type=F32)
          + jnp.dot(h_lo, wrh_ref[...], preferred_element_type=F32)
          + jnp.dot(h_hi, wrl_ref[...], preferred_element_type=F32))
    lg_ref[...] = lg + br_ref[...]


def _outproj(merged, w_out, x, g_ffn, wr_hi, wr_lo, br_pad, tm):
    m, d = x.shape
    row = lambda n: pl.BlockSpec((tm, n), lambda i: (i, 0))
    full = lambda arr: pl.BlockSpec(arr.shape, lambda i: (0, 0))
    return pl.pallas_call(
        _outproj_kernel,
        grid=(m // tm,),
        in_specs=[row(d), full(w_out), row(d), full(g_ffn), full(wr_hi), full(wr_lo), full(br_pad)],
        out_specs=[row(d), row(d // 2), row(ROUTER_PAD)],
        out_shape=[jax.ShapeDtypeStruct((m, d), F32), jax.ShapeDtypeStruct((m, d // 2), jnp.uint32),
                   jax.ShapeDtypeStruct((m, ROUTER_PAD), F32)],
        compiler_params=_cparams(("arbitrary",)),
        name="outproj_norm_router",
    )(merged, w_out, x, g_ffn, wr_hi, wr_lo, br_pad)


def _topk_kernel(lg_ref, idx_ref, gate_ref, rank_ref, cnt_ref, run_sc):
    @pl.when(pl.program_id(0) == 0)
    def _():
        run_sc[...] = jnp.zeros(run_sc.shape, F32)

    l = lg_ref[...]
    tm = l.shape[0]
    lane = lax.broadcasted_iota(jnp.int32, l.shape, 1)
    vals, ids = [], []
    for _ in range(TOP_K):
        m = jnp.max(l, axis=-1, keepdims=True)
        idx = jnp.min(jnp.where(l == m, lane, l.shape[-1]), axis=-1, keepdims=True)
        vals.append(m)
        ids.append(idx)
        l = jnp.where(lane == idx, -jnp.inf, l)
    es = [jnp.exp(v - vals[0]) for v in vals]
    den = es[0]
    for e in es[1:]:
        den = den + e

    chosen = jnp.zeros(l.shape, F32)
    for kk in range(TOP_K):
        chosen = jnp.where(lane == ids[kk], 1.0, chosen)
    tr = lax.broadcasted_iota(jnp.int32, (tm, tm), 0)
    tc = lax.broadcasted_iota(jnp.int32, (tm, tm), 1)
    earlier = jnp.where(tc < tr, 1.0, 0.0).astype(BF16)
    before = jnp.dot(earlier, chosen.astype(BF16), preferred_element_type=F32) + run_sc[...]

    idx_out = jnp.zeros(l.shape, jnp.int32)
    gate_out = jnp.zeros(l.shape, F32)
    rank_out = jnp.zeros(l.shape, F32)
    for kk in range(TOP_K):
        idx_out = jnp.where(lane == kk, ids[kk], idx_out)
        gate_out = jnp.where(lane == kk, es[kk] / den, gate_out)
        rk = jnp.sum(jnp.where(lane == ids[kk], before, 0.0), axis=-1, keepdims=True)
        rank_out = jnp.where(lane == kk, rk, rank_out)
    idx_ref[...] = idx_out
    gate_ref[...] = gate_out
    rank_ref[...] = rank_out.astype(jnp.int32)
    run = run_sc[...] + jnp.sum(chosen, axis=0, keepdims=True)
    run_sc[...] = run
    cnt_ref[...] = run.astype(jnp.int32)


def _topk(logits, tm):
    m, n = logits.shape
    spec = pl.BlockSpec((tm, n), lambda i: (i, 0))
    return pl.pallas_call(
        _topk_kernel,
        grid=(m // tm,),
        in_specs=[spec],
        out_specs=[spec, spec, spec, pl.BlockSpec((1, n), lambda i: (0, 0))],
        out_shape=[jax.ShapeDtypeStruct((m, n), jnp.int32), jax.ShapeDtypeStruct((m, n), F32),
                   jax.ShapeDtypeStruct((m, n), jnp.int32), jax.ShapeDtypeStruct((1, n), jnp.int32)],
        scratch_shapes=[pltpu.VMEM((1, n), F32)],
        compiler_params=_cparams(("arbitrary",)),
        name="router_topk",
    )(logits)


def _weight_group_prefetch(s, se_ref, sj_ref, sfirst_ref, ne_ref, nj_ref, hn_ref, copies, cast):
    @pl.when(s == 0)
    def _():
        for c in copies(se_ref[0], sj_ref[0]):
            c.start()

    @pl.when(sfirst_ref[s] == 1)
    def _():
        for c in copies(se_ref[s], sj_ref[s]):
            c.wait()
        cast()

        @pl.when(hn_ref[s] == 1)
        def _():
            for c in copies(ne_ref[s], nj_ref[s]):
                c.start()


def _ffn1_kernel(sb_ref, sj_ref, se_ref, sfirst_ref, svalid_ref, ne_ref, nj_ref, hn_ref,
                 x_ref, w_hbm, bg_ref, bl_ref, o_ref, stg_g, stg_l, wg_sc, wl_sc, sem):
    s = pl.program_id(0)
    tn = wg_sc.shape[1]
    ff = w_hbm.shape[2] // 2

    def copies(e, j):
        col = pl.multiple_of(j * tn, tn)
        return (pltpu.make_async_copy(w_hbm.at[e, :, pl.ds(col, tn)], stg_g, sem.at[0]),
                pltpu.make_async_copy(w_hbm.at[e, :, pl.ds(ff + col, tn)], stg_l, sem.at[1]))

    def cast():
        wg_sc[...] = stg_g[...].astype(BF16)
        wl_sc[...] = stg_l[...].astype(BF16)

    _weight_group_prefetch(s, se_ref, sj_ref, sfirst_ref, ne_ref, nj_ref, hn_ref, copies, cast)

    @pl.when(svalid_ref[s] == 1)
    def _():
        x_lo, x_hi = _unpack_bf16_pairs(x_ref[...])
        x_lo = x_lo.astype(BF16)
        x_hi = x_hi.astype(BF16)
        half = x_lo.shape[-1]

        def proj(w_sc, b_ref):
            return (jnp.dot(x_lo, w_sc[0:half, :], preferred_element_type=F32)
                    + jnp.dot(x_hi, w_sc[half:, :], preferred_element_type=F32) + b_ref[0])

        a_glu = jnp.minimum(proj(wg_sc, bg_ref), SWIGLU_LIMIT)
        a_lin = jnp.clip(proj(wl_sc, bl_ref), -SWIGLU_LIMIT, SWIGLU_LIMIT)
        act = a_glu * jax.nn.sigmoid(SWIGLU_ALPHA * a_glu) * (a_lin + 1.0)
        o_ref[...] = act.astype(o_ref.dtype)

    @pl.when(svalid_ref[s] == 0)
    def _():
        o_ref[...] = jnp.zeros(o_ref.shape, o_ref.dtype)


def _ffn2_kernel(sb_ref, sj_ref, se_ref, sfirst_ref, svalid_ref, ne_ref, nj_ref, hn_ref,
                 h_ref, w_hbm, b_ref, o_ref, stg, w_sc, sem):
    s = pl.program_id(0)
    tn = w_sc.shape[1]

    def copies(e, j):
        col = pl.multiple_of(j * tn, tn)
        return (pltpu.make_async_copy(w_hbm.at[e, :, pl.ds(col, tn)], stg, sem.at[0]),)

    def cast():
        w_sc[...] = stg[...].astype(BF16)

    _weight_group_prefetch(s, se_ref, sj_ref, sfirst_ref, ne_ref, nj_ref, hn_ref, copies, cast)

    @pl.when(svalid_ref[s] == 1)
    def _():
        y = jnp.dot(h_ref[...], w_sc[...], preferred_element_type=F32) + b_ref[0]
        half = y.shape[-1] // 2
        o_ref[...] = _pack_bf16_pairs(y[:, :half], y[:, half:])

    @pl.when(svalid_ref[s] == 0)
    def _():
        o_ref[...] = jnp.zeros(o_ref.shape, o_ref.dtype)


def _step_tables(nb, n_tiles, n_blocks):
    n_steps = n_blocks * n_tiles
    per_e = nb * n_tiles
    cum_end = jnp.cumsum(per_e)
    total = cum_end[-1]
    used_blocks = jnp.sum(nb)
    blk_start = jnp.cumsum(nb) - nb
    s_raw = jnp.arange(n_steps, dtype=jnp.int32)
    valid = s_raw < total
    s = jnp.minimum(s_raw, total - 1)
    e = jnp.minimum(jnp.sum((s[:, None] >= cum_end[None, :]).astype(jnp.int32), axis=1), N_EXPERTS - 1)
    r = s - (cum_end[e] - per_e[e])
    nbe = jnp.maximum(nb[e], 1)
    j = r // nbe
    bi = r % nbe
    first = jnp.logical_and(bi == 0, valid)
    spare = jnp.maximum(n_blocks - used_blocks, 1)
    r_pad = jnp.maximum(s_raw - total, 0)
    blk = jnp.where(valid, blk_start[e] + bi, used_blocks + r_pad % spare)
    j = jnp.where(valid, j, r_pad // spare)
    nxt = s_raw + nb[e]
    has_next = jnp.logical_and(first, nxt < total)
    nxt = jnp.minimum(nxt, n_steps - 1)
    i32 = lambda a: a.astype(jnp.int32)
    return i32(blk), i32(j), i32(e), i32(first), i32(valid), i32(e[nxt]), i32(j[nxt]), i32(has_next)


def _expert_ffn(xs, nb, w1, b1, w2, b2):
    p = xs.shape[0]
    d = 2 * xs.shape[1]
    ff = w2.shape[1]
    n_blocks = p // MOE_BLK
    nj1 = ff // FFN1_TN
    nj2 = d // FFN2_TN
    b1r = b1.reshape(N_EXPERTS, 1, 2 * ff)
    b2r = b2.reshape(N_EXPERTS, 1, d)

    t1 = _step_tables(nb, nj1, n_blocks)
    h = pl.pallas_call(
        _ffn1_kernel,
        grid_spec=pltpu.PrefetchScalarGridSpec(
            num_scalar_prefetch=len(t1),
            grid=(n_blocks * nj1,),
            in_specs=[
                pl.BlockSpec((MOE_BLK, d // 2), lambda s, sb, sj, se, *_: (sb[s], 0)),
                pl.BlockSpec(memory_space=pl.ANY),
                pl.BlockSpec((1, 1, FFN1_TN), lambda s, sb, sj, se, *_: (se[s], 0, sj[s])),
                pl.BlockSpec((1, 1, FFN1_TN), lambda s, sb, sj, se, *_: (se[s], 0, nj1 + sj[s])),
            ],
            out_specs=pl.BlockSpec((MOE_BLK, FFN1_TN), lambda s, sb, sj, se, *_: (sb[s], sj[s])),
            scratch_shapes=[pltpu.VMEM((d, FFN1_TN), F32), pltpu.VMEM((d, FFN1_TN), F32),
                            pltpu.VMEM((d, FFN1_TN), BF16), pltpu.VMEM((d, FFN1_TN), BF16),
                            pltpu.SemaphoreType.DMA((2,))],
        ),
        out_shape=jax.ShapeDtypeStruct((p, ff), BF16),
        compiler_params=_cparams(("arbitrary",)),
        name="expert_ffn_up",
    )(*t1, xs, w1, b1r, b1r)

    t2 = _step_tables(nb, nj2, n_blocks)
    ys = pl.pallas_call(
        _ffn2_kernel,
        grid_spec=pltpu.PrefetchScalarGridSpec(
            num_scalar_prefetch=len(t2),
            grid=(n_blocks * nj2,),
            in_specs=[
                pl.BlockSpec((MOE_BLK, ff), lambda s, sb, sj, se, *_: (sb[s], 0)),
                pl.BlockSpec(memory_space=pl.ANY),
                pl.BlockSpec((1, 1, FFN2_TN), lambda s, sb, sj, se, *_: (se[s], 0, sj[s])),
            ],
            out_specs=pl.BlockSpec((MOE_BLK, FFN2_TN // 2), lambda s, sb, sj, se, *_: (sb[s], sj[s])),
            scratch_shapes=[pltpu.VMEM((ff, FFN2_TN), F32), pltpu.VMEM((ff, FFN2_TN), BF16),
                            pltpu.SemaphoreType.DMA((1,))],
        ),
        out_shape=jax.ShapeDtypeStruct((p, d // 2), jnp.uint32),
        compiler_params=_cparams(("arbitrary",)),
        name="expert_ffn_down",
    )(*t2, h, w2, b2r)
    return ys


def _dispatch_kernel(dest_ref, h_ref, xs_in_ref, xs_ref, sem):
    del xs_in_ref
    tm = h_ref.shape[0]
    base = pl.program_id(0) * (tm * TOP_K)

    def row_copy(r, kk):
        dst = dest_ref[base + r * TOP_K + kk]
        return pltpu.make_async_copy(h_ref.at[pl.ds(r, 1), :], xs_ref.at[pl.ds(dst, 1), :], sem)

    def issue(r, carry):
        for kk in range(TOP_K):
            row_copy(r, kk).start(priority=kk % 2)
        return carry

    lax.fori_loop(0, tm, issue, 0, unroll=DMA_UNROLL)

    def drain(r, carry):
        for kk in range(TOP_K):
            row_copy(r, kk).wait()
        return carry

    lax.fori_loop(0, tm, drain, 0, unroll=DMA_UNROLL)


def _dispatch(h2p, dest_flat, p, tm):
    t, w = h2p.shape
    xs0 = jnp.zeros((p, w), h2p.dtype)
    return pl.pallas_call(
        _dispatch_kernel,
        grid_spec=pltpu.PrefetchScalarGridSpec(
            num_scalar_prefetch=1,
            grid=(t // tm,),
            in_specs=[pl.BlockSpec((tm, w), lambda i, dest: (i, 0)),
                      pl.BlockSpec(memory_space=pl.ANY)],
            out_specs=pl.BlockSpec(memory_space=pl.ANY),
            scratch_shapes=[pltpu.SemaphoreType.DMA(())],
        ),
        out_shape=jax.ShapeDtypeStruct((p, w), h2p.dtype),
        input_output_aliases={2: 0},
        compiler_params=_cparams(("arbitrary",)),
        name="moe_dispatch",
    )(dest_flat, h2p, xs0)


def _combine_kernel(dest_ref, x1_ref, gate_ref, g_ref, ys_ref, o_ref, buf, sem):
    i = pl.program_id(0)
    n = pl.num_programs(0)
    tm = x1_ref.shape[0]
    tn2 = FFN2_TN // 2

    def row_copy(blk, slot, r, kk):
        src = dest_ref[(blk * tm + r) * TOP_K + kk]
        return pltpu.make_async_copy(ys_ref.at[pl.ds(src, 1), :], buf.at[slot, pl.ds(kk * tm + r, 1), :],
                                     sem.at[slot])

    def issue_block(blk, slot):
        def body(r, carry):
            for kk in range(TOP_K):
                row_copy(blk, slot, r, kk).start(priority=kk % 2)
            return carry
        lax.fori_loop(0, tm, body, 0, unroll=DMA_UNROLL)

    @pl.when(i == 0)
    def _():
        issue_block(0, 0)

    @pl.when(i + 1 < n)
    def _():
        issue_block(i + 1, (i + 1) % 2)

    slot = i % 2

    def drain(r, carry):
        for kk in range(TOP_K):
            row_copy(i, slot, r, kk).wait()
        return carry

    lax.fori_loop(0, tm, drain, 0, unroll=DMA_UNROLL)

    gates = gate_ref[...]
    pieces = [None] * (2 * (2 * buf.shape[-1] // FFN2_TN))
    for kk in range(TOP_K):
        lo, hi = _unpack_bf16_pairs(buf[slot, kk * tm:(kk + 1) * tm, :])
        g = gates[:, kk:kk + 1]
        for j in range(len(pieces) // 2):
            for half, part in enumerate((lo, hi)):
                term = g * part[:, j * tn2:(j + 1) * tn2]
                idx = 2 * j + half
                pieces[idx] = term if pieces[idx] is None else pieces[idx] + term
    x = x1_ref[...] + jnp.concatenate(pieces, axis=-1)
    ms = jnp.mean(x * x, axis=-1, keepdims=True)
    o_ref[...] = (x * lax.rsqrt(ms + NORM_EPS) * g_ref[...]).astype(o_ref.dtype)


def _combine(x1, gates, dest_flat, ys, g_final, tm):
    t, d = x1.shape
    w = ys.shape[1]
    return pl.pallas_call(
        _combine_kernel,
        grid_spec=pltpu.PrefetchScalarGridSpec(
            num_scalar_prefetch=1,
            grid=(t // tm,),
            in_specs=[pl.BlockSpec((tm, d), lambda i, dest: (i, 0)),
                      pl.BlockSpec((tm, gates.shape[1]), lambda i, dest: (i, 0)),
                      pl.BlockSpec((1, d), lambda i, dest: (0, 0)),
                      pl.BlockSpec(memory_space=pl.ANY)],
            out_specs=pl.BlockSpec((tm, d), lambda i, dest: (i, 0)),
            scratch_shapes=[pltpu.VMEM((2, TOP_K * tm, w), ys.dtype), pltpu.SemaphoreType.DMA((2,))],
        ),
        out_shape=jax.ShapeDtypeStruct((t, d), F32),
        compiler_params=_cparams(("arbitrary",)),
        name="moe_combine_final_norm",
    )(dest_flat, x1, gates, g_final.reshape(1, d).astype(F32), ys)


def kernel(x, mem, g_mix, w_in, b_gate, rel_bias, lambda_q1, lambda_k1, lambda_q2, lambda_k2, diff_subln_g, g_mem, w_mem_kv, w_br_diff, w_br_sb, w_br_mem, w_out, g_ffn, w_router, b_router, w_exp1, b_exp1, w_exp2, b_exp2, g_final):
    b, s, d = x.shape
    t = b * s
    depth = g_mix.shape[0]
    dqk_w = DIFF_HEADS * 2 * DIFF_QK_DIM
    dv_w = DIFF_HEADS * DIFF_V_DIM
    sb_w = SB_HEADS * SB_DIM
    mem_w = MEM_HEADS * MEM_DIM
    qkv_w = 2 * dqk_w + dv_w + 3 * sb_w + mem_w
    mlen = mem.shape[1]
    tm = min(ROW_TILE, s)

    xf = x.reshape(t, d)
    for l in range(depth):
        w_in_b = w_in[l].astype(BF16)
        hmix = _rmsnorm(xf, g_mix[l], BF16, tm)
        c0 = 0
        dq = _matmul_heads(hmix, w_in_b, c0, b, s, DIFF_HEADS, 2 * DIFF_QK_DIM, _HEADS_T, name="in_proj_dq")
        c0 += dqk_w
        dk = _matmul_heads(hmix, w_in_b, c0, b, s, DIFF_HEADS, 2 * DIFF_QK_DIM, _HEADS, name="in_proj_dk")
        c0 += dqk_w
        dvv = _matmul_heads(hmix, w_in_b, c0, b, s, DIFF_HEADS, DIFF_V_DIM, _HEADS_T_TILED, tk=DIFF_TILE,
                            name="in_proj_dv")
        c0 += dv_w
        sq = _matmul_heads(hmix, w_in_b, c0, b, s, SB_HEADS, SB_DIM, _HEADS_T, name="in_proj_sq")
        c0 += sb_w
        sk = _matmul_heads(hmix, w_in_b, c0, b, s, SB_HEADS, SB_DIM, _HEADS, name="in_proj_sk")
        c0 += sb_w
        sv = _matmul_heads(hmix, w_in_b, c0, b, s, SB_HEADS, SB_DIM, _HEADS_T_TILED, tk=SB_GROUP_KEYS,
                           name="in_proj_sv")
        c0 += sb_w
        mq = _matmul(hmix, w_in_b, c0, mem_w, name="in_proj_mq")
        gates = _matmul(hmix, w_in_b, qkv_w, 3 * d, bias=b_gate[l], name="in_proj_gates")

        lam_init = 0.8 - 0.6 * math.exp(-0.3 * l)
        lam = (jnp.exp(jnp.sum(lambda_q1[l].astype(F32) * lambda_k1[l].astype(F32)))
               - jnp.exp(jnp.sum(lambda_q2[l].astype(F32) * lambda_k2[l].astype(F32))) + lam_init)
        ya = _diff_attention(dq, dk, dvv, rel_bias, lam, diff_subln_g[l], lam_init).reshape(t, dv_w)
        yb = _sb_attention(sq, sk, sv).reshape(t, sb_w)

        hmem = _rmsnorm(mem.reshape(b * mlen, d), g_mem[l], BF16, min(ROW_TILE, b * mlen))
        kv = _matmul(hmem, w_mem_kv[l].astype(BF16), 0, 2 * mem_w, name="mem_kv_proj")
        yc = _mem_attention(mq.reshape(b, s, mem_w), kv.reshape(b, mlen, 2 * mem_w), tm).reshape(t, mem_w)

        merged = _merge(ya, yb, yc, w_br_diff[l].astype(BF16), w_br_sb[l].astype(BF16),
                        w_br_mem[l].astype(BF16), gates, d, tm, MM_TN)

        wr = jnp.pad(w_router[l].astype(F32), ((0, 0), (0, ROUTER_PAD - N_EXPERTS)))
        wr_hi = wr.astype(BF16)
        wr_lo = (wr - wr_hi.astype(F32)).astype(BF16)
        br_pad = jnp.pad(b_router[l].astype(F32), (0, ROUTER_PAD - N_EXPERTS),
                         constant_values=-jnp.inf).reshape(1, ROUTER_PAD)
        x1, h2, logits = _outproj(merged, w_out[l].astype(BF16), xf, g_ffn[l].reshape(1, d).astype(F32),
                                  wr_hi, wr_lo, br_pad, tm)
        top_i, top_g, rank, counts = _topk(logits, tm)

        n = t * TOP_K
        counts = counts[0, :N_EXPERTS]
        nb = (counts + MOE_BLK - 1) // MOE_BLK
        pad_start = (jnp.cumsum(nb) - nb) * MOE_BLK
        dest = (pad_start[top_i[:, :TOP_K]] + rank[:, :TOP_K]).reshape(n)
        n_blocks = n // MOE_BLK + N_EXPERTS
        p = n_blocks * MOE_BLK

        xs = _dispatch(h2, dest, p, min(DMA_ROWS, s))
        ys = _expert_ffn(xs, nb, w_exp1[l], b_exp1[l], w_exp2[l], b_exp2[l])
        assert l + 1 == depth, "combine is fused with the final norm: single-layer trunk"
        return _combine(x1, top_g, dest, ys, g_final, min(DMA_ROWS, s)).reshape(b, s, d)
```

```python
import functools
import math

import jax
import jax.numpy as jnp
from jax import lax
from jax.experimental import pallas as pl
from jax.experimental.pallas import tpu as pltpu

F32 = jnp.float32
BF16 = jnp.bfloat16

DIFF_HEADS = 8
DIFF_QK_DIM = 64
DIFF_V_DIM = 128
SB_HEADS = 8
SB_DIM = 128
MEM_HEADS = 4
MEM_DIM = 256
N_BUCKETS = 32
MAX_DISTANCE = 128
N_EXPERTS = 32
TOP_K = 4
SWIGLU_LIMIT = 7.0
SWIGLU_ALPHA = 1.702
NORM_EPS = 1e-6
NEG_INF = -1e30
LOG2E = math.log2(math.e)

LANES = 128
VMEM_LIMIT_BYTES = 56 * 1024 * 1024

ROW_TILE = 512
MM_TM = 1024
MM_TN = 1024
DIFF_TILE = 512
ATT_CHUNK = 64
SB_TK = 128
SB_GROUP_KEYS = 512
MOE_BLK = 512
FFN1_TN = 1024
FFN2_TN = 2048
ROUTER_PAD = LANES
DMA_ROWS = 512
DMA_UNROLL = 8


def _cparams(sem):
    return pltpu.CompilerParams(dimension_semantics=sem, vmem_limit_bytes=VMEM_LIMIT_BYTES)


def _rmsnorm_kernel(x_ref, g_ref, o_ref):
    x = x_ref[...].astype(F32)
    ms = jnp.mean(x * x, axis=-1, keepdims=True)
    o_ref[...] = (x * lax.rsqrt(ms + NORM_EPS) * g_ref[...]).astype(o_ref.dtype)


def _rmsnorm(x, g, out_dtype, tm):
    m, d = x.shape
    return pl.pallas_call(
        _rmsnorm_kernel,
        grid=(m // tm,),
        in_specs=[pl.BlockSpec((tm, d), lambda i: (i, 0)),
                  pl.BlockSpec((1, d), lambda i: (0, 0))],
        out_specs=pl.BlockSpec((tm, d), lambda i: (i, 0)),
        out_shape=jax.ShapeDtypeStruct((m, d), out_dtype),
        compiler_params=_cparams(("arbitrary",)),
        name="rmsnorm",
    )(x, g.reshape(1, d).astype(F32))


def _mm_kernel(a_ref, w_ref, o_ref):
    o_ref[...] = jnp.dot(a_ref[...], w_ref[...], preferred_element_type=F32).astype(o_ref.dtype)


def _mm_sigmoid_kernel(a_ref, w_ref, b_ref, o_ref):
    acc = jnp.dot(a_ref[...], w_ref[...], preferred_element_type=F32)
    o_ref[...] = jax.nn.sigmoid(acc + b_ref[...]).astype(o_ref.dtype)


_HEADS, _HEADS_T, _HEADS_T_TILED = "heads", "heads_T", "heads_T_tiled"


def _mm_heads_kernel(a_ref, w_ref, o_ref, *, layout, hd):
    acc = jnp.dot(a_ref[...], w_ref[...], preferred_element_type=F32)
    nh = acc.shape[1] // hd
    if layout == _HEADS:
        for hh in range(nh):
            o_ref[0, hh] = acc[:, hh * hd:(hh + 1) * hd].astype(o_ref.dtype)
        return
    for hh in range(nh):
        acc_t = acc[:, hh * hd:(hh + 1) * hd].T
        if layout == _HEADS_T:
            o_ref[0, hh] = acc_t.astype(o_ref.dtype)
        else:
            tk = o_ref.shape[-1]
            for c in range(o_ref.shape[2]):
                o_ref[0, hh, c] = acc_t[:, c * tk:(c + 1) * tk].astype(o_ref.dtype)


def _matmul_heads(a, w, col_off, b, s, nh, hd, layout, tk=None, tm=MM_TM, name="proj_heads"):
    m, k = a.shape
    tm = min(tm, s)
    n = nh * hd
    off = col_off // n
    per_b = s // tm
    if layout == _HEADS:
        shape, blk = (b, nh, s, hd), (1, nh, tm, hd)
        omap = lambda i: (i // per_b, 0, i % per_b, 0)
    elif layout == _HEADS_T:
        shape, blk = (b, nh, hd, s), (1, nh, hd, tm)
        omap = lambda i: (i // per_b, 0, 0, i % per_b)
    else:
        shape, blk = (b, nh, s // tk, hd, tk), (1, nh, tm // tk, hd, tk)
        omap = lambda i: (i // per_b, 0, i % per_b, 0, 0)
    return pl.pallas_call(
        functools.partial(_mm_heads_kernel, layout=layout, hd=hd),
        grid=(m // tm,),
        in_specs=[pl.BlockSpec((tm, k), lambda i: (i, 0)),
                  pl.BlockSpec((k, n), lambda i: (0, off))],
        out_specs=pl.BlockSpec(blk, omap),
        out_shape=jax.ShapeDtypeStruct(shape, BF16),
        compiler_params=_cparams(("arbitrary",)),
        name=name,
    )(a, w)


def _matmul(a, w, col_off, n_out, bias=None, tm=MM_TM, tn=MM_TN, name="matmul"):
    m, k = a.shape
    tm = min(tm, m)
    off = col_off // tn
    in_specs = [pl.BlockSpec((tm, k), lambda j, i: (i, 0)),
                pl.BlockSpec((k, tn), lambda j, i: (0, j + off))]
    args = [a, w]
    body = _mm_kernel
    if bias is not None:
        in_specs.append(pl.BlockSpec((1, tn), lambda j, i: (0, j)))
        args.append(bias.reshape(1, n_out).astype(F32))
        body = _mm_sigmoid_kernel
    return pl.pallas_call(
        body,
        grid=(n_out // tn, m // tm),
        in_specs=in_specs,
        out_specs=pl.BlockSpec((tm, tn), lambda j, i: (i, j)),
        out_shape=jax.ShapeDtypeStruct((m, n_out), BF16),
        compiler_params=_cparams(("arbitrary", "arbitrary")),
        name=name,
    )(*args)


def _t5_bias_tile(rel_ref, h, offset, tk, tq):
    r = lax.broadcasted_iota(jnp.int32, (tk, tq), 0)
    c = lax.broadcasted_iota(jnp.int32, (tk, tq), 1)
    n = jnp.maximum(c - r + offset, 0)
    max_exact = N_BUCKETS // 2
    nf = jnp.maximum(n, 1).astype(F32)
    large = max_exact + (jnp.log(nf / max_exact) / math.log(MAX_DISTANCE / max_exact)
                         * (N_BUCKETS - max_exact)).astype(jnp.int32)
    large = jnp.minimum(large, N_BUCKETS - 1)
    bucket = jnp.where(n < max_exact, n, large)
    far = rel_ref[(N_BUCKETS - 1) * DIFF_HEADS + h]
    out = jnp.zeros((tk, tq), F32)
    for j in range(N_BUCKETS - 1):
        out = jnp.where(bucket == j, (rel_ref[j * DIFF_HEADS + h] - far) * LOG2E, out)
    return out


def _tree(xs, op):
    while len(xs) > 1:
        nxt = [op(xs[i], xs[i + 1]) for i in range(0, len(xs) - 1, 2)]
        if len(xs) % 2:
            nxt.append(xs[-1])
        xs = nxt
    return xs[0]


def _rows_to_8(x, op):
    return _tree([x[i:i + 8] for i in range(0, x.shape[0], 8)], op)


_FAR, _NEAR, _DIAG = 0, 1, 2


def _diff_attn_kernel(rel_ref, lam_ref, qT_ref, k_ref, vT_ref, g_ref, o_ref,
                      bias_sc, s_a, s_b, mt_a, mt_b, pb_sc, m_sc, l_sc, acc_sc, *, lam_init):
    h = pl.program_id(1)
    qi = pl.program_id(2)
    tq = qT_ref.shape[-1]
    tk = vT_ref.shape[-1]
    ch = ATT_CHUNK
    nch = tk // ch

    @pl.when(qi == 0)
    def _():
        for c in range(nch):
            bias_sc[0, c * ch:(c + 1) * ch, :] = _t5_bias_tile(rel_ref, h, -c * ch, ch, tq)
            bias_sc[1, c * ch:(c + 1) * ch, :] = _t5_bias_tile(rel_ref, h, tk - c * ch, ch, tq)

    qs = (qT_ref[0, 0].astype(F32) * (DIFF_QK_DIM ** -0.5 * LOG2E)).astype(BF16)
    row = lax.broadcasted_iota(jnp.int32, qs.shape, 0)
    zero = jnp.zeros_like(qs)
    qmaps = (jnp.where(row < DIFF_QK_DIM, qs, zero), jnp.where(row >= DIFF_QK_DIM, qs, zero))

    m_sc[...] = jnp.full(m_sc.shape, NEG_INF, F32)
    l_sc[...] = jnp.zeros(l_sc.shape, F32)
    acc_sc[...] = jnp.zeros(acc_sc.shape, F32)

    def scores_max(j, buf, kind):
        dst, mt = buf
        base = pl.multiple_of(j * tk, tk)
        kt = k_ref[0, 0, pl.ds(base, tk), :]
        for mi in range(2):
            dst[mi] = jnp.dot(kt, qmaps[mi], preferred_element_type=F32)
        for mi in range(2):
            parts = []
            for c in range(nch):
                rows = slice(c * ch, (c + 1) * ch)
                x = dst[mi, rows, :]
                if kind != _FAR:
                    x = x + bias_sc[0 if kind == _DIAG else 1, rows, :]
                    if kind == _DIAG:
                        r = lax.broadcasted_iota(jnp.int32, (ch, tq), 0) + c * ch
                        cc = lax.broadcasted_iota(jnp.int32, (ch, tq), 1)
                        x = jnp.where(r <= cc, x, NEG_INF)
                    dst[mi, rows, :] = x
                parts.append(_rows_to_8(x, jnp.maximum))
            mt[mi] = jnp.max(_tree(parts, jnp.maximum), axis=0, keepdims=True)

    def softmax_pv(buf, j):
        src, mt = buf
        vt = vT_ref[0, 0, j]
        for mi in range(2):
            m_old = m_sc[mi]
            m_new = jnp.maximum(m_old, mt[mi])
            alpha = jnp.exp2(m_old - m_new)
            parts = []
            for c in range(nch):
                rows = slice(c * ch, (c + 1) * ch)
                p = jnp.exp2(src[mi, rows, :] - m_new)
                parts.append(_rows_to_8(p, jnp.add))
                pb_sc[mi, rows, :] = p.astype(BF16)
            l_sc[mi] = alpha * l_sc[mi] + jnp.sum(_tree(parts, jnp.add), axis=0, keepdims=True)
            pv = jnp.dot(vt, pb_sc[mi], preferred_element_type=F32)
            acc_sc[mi] = alpha * acc_sc[mi] + pv
            m_sc[mi] = m_new

    buf_a, buf_b = (s_a, mt_a), (s_b, mt_b)
    n_far = jnp.maximum(qi - 1, 0)

    @pl.when(qi == 0)
    def _():
        scores_max(0, buf_a, _DIAG)
        softmax_pv(buf_a, 0)

    @pl.when(qi == 1)
    def _():
        scores_max(0, buf_a, _NEAR)
        scores_max(1, buf_b, _DIAG)
        softmax_pv(buf_a, 0)
        softmax_pv(buf_b, 1)

    @pl.when(qi >= 2)
    def _():
        scores_max(0, buf_a, _FAR)
        n_pairs = (n_far - 1) // 2

        def far_pair(i, carry):
            scores_max(2 * i + 1, buf_b, _FAR)
            softmax_pv(buf_a, 2 * i)
            scores_max(2 * i + 2, buf_a, _FAR)
            softmax_pv(buf_b, 2 * i + 1)
            return carry

        lax.fori_loop(0, n_pairs, far_pair, 0)

        @pl.when(n_far % 2 == 1)
        def _():
            scores_max(qi - 1, buf_b, _NEAR)
            softmax_pv(buf_a, qi - 2)
            scores_max(qi, buf_a, _DIAG)
            softmax_pv(buf_b, qi - 1)
            softmax_pv(buf_a, qi)

        @pl.when(n_far % 2 == 0)
        def _():
            scores_max(qi - 2, buf_b, _FAR)
            softmax_pv(buf_a, qi - 3)
            scores_max(qi - 1, buf_a, _NEAR)
            softmax_pv(buf_b, qi - 2)
            scores_max(qi, buf_b, _DIAG)
            softmax_pv(buf_a, qi - 1)
            softmax_pv(buf_b, qi)

    lam = lam_ref[0]
    o = acc_sc[0] / l_sc[0] - lam * (acc_sc[1] / l_sc[1])
    ms = jnp.mean(o * o, axis=0, keepdims=True)
    y = o * lax.rsqrt(ms + NORM_EPS) * g_ref[...]
    o_ref[0] = (y * (1.0 - lam_init)).T.astype(o_ref.dtype)


def _diff_attention(qT, k, vT5, rel_bias, lam, subln_g, lam_init):
    b, h, dqk, s = qT.shape
    tq = tk = DIFF_TILE
    dv = vT5.shape[3]
    grid_spec = pltpu.PrefetchScalarGridSpec(
        num_scalar_prefetch=2,
        grid=(b, h, s // tq),
        in_specs=[
            pl.BlockSpec((1, 1, dqk, tq), lambda bi, hi, qi, *_: (bi, hi, 0, qi)),
            pl.BlockSpec((1, 1, s, dqk), lambda bi, hi, qi, *_: (bi, hi, 0, 0)),
            pl.BlockSpec((1, 1, s // tk, dv, tk), lambda bi, hi, qi, *_: (bi, hi, 0, 0, 0)),
            pl.BlockSpec((dv, 1), lambda bi, hi, qi, *_: (0, 0)),
        ],
        out_specs=pl.BlockSpec((1, tq, dv), lambda bi, hi, qi, *_: (bi, qi, hi)),
        scratch_shapes=[
            pltpu.VMEM((2, tk, tq), F32),
            pltpu.VMEM((2, tk, tq), F32),
            pltpu.VMEM((2, tk, tq), F32),
            pltpu.VMEM((2, 1, tq), F32),
            pltpu.VMEM((2, 1, tq), F32),
            pltpu.VMEM((2, tk, tq), BF16),
            pltpu.VMEM((2, 1, tq), F32),
            pltpu.VMEM((2, 1, tq), F32),
            pltpu.VMEM((2, dv, tq), F32),
        ],
    )
    return pl.pallas_call(
        functools.partial(_diff_attn_kernel, lam_init=lam_init),
        grid_spec=grid_spec,
        out_shape=jax.ShapeDtypeStruct((b, s, h * dv), BF16),
        compiler_params=_cparams(("arbitrary", "arbitrary", "arbitrary")),
        name="diff_attention",
    )(rel_bias.reshape(-1).astype(F32), lam.reshape(1).astype(F32), qT, k, vT5, subln_g.reshape(dv, 1).astype(F32))


def _sb_attn_kernel(qT_ref, k_ref, vT_ref, o_ref, z_a, z_b, w_a, w_b, hl_a, hl_b, af_a, af_b, a_a, a_b,
                    acc_sc, carry_sc):
    qi = pl.program_id(2)
    tq = qT_ref.shape[-1]
    gk = vT_ref.shape[-1]
    tk = SB_TK
    nt = gk // tk
    ch = ATT_CHUNK
    d = qT_ref.shape[2]
    qs = (qT_ref[0, 0].astype(F32) * (d ** -0.5 * LOG2E)).astype(BF16)

    sr = lax.broadcasted_iota(jnp.int32, (tk, tk), 0)
    sc = lax.broadcasted_iota(jnp.int32, (tk, tk), 1)
    suffix = jnp.where(sc > sr, 1.0, 0.0).astype(BF16)

    acc_sc[...] = jnp.zeros(acc_sc.shape, F32)
    carry_sc[...] = jnp.zeros(carry_sc.shape, F32)

    def qk(g, dst):
        base = pl.multiple_of(g * gk, gk)
        dst[...] = jnp.dot(k_ref[0, 0, pl.ds(base, gk), :], qs, preferred_element_type=F32)

    def first_col(row0, diag):
        return (row0 // LANES) * LANES if diag else 0

    def strict_mask(row0, lo):
        r = lax.broadcasted_iota(jnp.int32, (ch, tq - lo), 0) + row0
        c = lax.broadcasted_iota(jnp.int32, (ch, tq - lo), 1) + lo
        return r < c

    def first_pass(st, diag):
        z_sc, w_sc, hl_sc, af_sc, _ = st
        lk0 = []
        for t in range(nt):
            for c in range(tk // ch):
                row0 = t * tk + c * ch
                lo = first_col(row0, diag)
                z = z_sc[row0:row0 + ch, lo:]
                neg_abs = lax.bitcast_convert_type(
                    lax.bitcast_convert_type(z, jnp.uint32) | jnp.uint32(0x80000000), F32)
                lg = jnp.log(1.0 + jnp.exp2(neg_abs)) * LOG2E
                w = jnp.minimum(z, 0.0) - lg
                lk = w - z
                if diag:
                    lk = jnp.where(strict_mask(row0, lo), lk, 0.0)
                hl_sc[t, c * ch:(c + 1) * ch, lo:] = lk.astype(BF16)
                w_sc[row0:row0 + ch, lo:] = w
                lk_row = lk[0:1, :]
                if lo:
                    hl_sc[t, c * ch:(c + 1) * ch, :lo] = jnp.zeros((ch, lo), BF16)
                    lk_row = jnp.concatenate([jnp.zeros((1, lo), F32), lk_row], axis=1)
                if c == 0:
                    lk0.append(lk_row)
            af_sc[t] = jnp.dot(suffix, hl_sc[t], preferred_element_type=F32)
        return lk0

    def second_pass(st, lk0, g, diag):
        _, w_sc, _, af_sc, a_sc = st
        carry = carry_sc[...]
        for t in reversed(range(nt)):
            for c in range(tk // ch):
                row0 = t * tk + c * ch
                lo = first_col(row0, diag)
                a = jnp.exp2(w_sc[row0:row0 + ch, lo:] + af_sc[t, c * ch:(c + 1) * ch, lo:] + carry[:, lo:])
                if diag:
                    a = jnp.where(strict_mask(row0, lo), a, 0.0)
                a_sc[row0:row0 + ch, lo:] = a.astype(BF16)
                if lo:
                    a_sc[row0:row0 + ch, :lo] = jnp.zeros((ch, lo), BF16)
            carry = carry + af_sc[t, 0:1, :] + lk0[t]
        carry_sc[...] = carry
        acc_sc[...] += jnp.dot(vT_ref[0, 0, g], a_sc[...], preferred_element_type=F32)

    st_a = (z_a, w_a, hl_a, af_a, a_a)
    st_b = (z_b, w_b, hl_b, af_b, a_b)
    qk(qi, z_a)

    @pl.when(qi == 0)
    def _():
        second_pass(st_a, first_pass(st_a, True), qi, True)

    @pl.when(qi >= 1)
    def _():
        qk(qi - 1, z_b)
        second_pass(st_a, first_pass(st_a, True), qi, True)
        n_pairs = (qi - 1) // 2

        def pair(i, carry):
            g = qi - 1 - 2 * i
            qk(g - 1, z_a)
            lk_b = first_pass(st_b, False)
            qk(g - 2, z_b)
            lk_a = first_pass(st_a, False)
            second_pass(st_b, lk_b, g, False)
            second_pass(st_a, lk_a, g - 1, False)
            return carry

        lax.fori_loop(0, n_pairs, pair, 0)

        @pl.when(qi % 2 == 0)
        def _():
            qk(0, z_a)
            lk_b = first_pass(st_b, False)
            lk_a = first_pass(st_a, False)
            second_pass(st_b, lk_b, 1, False)
            second_pass(st_a, lk_a, 0, False)

        @pl.when(qi % 2 == 1)
        def _():
            second_pass(st_b, first_pass(st_b, False), 0, False)

    o_ref[0] = acc_sc[...].T.astype(o_ref.dtype)


def _sb_attention(qT, k, vT5):
    b, h, d, s = qT.shape
    tq = gk = SB_GROUP_KEYS
    nt = gk // SB_TK
    two = lambda shape, dt: [pltpu.VMEM(shape, dt), pltpu.VMEM(shape, dt)]
    return pl.pallas_call(
        _sb_attn_kernel,
        grid=(b, h, s // tq),
        in_specs=[
            pl.BlockSpec((1, 1, d, tq), lambda bi, hi, qi: (bi, hi, 0, qi)),
            pl.BlockSpec((1, 1, s, d), lambda bi, hi, qi: (bi, hi, 0, 0)),
            pl.BlockSpec((1, 1, s // gk, d, gk), lambda bi, hi, qi: (bi, hi, 0, 0, 0)),
        ],
        out_specs=pl.BlockSpec((1, tq, d), lambda bi, hi, qi: (bi, qi, hi)),
        out_shape=jax.ShapeDtypeStruct((b, s, h * d), BF16),
        scratch_shapes=(two((gk, tq), F32) + two((gk, tq), F32) + two((nt, SB_TK, tq), BF16)
                        + two((nt, SB_TK, tq), F32) + two((gk, tq), BF16)
                        + [pltpu.VMEM((d, tq), F32), pltpu.VMEM((1, tq), F32)]),
        compiler_params=_cparams(("arbitrary", "arbitrary", "arbitrary")),
        name="sb_attention",
    )(qT, k, vT5)


def _mem_attn_kernel(q_ref, kv_ref, o_ref):
    hd = q_ref.shape[-1]
    d = MEM_DIM
    outs = []
    for hh in range(hd // d):
        q = q_ref[0, :, hh * d:(hh + 1) * d]
        kk = kv_ref[0, :, hh * d:(hh + 1) * d]
        vv = kv_ref[0, :, hd + hh * d:hd + (hh + 1) * d]
        s = lax.dot_general(q, kk, (((1,), (1,)), ((), ())), preferred_element_type=F32) * (d ** -0.5)
        m = jnp.max(s, axis=-1, keepdims=True)
        p = jnp.exp(s - m)
        p = p / jnp.sum(p, axis=-1, keepdims=True)
        outs.append(jnp.dot(p.astype(BF16), vv, preferred_element_type=F32))
    o_ref[0] = jnp.concatenate(outs, axis=-1).astype(o_ref.dtype)


def _mem_attention(q, kv, tm):
    b, s, hd = q.shape
    mlen = kv.shape[1]
    return pl.pallas_call(
        _mem_attn_kernel,
        grid=(b, s // tm),
        in_specs=[pl.BlockSpec((1, tm, hd), lambda bi, i: (bi, i, 0)),
                  pl.BlockSpec((1, mlen, 2 * hd), lambda bi, i: (bi, 0, 0))],
        out_specs=pl.BlockSpec((1, tm, hd), lambda bi, i: (bi, i, 0)),
        out_shape=jax.ShapeDtypeStruct((b, s, hd), BF16),
        compiler_params=_cparams(("arbitrary", "arbitrary")),
        name="mem_attention",
    )(q, kv)


def _merge_kernel(ya_ref, yb_ref, yc_ref, wa_ref, wb_ref, wc_ref, ga_ref, gb_ref, gc_ref, o_ref):
    pa = jnp.dot(ya_ref[...], wa_ref[...], preferred_element_type=F32)
    pb = jnp.dot(yb_ref[...], wb_ref[...], preferred_element_type=F32)
    pc = jnp.dot(yc_ref[...], wc_ref[...], preferred_element_type=F32)
    merged = (ga_ref[...].astype(F32) * pa + gb_ref[...].astype(F32) * pb + gc_ref[...].astype(F32) * pc)
    o_ref[...] = merged.astype(o_ref.dtype)


def _merge(ya, yb, yc, wa, wb, wc, gates, d_model, tm, tn):
    m = ya.shape[0]
    nblk = d_model // tn
    y_spec = lambda arr: pl.BlockSpec((tm, arr.shape[1]), lambda j, i: (i, 0))
    w_spec = lambda arr: pl.BlockSpec((arr.shape[0], tn), lambda j, i: (0, j))
    g_spec = lambda br: pl.BlockSpec((tm, tn), lambda j, i, br=br: (i, br * nblk + j))
    return pl.pallas_call(
        _merge_kernel,
        grid=(nblk, m // tm),
        in_specs=[y_spec(ya), y_spec(yb), y_spec(yc), w_spec(wa), w_spec(wb), w_spec(wc),
                  g_spec(0), g_spec(1), g_spec(2)],
        out_specs=pl.BlockSpec((tm, tn), lambda j, i: (i, j)),
        out_shape=jax.ShapeDtypeStruct((m, d_model), BF16),
        compiler_params=_cparams(("arbitrary", "arbitrary")),
        name="gated_merge",
    )(ya, yb, yc, wa, wb, wc, gates, gates, gates)


def _pack_bf16_pairs(lo, hi):
    lo_bits = lax.bitcast_convert_type(lo.astype(BF16).astype(F32), jnp.uint32)
    hi_bits = lax.bitcast_convert_type(hi.astype(BF16).astype(F32), jnp.uint32)
    return (hi_bits & jnp.uint32(0xFFFF0000)) | (lo_bits >> 16)


def _unpack_bf16_pairs(words):
    lo = lax.bitcast_convert_type(words << 16, F32)
    hi = lax.bitcast_convert_type(words & jnp.uint32(0xFFFF0000), F32)
    return lo, hi


def _outproj_kernel(mg_ref, w_ref, x_ref, g_ref, wrh_ref, wrl_ref, br_ref, x1_ref, h2_ref, lg_ref):
    x1 = x_ref[...] + jnp.dot(mg_ref[...], w_ref[...], preferred_element_type=F32)
    x1_ref[...] = x1
    ms = jnp.mean(x1 * x1, axis=-1, keepdims=True)
    h2 = x1 * lax.rsqrt(ms + NORM_EPS) * g_ref[...]
    h_hi = h2.astype(BF16)
    half = h2.shape[-1] // 2
    h2_ref[...] = _pack_bf16_pairs(h2[:, :half], h2[:, half:])
    h_lo = (h2 - h_hi.astype(F32)).astype(BF16)
    lg = (jnp.dot(h_hi, wrh_ref[...], preferred_element_type=F32)
          + jnp.dot(h_lo, wrh_ref[...], preferred_element_type=F32)
          + jnp.dot(h_hi, wrl_ref[...], preferred_element_type=F32))
    lg_ref[...] = lg + br_ref[...]


def _outproj(merged, w_out, x, g_ffn, wr_hi, wr_lo, br_pad, tm):
    m, d = x.shape
    row = lambda n: pl.BlockSpec((tm, n), lambda i: (i, 0))
    full = lambda arr: pl.BlockSpec(arr.shape, lambda i: (0, 0))
    return pl.pallas_call(
        _outproj_kernel,
        grid=(m // tm,),
        in_specs=[row(d), full(w_out), row(d), full(g_ffn), full(wr_hi), full(wr_lo), full(br_pad)],
        out_specs=[row(d), row(d // 2), row(ROUTER_PAD)],
        out_shape=[jax.ShapeDtypeStruct((m, d), F32), jax.ShapeDtypeStruct((m, d // 2), jnp.uint32),
                   jax.ShapeDtypeStruct((m, ROUTER_PAD), F32)],
        compiler_params=_cparams(("arbitrary",)),
        name="outproj_norm_router",
    )(merged, w_out, x, g_ffn, wr_hi, wr_lo, br_pad)


def _topk_kernel(lg_ref, idx_ref, gate_ref, rank_ref, cnt_ref, run_sc):
    @pl.when(pl.program_id(0) == 0)
    def _():
        run_sc[...] = jnp.zeros(run_sc.shape, F32)

    l = lg_ref[...]
    tm = l.shape[0]
    lane = lax.broadcasted_iota(jnp.int32, l.shape, 1)
    vals, ids = [], []
    for _ in range(TOP_K):
        m = jnp.max(l, axis=-1, keepdims=True)
        idx = jnp.min(jnp.where(l == m, lane, l.shape[-1]), axis=-1, keepdims=True)
        vals.append(m)
        ids.append(idx)
        l = jnp.where(lane == idx, -jnp.inf, l)
    es = [jnp.exp(v - vals[0]) for v in vals]
    den = es[0]
    for e in es[1:]:
        den = den + e

    chosen = jnp.zeros(l.shape, F32)
    for kk in range(TOP_K):
        chosen = jnp.where(lane == ids[kk], 1.0, chosen)
    tr = lax.broadcasted_iota(jnp.int32, (tm, tm), 0)
    tc = lax.broadcasted_iota(jnp.int32, (tm, tm), 1)
    earlier = jnp.where(tc < tr, 1.0, 0.0).astype(BF16)
    before = jnp.dot(earlier, chosen.astype(BF16), preferred_element_type=F32) + run_sc[...]

    idx_out = jnp.zeros(l.shape, jnp.int32)
    gate_out = jnp.zeros(l.shape, F32)
    rank_out = jnp.zeros(l.shape, F32)
    for kk in range(TOP_K):
        idx_out = jnp.where(lane == kk, ids[kk], idx_out)
        gate_out = jnp.where(lane == kk, es[kk] / den, gate_out)
        rk = jnp.sum(jnp.where(lane == ids[kk], before, 0.0), axis=-1, keepdims=True)
        rank_out = jnp.where(lane == kk, rk, rank_out)
    idx_ref[...] = idx_out
    gate_ref[...] = gate_out
    rank_ref[...] = rank_out.astype(jnp.int32)
    run = run_sc[...] + jnp.sum(chosen, axis=0, keepdims=True)
    run_sc[...] = run
    cnt_ref[...] = run.astype(jnp.int32)


def _topk(logits, tm):
    m, n = logits.shape
    spec = pl.BlockSpec((tm, n), lambda i: (i, 0))
    return pl.pallas_call(
        _topk_kernel,
        grid=(m // tm,),
        in_specs=[spec],
        out_specs=[spec, spec, spec, pl.BlockSpec((1, n), lambda i: (0, 0))],
        out_shape=[jax.ShapeDtypeStruct((m, n), jnp.int32), jax.ShapeDtypeStruct((m, n), F32),
                   jax.ShapeDtypeStruct((m, n), jnp.int32), jax.ShapeDtypeStruct((1, n), jnp.int32)],
        scratch_shapes=[pltpu.VMEM((1, n), F32)],
        compiler_params=_cparams(("arbitrary",)),
        name="router_topk",
    )(logits)


def _weight_group_prefetch(s, se_ref, sj_ref, sfirst_ref, ne_ref, nj_ref, hn_ref, copies, cast):
    @pl.when(s == 0)
    def _():
        for c in copies(se_ref[0], sj_ref[0]):
            c.start()

    @pl.when(sfirst_ref[s] == 1)
    def _():
        for c in copies(se_ref[s], sj_ref[s]):
            c.wait()
        cast()

        @pl.when(hn_ref[s] == 1)
        def _():
            for c in copies(ne_ref[s], nj_ref[s]):
                c.start()


def _ffn1_kernel(sb_ref, sj_ref, se_ref, sfirst_ref, svalid_ref, ne_ref, nj_ref, hn_ref,
                 x_ref, w_hbm, bg_ref, bl_ref, o_ref, stg_g, stg_l, wg_sc, wl_sc, sem):
    s = pl.program_id(0)
    tn = wg_sc.shape[1]
    ff = w_hbm.shape[2] // 2

    def copies(e, j):
        col = pl.multiple_of(j * tn, tn)
        return (pltpu.make_async_copy(w_hbm.at[e, :, pl.ds(col, tn)], stg_g, sem.at[0]),
                pltpu.make_async_copy(w_hbm.at[e, :, pl.ds(ff + col, tn)], stg_l, sem.at[1]))

    def cast():
        wg_sc[...] = stg_g[...].astype(BF16)
        wl_sc[...] = stg_l[...].astype(BF16)

    _weight_group_prefetch(s, se_ref, sj_ref, sfirst_ref, ne_ref, nj_ref, hn_ref, copies, cast)

    @pl.when(svalid_ref[s] == 1)
    def _():
        x_lo, x_hi = _unpack_bf16_pairs(x_ref[...])
        x_lo = x_lo.astype(BF16)
        x_hi = x_hi.astype(BF16)
        half = x_lo.shape[-1]

        def proj(w_sc, b_ref):
            return (jnp.dot(x_lo, w_sc[0:half, :], preferred_element_type=F32)
                    + jnp.dot(x_hi, w_sc[half:, :], preferred_element_type=F32) + b_ref[0])

        a_glu = jnp.minimum(proj(wg_sc, bg_ref), SWIGLU_LIMIT)
        a_lin = jnp.clip(proj(wl_sc, bl_ref), -SWIGLU_LIMIT, SWIGLU_LIMIT)
        act = a_glu * jax.nn.sigmoid(SWIGLU_ALPHA * a_glu) * (a_lin + 1.0)
        o_ref[...] = act.astype(o_ref.dtype)

    @pl.when(svalid_ref[s] == 0)
    def _():
        o_ref[...] = jnp.zeros(o_ref.shape, o_ref.dtype)


def _ffn2_kernel(sb_ref, sj_ref, se_ref, sfirst_ref, svalid_ref, ne_ref, nj_ref, hn_ref,
                 h_ref, w_hbm, b_ref, o_ref, stg, w_sc, sem):
    s = pl.program_id(0)
    tn = w_sc.shape[1]

    def copies(e, j):
        col = pl.multiple_of(j * tn, tn)
        return (pltpu.make_async_copy(w_hbm.at[e, :, pl.ds(col, tn)], stg, sem.at[0]),)

    def cast():
        w_sc[...] = stg[...].astype(BF16)

    _weight_group_prefetch(s, se_ref, sj_ref, sfirst_ref, ne_ref, nj_ref, hn_ref, copies, cast)

    @pl.when(svalid_ref[s] == 1)
    def _():
        y = jnp.dot(h_ref[...], w_sc[...], preferred_element_type=F32) + b_ref[0]
        half = y.shape[-1] // 2
        o_ref[...] = _pack_bf16_pairs(y[:, :half], y[:, half:])

    @pl.when(svalid_ref[s] == 0)
    def _():
        o_ref[...] = jnp.zeros(o_ref.shape, o_ref.dtype)


def _step_tables(nb, n_tiles, n_blocks):
    n_steps = n_blocks * n_tiles
    per_e = nb * n_tiles
    cum_end = jnp.cumsum(per_e)
    total = cum_end[-1]
    used_blocks = jnp.sum(nb)
    blk_start = jnp.cumsum(nb) - nb
    s_raw = jnp.arange(n_steps, dtype=jnp.int32)
    valid = s_raw < total
    s = jnp.minimum(s_raw, total - 1)
    e = jnp.minimum(jnp.sum((s[:, None] >= cum_end[None, :]).astype(jnp.int32), axis=1), N_EXPERTS - 1)
    r = s - (cum_end[e] - per_e[e])
    nbe = jnp.maximum(nb[e], 1)
    j = r // nbe
    bi = r % nbe
    first = jnp.logical_and(bi == 0, valid)
    spare = jnp.maximum(n_blocks - used_blocks, 1)
    r_pad = jnp.maximum(s_raw - total, 0)
    blk = jnp.where(valid, blk_start[e] + bi, used_blocks + r_pad % spare)
    j = jnp.where(valid, j, r_pad // spare)
    nxt = s_raw + nb[e]
    has_next = jnp.logical_and(first, nxt < total)
    nxt = jnp.minimum(nxt, n_steps - 1)
    i32 = lambda a: a.astype(jnp.int32)
    return i32(blk), i32(j), i32(e), i32(first), i32(valid), i32(e[nxt]), i32(j[nxt]), i32(has_next)


def _expert_ffn(xs, nb, w1, b1, w2, b2):
    p = xs.shape[0]
    d = 2 * xs.shape[1]
    ff = w2.shape[1]
    n_blocks = p // MOE_BLK
    nj1 = ff // FFN1_TN
    nj2 = d // FFN2_TN
    b1r = b1.reshape(N_EXPERTS, 1, 2 * ff)
    b2r = b2.reshape(N_EXPERTS, 1, d)

    t1 = _step_tables(nb, nj1, n_blocks)
    h = pl.pallas_call(
        _ffn1_kernel,
        grid_spec=pltpu.PrefetchScalarGridSpec(
            num_scalar_prefetch=len(t1),
            grid=(n_blocks * nj1,),
            in_specs=[
                pl.BlockSpec((MOE_BLK, d // 2), lambda s, sb, sj, se, *_: (sb[s], 0)),
                pl.BlockSpec(memory_space=pl.ANY),
                pl.BlockSpec((1, 1, FFN1_TN), lambda s, sb, sj, se, *_: (se[s], 0, sj[s])),
                pl.BlockSpec((1, 1, FFN1_TN), lambda s, sb, sj, se, *_: (se[s], 0, nj1 + sj[s])),
            ],
            out_specs=pl.BlockSpec((MOE_BLK, FFN1_TN), lambda s, sb, sj, se, *_: (sb[s], sj[s])),
            scratch_shapes=[pltpu.VMEM((d, FFN1_TN), F32), pltpu.VMEM((d, FFN1_TN), F32),
                            pltpu.VMEM((d, FFN1_TN), BF16), pltpu.VMEM((d, FFN1_TN), BF16),
                            pltpu.SemaphoreType.DMA((2,))],
        ),
        out_shape=jax.ShapeDtypeStruct((p, ff), BF16),
        compiler_params=_cparams(("arbitrary",)),
        name="expert_ffn_up",
    )(*t1, xs, w1, b1r, b1r)

    t2 = _step_tables(nb, nj2, n_blocks)
    ys = pl.pallas_call(
        _ffn2_kernel,
        grid_spec=pltpu.PrefetchScalarGridSpec(
            num_scalar_prefetch=len(t2),
            grid=(n_blocks * nj2,),
            in_specs=[
                pl.BlockSpec((MOE_BLK, ff), lambda s, sb, sj, se, *_: (sb[s], 0)),
                pl.BlockSpec(memory_space=pl.ANY),
                pl.BlockSpec((1, 1, FFN2_TN), lambda s, sb, sj, se, *_: (se[s], 0, sj[s])),
            ],
            out_specs=pl.BlockSpec((MOE_BLK, FFN2_TN // 2), lambda s, sb, sj, se, *_: (sb[s], sj[s])),
            scratch_shapes=[pltpu.VMEM((ff, FFN2_TN), F32), pltpu.VMEM((ff, FFN2_TN), BF16),
                            pltpu.SemaphoreType.DMA((1,))],
        ),
        out_shape=jax.ShapeDtypeStruct((p, d // 2), jnp.uint32),
        compiler_params=_cparams(("arbitrary",)),
        name="expert_ffn_down",
    )(*t2, h, w2, b2r)
    return ys


def _dispatch_kernel(dest_ref, h_ref, xs_in_ref, xs_ref, sem):
    del xs_in_ref
    tm = h_ref.shape[0]
    base = pl.program_id(0) * (tm * TOP_K)

    def row_copy(r, kk):
        dst = dest_ref[base + r * TOP_K + kk]
        return pltpu.make_async_copy(h_ref.at[pl.ds(r, 1), :], xs_ref.at[pl.ds(dst, 1), :], sem)

    def issue(r, carry):
        for kk in range(TOP_K):
            row_copy(r, kk).start(priority=kk % 2)
        return carry

    lax.fori_loop(0, tm, issue, 0, unroll=DMA_UNROLL)

    def drain(r, carry):
        for kk in range(TOP_K):
            row_copy(r, kk).wait()
        return carry

    lax.fori_loop(0, tm, drain, 0, unroll=DMA_UNROLL)


def _dispatch(h2p, dest_flat, p, tm):
    t, w = h2p.shape
    xs0 = jnp.zeros((p, w), h2p.dtype)
    return pl.pallas_call(
        _dispatch_kernel,
        grid_spec=pltpu.PrefetchScalarGridSpec(
            num_scalar_prefetch=1,
            grid=(t // tm,),
            in_specs=[pl.BlockSpec((tm, w), lambda i, dest: (i, 0)),
                      pl.BlockSpec(memory_space=pl.ANY)],
            out_specs=pl.BlockSpec(memory_space=pl.ANY),
            scratch_shapes=[pltpu.SemaphoreType.DMA(())],
        ),
        out_shape=jax.ShapeDtypeStruct((p, w), h2p.dtype),
        input_output_aliases={2: 0},
        compiler_params=_cparams(("arbitrary",)),
        name="moe_dispatch",
    )(dest_flat, h2p, xs0)


def _combine_kernel(dest_ref, x1_ref, gate_ref, g_ref, ys_ref, o_ref, buf, sem):
    i = pl.program_id(0)
    n = pl.num_programs(0)
    tm = x1_ref.shape[0]
    tn2 = FFN2_TN // 2

    def row_copy(blk, slot, r, kk):
        src = dest_ref[(blk * tm + r) * TOP_K + kk]
        return pltpu.make_async_copy(ys_ref.at[pl.ds(src, 1), :], buf.at[slot, pl.ds(kk * tm + r, 1), :],
                                     sem.at[slot])

    def issue_block(blk, slot):
        def body(r, carry):
            for kk in range(TOP_K):
                row_copy(blk, slot, r, kk).start(priority=kk % 2)
            return carry
        lax.fori_loop(0, tm, body, 0, unroll=DMA_UNROLL)

    @pl.when(i == 0)
    def _():
        issue_block(0, 0)

    @pl.when(i + 1 < n)
    def _():
        issue_block(i + 1, (i + 1) % 2)

    slot = i % 2

    def drain(r, carry):
        for kk in range(TOP_K):
            row_copy(i, slot, r, kk).wait()
        return carry

    lax.fori_loop(0, tm, drain, 0, unroll=DMA_UNROLL)

    gates = gate_ref[...]
    pieces = [None] * (2 * (2 * buf.shape[-1] // FFN2_TN))
    for kk in range(TOP_K):
        lo, hi = _unpack_bf16_pairs(buf[slot, kk * tm:(kk + 1) * tm, :])
        g = gates[:, kk:kk + 1]
        for j in range(len(pieces) // 2):
            for half, part in enumerate((lo, hi)):
                term = g * part[:, j * tn2:(j + 1) * tn2]
                idx = 2 * j + half
                pieces[idx] = term if pieces[idx] is None else pieces[idx] + term
    x = x1_ref[...] + jnp.concatenate(pieces, axis=-1)
    ms = jnp.mean(x * x, axis=-1, keepdims=True)
    o_ref[...] = (x * lax.rsqrt(ms + NORM_EPS) * g_ref[...]).astype(o_ref.dtype)


def _combine(x1, gates, dest_flat, ys, g_final, tm):
    t, d = x1.shape
    w = ys.shape[1]
    return pl.pallas_call(
        _combine_kernel,
        grid_spec=pltpu.PrefetchScalarGridSpec(
            num_scalar_prefetch=1,
            grid=(t // tm,),
            in_specs=[pl.BlockSpec((tm, d), lambda i, dest: (i, 0)),
                      pl.BlockSpec((tm, gates.shape[1]), lambda i, dest: (i, 0)),
                      pl.BlockSpec((1, d), lambda i, dest: (0, 0)),
                      pl.BlockSpec(memory_space=pl.ANY)],
            out_specs=pl.BlockSpec((tm, d), lambda i, dest: (i, 0)),
            scratch_shapes=[pltpu.VMEM((2, TOP_K * tm, w), ys.dtype), pltpu.SemaphoreType.DMA((2,))],
        ),
        out_shape=jax.ShapeDtypeStruct((t, d), F32),
        compiler_params=_cparams(("arbitrary",)),
        name="moe_combine_final_norm",
    )(dest_flat, x1, gates, g_final.reshape(1, d).astype(F32), ys)


def kernel(x, mem, g_mix, w_in, b_gate, rel_bias, lambda_q1, lambda_k1, lambda_q2, lambda_k2, diff_subln_g, g_mem, w_mem_kv, w_br_diff, w_br_sb, w_br_mem, w_out, g_ffn, w_router, b_router, w_exp1, b_exp1, w_exp2, b_exp2, g_final):
    b, s, d = x.shape
    t = b * s
    depth = g_mix.shape[0]
    dqk_w = DIFF_HEADS * 2 * DIFF_QK_DIM
    dv_w = DIFF_HEADS * DIFF_V_DIM
    sb_w = SB_HEADS * SB_DIM
    mem_w = MEM_HEADS * MEM_DIM
    qkv_w = 2 * dqk_w + dv_w + 3 * sb_w + mem_w
    mlen = mem.shape[1]
    tm = min(ROW_TILE, s)

    xf = x.reshape(t, d)
    for l in range(depth):
        w_in_b = w_in[l].astype(BF16)
        hmix = _rmsnorm(xf, g_mix[l], BF16, tm)
        c0 = 0
        dq = _matmul_heads(hmix, w_in_b, c0, b, s, DIFF_HEADS, 2 * DIFF_QK_DIM, _HEADS_T, name="in_proj_dq")
        c0 += dqk_w
        dk = _matmul_heads(hmix, w_in_b, c0, b, s, DIFF_HEADS, 2 * DIFF_QK_DIM, _HEADS, name="in_proj_dk")
        c0 += dqk_w
        dvv = _matmul_heads(hmix, w_in_b, c0, b, s, DIFF_HEADS, DIFF_V_DIM, _HEADS_T_TILED, tk=DIFF_TILE,
                            name="in_proj_dv")
        c0 += dv_w
        sq = _matmul_heads(hmix, w_in_b, c0, b, s, SB_HEADS, SB_DIM, _HEADS_T, name="in_proj_sq")
        c0 += sb_w
        sk = _matmul_heads(hmix, w_in_b, c0, b, s, SB_HEADS, SB_DIM, _HEADS, name="in_proj_sk")
        c0 += sb_w
        sv = _matmul_heads(hmix, w_in_b, c0, b, s, SB_HEADS, SB_DIM, _HEADS_T_TILED, tk=SB_GROUP_KEYS,
                           name="in_proj_sv")
        c0 += sb_w
        mq = _matmul(hmix, w_in_b, c0, mem_w, name="in_proj_mq")
        gates = _matmul(hmix, w_in_b, qkv_w, 3 * d, bias=b_gate[l], name="in_proj_gates")

        lam_init = 0.8 - 0.6 * math.exp(-0.3 * l)
        lam = (jnp.exp(jnp.sum(lambda_q1[l].astype(F32) * lambda_k1[l].astype(F32)))
               - jnp.exp(jnp.sum(lambda_q2[l].astype(F32) * lambda_k2[l].astype(F32))) + lam_init)
        ya = _diff_attention(dq, dk, dvv, rel_bias, lam, diff_subln_g[l], lam_init).reshape(t, dv_w)
        yb = _sb_attention(sq, sk, sv).reshape(t, sb_w)

        hmem = _rmsnorm(mem.reshape(b * mlen, d), g_mem[l], BF16, min(ROW_TILE, b * mlen))
        kv = _matmul(hmem, w_mem_kv[l].astype(BF16), 0, 2 * mem_w, name="mem_kv_proj")
        yc = _mem_attention(mq.reshape(b, s, mem_w), kv.reshape(b, mlen, 2 * mem_w), tm).reshape(t, mem_w)

        merged = _merge(ya, yb, yc, w_br_diff[l].astype(BF16), w_br_sb[l].astype(BF16),
                        w_br_mem[l].astype(BF16), gates, d, tm, MM_TN)

        wr = jnp.pad(w_router[l].astype(F32), ((0, 0), (0, ROUTER_PAD - N_EXPERTS)))
        wr_hi = wr.astype(BF16)
        wr_lo = (wr - wr_hi.astype(F32)).astype(BF16)
        br_pad = jnp.pad(b_router[l].astype(F32), (0, ROUTER_PAD - N_EXPERTS),
                         constant_values=-jnp.inf).reshape(1, ROUTER_PAD)
        x1, h2, logits = _outproj(merged, w_out[l].astype(BF16), xf, g_ffn[l].reshape(1, d).astype(F32),
                                  wr_hi, wr_lo, br_pad, tm)
        top_i, top_g, rank, counts = _topk(logits, tm)

        n = t * TOP_K
        counts = counts[0, :N_EXPERTS]
        nb = (counts + MOE_BLK - 1) // MOE_BLK
        pad_start = (jnp.cumsum(nb) - nb) * MOE_BLK
        dest = (pad_start[top_i[:, :TOP_K]] + rank[:, :TOP_K]).reshape(n)
        n_blocks = n // MOE_BLK + N_EXPERTS
        p = n_blocks * MOE_BLK

        xs = _dispatch(h2, dest, p, min(DMA_ROWS, s))
        ys = _expert_ffn(xs, nb, w_exp1[l], b_exp1[l], w_exp2[l], b_exp2[l])
        assert l + 1 == depth, "combine is fused with the final norm: single-layer trunk"
        return _combine(x1, top_g, dest, ys, g_final, min(DMA_ROWS, s)).reshape(b, s, d)
```

```python
import functools
import math

import jax
import jax.numpy as jnp
from jax import lax
from jax.experimental import pallas as pl
from jax.experimental.pallas import tpu as pltpu

F32 = jnp.float32
BF16 = jnp.bfloat16

DIFF_HEADS = 8
DIFF_QK_DIM = 64
DIFF_V_DIM = 128
SB_HEADS = 8
SB_DIM = 128
MEM_HEADS = 4
MEM_DIM = 256
N_BUCKETS = 32
MAX_DISTANCE = 128
N_EXPERTS = 32
TOP_K = 4
SWIGLU_LIMIT = 7.0
SWIGLU_ALPHA = 1.702
NORM_EPS = 1e-6
NEG_INF = -1e30
LOG2E = math.log2(math.e)

LANES = 128
VMEM_LIMIT_BYTES = 56 * 1024 * 1024

ROW_TILE = 512
MM_TM = 1024
MM_TN = 1024
DIFF_TILE = 512
ATT_CHUNK = 64
SB_TK = 128
SB_GROUP_KEYS = 512
MOE_BLK = 512
FFN1_TN = 1024
FFN2_TN = 2048
ROUTER_PAD = LANES
DMA_ROWS = 512
DMA_UNROLL = 8


def _cparams(sem):
    return pltpu.CompilerParams(dimension_semantics=sem, vmem_limit_bytes=VMEM_LIMIT_BYTES)


def _rmsnorm_kernel(x_ref, g_ref, o_ref):
    x = x_ref[...].astype(F32)
    ms = jnp.mean(x * x, axis=-1, keepdims=True)
    o_ref[...] = (x * lax.rsqrt(ms + NORM_EPS) * g_ref[...]).astype(o_ref.dtype)


def _rmsnorm(x, g, out_dtype, tm):
    m, d = x.shape
    return pl.pallas_call(
        _rmsnorm_kernel,
        grid=(m // tm,),
        in_specs=[pl.BlockSpec((tm, d), lambda i: (i, 0)),
                  pl.BlockSpec((1, d), lambda i: (0, 0))],
        out_specs=pl.BlockSpec((tm, d), lambda i: (i, 0)),
        out_shape=jax.ShapeDtypeStruct((m, d), out_dtype),
        compiler_params=_cparams(("arbitrary",)),
        name="rmsnorm",
    )(x, g.reshape(1, d).astype(F32))


def _mm_kernel(a_ref, w_ref, o_ref):
    o_ref[...] = jnp.dot(a_ref[...], w_ref[...], preferred_element_type=F32).astype(o_ref.dtype)


def _mm_sigmoid_kernel(a_ref, w_ref, b_ref, o_ref):
    acc = jnp.dot(a_ref[...], w_ref[...], preferred_element_type=F32)
    o_ref[...] = jax.nn.sigmoid(acc + b_ref[...]).astype(o_ref.dtype)


_HEADS, _HEADS_T, _HEADS_T_TILED = "heads", "heads_T", "heads_T_tiled"


def _mm_heads_kernel(a_ref, w_ref, o_ref, *, layout, hd):
    acc = jnp.dot(a_ref[...], w_ref[...], preferred_element_type=F32)
    nh = acc.shape[1] // hd
    if layout == _HEADS:
        for hh in range(nh):
            o_ref[0, hh] = acc[:, hh * hd:(hh + 1) * hd].astype(o_ref.dtype)
        return
    for hh in range(nh):
        acc_t = acc[:, hh * hd:(hh + 1) * hd].T
        if layout == _HEADS_T:
            o_ref[0, hh] = acc_t.astype(o_ref.dtype)
        else:
            tk = o_ref.shape[-1]
            for c in range(o_ref.shape[2]):
                o_ref[0, hh, c] = acc_t[:, c * tk:(c + 1) * tk].astype(o_ref.dtype)


def _matmul_heads(a, w, col_off, b, s, nh, hd, layout, tk=None, tm=MM_TM, name="proj_heads"):
    m, k = a.shape
    tm = min(tm, s)
    n = nh * hd
    off = col_off // n
    per_b = s // tm
    if layout == _HEADS:
        shape, blk = (b, nh, s, hd), (1, nh, tm, hd)
        omap = lambda i: (i // per_b, 0, i % per_b, 0)
    elif layout == _HEADS_T:
        shape, blk = (b, nh, hd, s), (1, nh, hd, tm)
        omap = lambda i: (i // per_b, 0, 0, i % per_b)
    else:
        shape, blk = (b, nh, s // tk, hd, tk), (1, nh, tm // tk, hd, tk)
        omap = lambda i: (i // per_b, 0, i % per_b, 0, 0)
    return pl.pallas_call(
        functools.partial(_mm_heads_kernel, layout=layout, hd=hd),
        grid=(m // tm,),
        in_specs=[pl.BlockSpec((tm, k), lambda i: (i, 0)),
                  pl.BlockSpec((k, n), lambda i: (0, off))],
        out_specs=pl.BlockSpec(blk, omap),
        out_shape=jax.ShapeDtypeStruct(shape, BF16),
        compiler_params=_cparams(("arbitrary",)),
        name=name,
    )(a, w)


def _matmul(a, w, col_off, n_out, bias=None, tm=MM_TM, tn=MM_TN, name="matmul"):
    m, k = a.shape
    tm = min(tm, m)
    off = col_off // tn
    in_specs = [pl.BlockSpec((tm, k), lambda j, i: (i, 0)),
                pl.BlockSpec((k, tn), lambda j, i: (0, j + off))]
    args = [a, w]
    body = _mm_kernel
    if bias is not None:
        in_specs.append(pl.BlockSpec((1, tn), lambda j, i: (0, j)))
        args.append(bias.reshape(1, n_out).astype(F32))
        body = _mm_sigmoid_kernel
    return pl.pallas_call(
        body,
        grid=(n_out // tn, m // tm),
        in_specs=in_specs,
        out_specs=pl.BlockSpec((tm, tn), lambda j, i: (i, j)),
        out_shape=jax.ShapeDtypeStruct((m, n_out), BF16),
        compiler_params=_cparams(("arbitrary", "arbitrary")),
        name=name,
    )(*args)


def _t5_bias_tile(rel_ref, h, offset, tk, tq):
    r = lax.broadcasted_iota(jnp.int32, (tk, tq), 0)
    c = lax.broadcasted_iota(jnp.int32, (tk, tq), 1)
    n = jnp.maximum(c - r + offset, 0)
    max_exact = N_BUCKETS // 2
    nf = jnp.maximum(n, 1).astype(F32)
    large = max_exact + (jnp.log(nf / max_exact) / math.log(MAX_DISTANCE / max_exact)
                         * (N_BUCKETS - max_exact)).astype(jnp.int32)
    large = jnp.minimum(large, N_BUCKETS - 1)
    bucket = jnp.where(n < max_exact, n, large)
    far = rel_ref[(N_BUCKETS - 1) * DIFF_HEADS + h]
    out = jnp.zeros((tk, tq), F32)
    for j in range(N_BUCKETS - 1):
        out = jnp.where(bucket == j, (rel_ref[j * DIFF_HEADS + h] - far) * LOG2E, out)
    return out


def _tree(xs, op):
    while len(xs) > 1:
        nxt = [op(xs[i], xs[i + 1]) for i in range(0, len(xs) - 1, 2)]
        if len(xs) % 2:
            nxt.append(xs[-1])
        xs = nxt
    return xs[0]


def _rows_to_8(x, op):
    return _tree([x[i:i + 8] for i in range(0, x.shape[0], 8)], op)


def _loop_by_two(n, body):
    def two(i, carry):
        body(2 * i, carry)
        body(2 * i + 1, carry)
        return carry

    lax.fori_loop(0, n // 2, two, 0)

    @pl.when(n % 2 == 1)
    def _():
        body(n - 1, 0)


_FAR, _NEAR, _DIAG = 0, 1, 2


def _diff_attn_kernel(rel_ref, lam_ref, qT_ref, k_ref, vT_ref, g_ref, o_ref,
                      bias_sc, s_a, s_b, mt_a, mt_b, pb_sc, m_sc, l_sc, acc_sc, *, lam_init):
    h = pl.program_id(1)
    qi = pl.program_id(2)
    tq = qT_ref.shape[-1]
    tk = vT_ref.shape[-1]
    ch = ATT_CHUNK
    nch = tk // ch

    @pl.when(qi == 0)
    def _():
        for c in range(nch):
            bias_sc[0, c * ch:(c + 1) * ch, :] = _t5_bias_tile(rel_ref, h, -c * ch, ch, tq)
            bias_sc[1, c * ch:(c + 1) * ch, :] = _t5_bias_tile(rel_ref, h, tk - c * ch, ch, tq)

    qs = (qT_ref[0, 0].astype(F32) * (DIFF_QK_DIM ** -0.5 * LOG2E)).astype(BF16)
    row = lax.broadcasted_iota(jnp.int32, qs.shape, 0)
    zero = jnp.zeros_like(qs)
    qmaps = (jnp.where(row < DIFF_QK_DIM, qs, zero), jnp.where(row >= DIFF_QK_DIM, qs, zero))

    m_sc[...] = jnp.full(m_sc.shape, NEG_INF, F32)
    l_sc[...] = jnp.zeros(l_sc.shape, F32)
    acc_sc[...] = jnp.zeros(acc_sc.shape, F32)

    def scores_max(j, buf, kind):
        dst, mt = buf
        base = pl.multiple_of(j * tk, tk)
        kt = k_ref[0, 0, pl.ds(base, tk), :]
        for mi in range(2):
            dst[mi] = jnp.dot(kt, qmaps[mi], preferred_element_type=F32)
        for mi in range(2):
            parts = []
            for c in range(nch):
                rows = slice(c * ch, (c + 1) * ch)
                x = dst[mi, rows, :]
                if kind != _FAR:
                    x = x + bias_sc[0 if kind == _DIAG else 1, rows, :]
                    if kind == _DIAG:
                        r = lax.broadcasted_iota(jnp.int32, (ch, tq), 0) + c * ch
                        cc = lax.broadcasted_iota(jnp.int32, (ch, tq), 1)
                        x = jnp.where(r <= cc, x, NEG_INF)
                    dst[mi, rows, :] = x
                parts.append(_rows_to_8(x, jnp.maximum))
            mt[mi] = jnp.max(_tree(parts, jnp.maximum), axis=0, keepdims=True)

    def softmax_pv(buf, j):
        src, mt = buf
        vt = vT_ref[0, 0, j]
        for mi in range(2):
            m_old = m_sc[mi]
            m_new = jnp.maximum(m_old, mt[mi])
            alpha = jnp.exp2(m_old - m_new)
            parts = []
            for c in range(nch):
                rows = slice(c * ch, (c + 1) * ch)
                p = jnp.exp2(src[mi, rows, :] - m_new)
                parts.append(_rows_to_8(p, jnp.add))
                pb_sc[mi, rows, :] = p.astype(BF16)
            l_sc[mi] = alpha * l_sc[mi] + jnp.sum(_tree(parts, jnp.add), axis=0, keepdims=True)
            pv = jnp.dot(vt, pb_sc[mi], preferred_element_type=F32)
            acc_sc[mi] = alpha * acc_sc[mi] + pv
            m_sc[mi] = m_new

    buf_a, buf_b = (s_a, mt_a), (s_b, mt_b)
    n_far = jnp.maximum(qi - 1, 0)

    @pl.when(qi == 0)
    def _():
        scores_max(0, buf_a, _DIAG)
        softmax_pv(buf_a, 0)

    @pl.when(qi == 1)
    def _():
        scores_max(0, buf_a, _NEAR)
        scores_max(1, buf_b, _DIAG)
        softmax_pv(buf_a, 0)
        softmax_pv(buf_b, 1)

    @pl.when(qi >= 2)
    def _():
        scores_max(0, buf_a, _FAR)
        n_pairs = (n_far - 1) // 2

        def far_pair(i, carry):
            scores_max(2 * i + 1, buf_b, _FAR)
            softmax_pv(buf_a, 2 * i)
            scores_max(2 * i + 2, buf_a, _FAR)
            softmax_pv(buf_b, 2 * i + 1)
            return carry

        _loop_by_two(n_pairs, far_pair)

        @pl.when(n_far % 2 == 1)
        def _():
            scores_max(qi - 1, buf_b, _NEAR)
            softmax_pv(buf_a, qi - 2)
            scores_max(qi, buf_a, _DIAG)
            softmax_pv(buf_b, qi - 1)
            softmax_pv(buf_a, qi)

        @pl.when(n_far % 2 == 0)
        def _():
            scores_max(qi - 2, buf_b, _FAR)
            softmax_pv(buf_a, qi - 3)
            scores_max(qi - 1, buf_a, _NEAR)
            softmax_pv(buf_b, qi - 2)
            scores_max(qi, buf_b, _DIAG)
            softmax_pv(buf_a, qi - 1)
            softmax_pv(buf_b, qi)

    lam = lam_ref[0]
    o = acc_sc[0] / l_sc[0] - lam * (acc_sc[1] / l_sc[1])
    ms = jnp.mean(o * o, axis=0, keepdims=True)
    y = o * lax.rsqrt(ms + NORM_EPS) * g_ref[...]
    o_ref[0] = (y * (1.0 - lam_init)).T.astype(o_ref.dtype)


def _diff_attention(qT, k, vT5, rel_bias, lam, subln_g, lam_init):
    b, h, dqk, s = qT.shape
    tq = tk = DIFF_TILE
    dv = vT5.shape[3]
    grid_spec = pltpu.PrefetchScalarGridSpec(
        num_scalar_prefetch=2,
        grid=(b, h, s // tq),
        in_specs=[
            pl.BlockSpec((1, 1, dqk, tq), lambda bi, hi, qi, *_: (bi, hi, 0, qi)),
            pl.BlockSpec((1, 1, s, dqk), lambda bi, hi, qi, *_: (bi, hi, 0, 0)),
            pl.BlockSpec((1, 1, s // tk, dv, tk), lambda bi, hi, qi, *_: (bi, hi, 0, 0, 0)),
            pl.BlockSpec((dv, 1), lambda bi, hi, qi, *_: (0, 0)),
        ],
        out_specs=pl.BlockSpec((1, tq, dv), lambda bi, hi, qi, *_: (bi, qi, hi)),
        scratch_shapes=[
            pltpu.VMEM((2, tk, tq), F32),
            pltpu.VMEM((2, tk, tq), F32),
            pltpu.VMEM((2, tk, tq), F32),
            pltpu.VMEM((2, 1, tq), F32),
            pltpu.VMEM((2, 1, tq), F32),
            pltpu.VMEM((2, tk, tq), BF16),
            pltpu.VMEM((2, 1, tq), F32),
            pltpu.VMEM((2, 1, tq), F32),
            pltpu.VMEM((2, dv, tq), F32),
        ],
    )
    return pl.pallas_call(
        functools.partial(_diff_attn_kernel, lam_init=lam_init),
        grid_spec=grid_spec,
        out_shape=jax.ShapeDtypeStruct((b, s, h * dv), BF16),
        compiler_params=_cparams(("arbitrary", "arbitrary", "arbitrary")),
        name="diff_attention",
    )(rel_bias.reshape(-1).astype(F32), lam.reshape(1).astype(F32), qT, k, vT5, subln_g.reshape(dv, 1).astype(F32))


def _sb_attn_kernel(qT_ref, k_ref, vT_ref, o_ref, z_a, z_b, w_a, w_b, hl_a, hl_b, af_a, af_b, a_a, a_b,
                    acc_sc, carry_sc):
    qi = pl.program_id(2)
    tq = qT_ref.shape[-1]
    gk = vT_ref.shape[-1]
    tk = SB_TK
    nt = gk // tk
    ch = ATT_CHUNK
    d = qT_ref.shape[2]
    qs = (qT_ref[0, 0].astype(F32) * (d ** -0.5 * LOG2E)).astype(BF16)

    sr = lax.broadcasted_iota(jnp.int32, (tk, tk), 0)
    sc = lax.broadcasted_iota(jnp.int32, (tk, tk), 1)
    suffix = jnp.where(sc > sr, 1.0, 0.0).astype(BF16)

    acc_sc[...] = jnp.zeros(acc_sc.shape, F32)
    carry_sc[...] = jnp.zeros(carry_sc.shape, F32)

    def qk(g, dst):
        base = pl.multiple_of(g * gk, gk)
        dst[...] = jnp.dot(k_ref[0, 0, pl.ds(base, gk), :], qs, preferred_element_type=F32)

    def first_col(row0, diag):
        return (row0 // LANES) * LANES if diag else 0

    def strict_mask(row0, lo):
        r = lax.broadcasted_iota(jnp.int32, (ch, tq - lo), 0) + row0
        c = lax.broadcasted_iota(jnp.int32, (ch, tq - lo), 1) + lo
        return r < c

    def first_pass(st, diag):
        z_sc, w_sc, hl_sc, af_sc, _ = st
        lk0 = []
        for t in range(nt):
            for c in range(tk // ch):
                row0 = t * tk + c * ch
                lo = first_col(row0, diag)
                z = z_sc[row0:row0 + ch, lo:]
                neg_abs = lax.bitcast_convert_type(
                    lax.bitcast_convert_type(z, jnp.uint32) | jnp.uint32(0x80000000), F32)
                lg = jnp.log(1.0 + jnp.exp2(neg_abs)) * LOG2E
                w = jnp.minimum(z, 0.0) - lg
                lk = w - z
                if diag:
                    lk = jnp.where(strict_mask(row0, lo), lk, 0.0)
                hl_sc[t, c * ch:(c + 1) * ch, lo:] = lk.astype(BF16)
                w_sc[row0:row0 + ch, lo:] = w
                lk_row = lk[0:1, :]
                if lo:
                    hl_sc[t, c * ch:(c + 1) * ch, :lo] = jnp.zeros((ch, lo), BF16)
                    lk_row = jnp.concatenate([jnp.zeros((1, lo), F32), lk_row], axis=1)
                if c == 0:
                    lk0.append(lk_row)
            af_sc[t] = jnp.dot(suffix, hl_sc[t], preferred_element_type=F32)
        return lk0

    def second_pass(st, lk0, g, diag):
        _, w_sc, _, af_sc, a_sc = st
        carry = carry_sc[...]
        for t in reversed(range(nt)):
            for c in range(tk // ch):
                row0 = t * tk + c * ch
                lo = first_col(row0, diag)
                a = jnp.exp2(w_sc[row0:row0 + ch, lo:] + af_sc[t, c * ch:(c + 1) * ch, lo:] + carry[:, lo:])
                if diag:
                    a = jnp.where(strict_mask(row0, lo), a, 0.0)
                a_sc[row0:row0 + ch, lo:] = a.astype(BF16)
                if lo:
                    a_sc[row0:row0 + ch, :lo] = jnp.zeros((ch, lo), BF16)
            carry = carry + af_sc[t, 0:1, :] + lk0[t]
        carry_sc[...] = carry
        acc_sc[...] += jnp.dot(vT_ref[0, 0, g], a_sc[...], preferred_element_type=F32)

    st_a = (z_a, w_a, hl_a, af_a, a_a)
    st_b = (z_b, w_b, hl_b, af_b, a_b)
    qk(qi, z_a)

    @pl.when(qi == 0)
    def _():
        second_pass(st_a, first_pass(st_a, True), qi, True)

    @pl.when(qi >= 1)
    def _():
        qk(qi - 1, z_b)
        second_pass(st_a, first_pass(st_a, True), qi, True)
        n_pairs = (qi - 1) // 2

        def pair(i, carry):
            g = qi - 1 - 2 * i
            qk(g - 1, z_a)
            lk_b = first_pass(st_b, False)
            qk(g - 2, z_b)
            lk_a = first_pass(st_a, False)
            second_pass(st_b, lk_b, g, False)
            second_pass(st_a, lk_a, g - 1, False)
            return carry

        _loop_by_two(n_pairs, pair)

        @pl.when(qi % 2 == 0)
        def _():
            qk(0, z_a)
            lk_b = first_pass(st_b, False)
            lk_a = first_pass(st_a, False)
            second_pass(st_b, lk_b, 1, False)
            second_pass(st_a, lk_a, 0, False)

        @pl.when(qi % 2 == 1)
        def _():
            second_pass(st_b, first_pass(st_b, False), 0, False)

    o_ref[0] = acc_sc[...].T.astype(o_ref.dtype)


def _sb_attention(qT, k, vT5):
    b, h, d, s = qT.shape
    tq = gk = SB_GROUP_KEYS
    nt = gk // SB_TK
    two = lambda shape, dt: [pltpu.VMEM(shape, dt), pltpu.VMEM(shape, dt)]
    return pl.pallas_call(
        _sb_attn_kernel,
        grid=(b, h, s // tq),
        in_specs=[
            pl.BlockSpec((1, 1, d, tq), lambda bi, hi, qi: (bi, hi, 0, qi)),
            pl.BlockSpec((1, 1, s, d), lambda bi, hi, qi: (bi, hi, 0, 0)),
            pl.BlockSpec((1, 1, s // gk, d, gk), lambda bi, hi, qi: (bi, hi, 0, 0, 0)),
        ],
        out_specs=pl.BlockSpec((1, tq, d), lambda bi, hi, qi: (bi, qi, hi)),
        out_shape=jax.ShapeDtypeStruct((b, s, h * d), BF16),
        scratch_shapes=(two((gk, tq), F32) + two((gk, tq), F32) + two((nt, SB_TK, tq), BF16)
                        + two((nt, SB_TK, tq), F32) + two((gk, tq), BF16)
                        + [pltpu.VMEM((d, tq), F32), pltpu.VMEM((1, tq), F32)]),
        compiler_params=_cparams(("arbitrary", "arbitrary", "arbitrary")),
        name="sb_attention",
    )(qT, k, vT5)


def _mem_attn_kernel(q_ref, kv_ref, o_ref):
    hd = q_ref.shape[-1]
    d = MEM_DIM
    outs = []
    for hh in range(hd // d):
        q = q_ref[0, :, hh * d:(hh + 1) * d]
        kk = kv_ref[0, :, hh * d:(hh + 1) * d]
        vv = kv_ref[0, :, hd + hh * d:hd + (hh + 1) * d]
        s = lax.dot_general(q, kk, (((1,), (1,)), ((), ())), preferred_element_type=F32) * (d ** -0.5)
        m = jnp.max(s, axis=-1, keepdims=True)
        p = jnp.exp(s - m)
        p = p / jnp.sum(p, axis=-1, keepdims=True)
        outs.append(jnp.dot(p.astype(BF16), vv, preferred_element_type=F32))
    o_ref[0] = jnp.concatenate(outs, axis=-1).astype(o_ref.dtype)


def _mem_attention(q, kv, tm):
    b, s, hd = q.shape
    mlen = kv.shape[1]
    return pl.pallas_call(
        _mem_attn_kernel,
        grid=(b, s // tm),
        in_specs=[pl.BlockSpec((1, tm, hd), lambda bi, i: (bi, i, 0)),
                  pl.BlockSpec((1, mlen, 2 * hd), lambda bi, i: (bi, 0, 0))],
        out_specs=pl.BlockSpec((1, tm, hd), lambda bi, i: (bi, i, 0)),
        out_shape=jax.ShapeDtypeStruct((b, s, hd), BF16),
        compiler_params=_cparams(("arbitrary", "arbitrary")),
        name="mem_attention",
    )(q, kv)


def _merge_kernel(ya_ref, yb_ref, yc_ref, wa_ref, wb_ref, wc_ref, ga_ref, gb_ref, gc_ref, o_ref):
    pa = jnp.dot(ya_ref[...], wa_ref[...], preferred_element_type=F32)
    pb = jnp.dot(yb_ref[...], wb_ref[...], preferred_element_type=F32)
    pc = jnp.dot(yc_ref[...], wc_ref[...], preferred_element_type=F32)
    merged = (ga_ref[...].astype(F32) * pa + gb_ref[...].astype(F32) * pb + gc_ref[...].astype(F32) * pc)
    o_ref[...] = merged.astype(o_ref.dtype)


def _merge(ya, yb, yc, wa, wb, wc, gates, d_model, tm, tn):
    m = ya.shape[0]
    nblk = d_model // tn
    y_spec = lambda arr: pl.BlockSpec((tm, arr.shape[1]), lambda j, i: (i, 0))
    w_spec = lambda arr: pl.BlockSpec((arr.shape[0], tn), lambda j, i: (0, j))
    g_spec = lambda br: pl.BlockSpec((tm, tn), lambda j, i, br=br: (i, br * nblk + j))
    return pl.pallas_call(
        _merge_kernel,
        grid=(nblk, m // tm),
        in_specs=[y_spec(ya), y_spec(yb), y_spec(yc), w_spec(wa), w_spec(wb), w_spec(wc),
                  g_spec(0), g_spec(1), g_spec(2)],
        out_specs=pl.BlockSpec((tm, tn), lambda j, i: (i, j)),
        out_shape=jax.ShapeDtypeStruct((m, d_model), BF16),
        compiler_params=_cparams(("arbitrary", "arbitrary")),
        name="gated_merge",
    )(ya, yb, yc, wa, wb, wc, gates, gates, gates)


def _pack_bf16_pairs(lo, hi):
    lo_bits = lax.bitcast_convert_type(lo.astype(BF16).astype(F32), jnp.uint32)
    hi_bits = lax.bitcast_convert_type(hi.astype(BF16).astype(F32), jnp.uint32)
    return (hi_bits & jnp.uint32(0xFFFF0000)) | (lo_bits >> 16)


def _unpack_bf16_pairs(words):
    lo = lax.bitcast_convert_type(words << 16, F32)
    hi = lax.bitcast_convert_type(words & jnp.uint32(0xFFFF0000), F32)
    return lo, hi


def _outproj_kernel(mg_ref, w_ref, x_ref, g_ref, wrh_ref, wrl_ref, br_ref, x1_ref, h2_ref, lg_ref):
    x1 = x_ref[...] + jnp.dot(mg_ref[...], w_ref[...], preferred_element_type=F32)
    x1_ref[...] = x1
    ms = jnp.mean(x1 * x1, axis=-1, keepdims=True)
    h2 = x1 * lax.rsqrt(ms + NORM_EPS) * g_ref[...]
    h_hi = h2.astype(BF16)
    half = h2.shape[-1] // 2
    h2_ref[...] = _pack_bf16_pairs(h2[:, :half], h2[:, half:])
    h_lo = (h2 - h_hi.astype(F32)).astype(BF16)
    lg = (jnp.dot(h_hi, wrh_ref[...], preferred_element_type=F32)
          + jnp.dot(h_lo, wrh_ref[...], preferred_element_type=F32)
          + jnp.dot(h_hi, wrl_ref[...], preferred_element_type=F32))
    lg_ref[...] = lg + br_ref[...]


def _outproj(merged, w_out, x, g_ffn, wr_hi, wr_lo, br_pad, tm):
    m, d = x.shape
    row = lambda n: pl.BlockSpec((tm, n), lambda i: (i, 0))
    full = lambda arr: pl.BlockSpec(arr.shape, lambda i: (0, 0))
    return pl.pallas_call(
        _outproj_kernel,
        grid=(m // tm,),
        in_specs=[row(d), full(w_out), row(d), full(g_ffn), full(wr_hi), full(wr_lo), full(br_pad)],
        out_specs=[row(d), row(d // 2), row(ROUTER_PAD)],
        out_shape=[jax.ShapeDtypeStruct((m, d), F32), jax.ShapeDtypeStruct((m, d // 2), jnp.uint32),
                   jax.ShapeDtypeStruct((m, ROUTER_PAD), F32)],
        compiler_params=_cparams(("arbitrary",)),
        name="outproj_norm_router",
    )(merged, w_out, x, g_ffn, wr_hi, wr_lo, br_pad)


def _topk_kernel(lg_ref, idx_ref, gate_ref, rank_ref, cnt_ref, run_sc):
    @pl.when(pl.program_id(0) == 0)
    def _():
        run_sc[...] = jnp.zeros(run_sc.shape, F32)

    l = lg_ref[...]
    tm = l.shape[0]
    lane = lax.broadcasted_iota(jnp.int32, l.shape, 1)
    vals, ids = [], []
    for _ in range(TOP_K):
        m = jnp.max(l, axis=-1, keepdims=True)
        idx = jnp.min(jnp.where(l == m, lane, l.shape[-1]), axis=-1, keepdims=True)
        vals.append(m)
        ids.append(idx)
        l = jnp.where(lane == idx, -jnp.inf, l)
    es = [jnp.exp(v - vals[0]) for v in vals]
    den = es[0]
    for e in es[1:]:
        den = den + e

    chosen = jnp.zeros(l.shape, F32)
    for kk in range(TOP_K):
        chosen = jnp.where(lane == ids[kk], 1.0, chosen)
    tr = lax.broadcasted_iota(jnp.int32, (tm, tm), 0)
    tc = lax.broadcasted_iota(jnp.int32, (tm, tm), 1)
    earlier = jnp.where(tc < tr, 1.0, 0.0).astype(BF16)
    before = jnp.dot(earlier, chosen.astype(BF16), preferred_element_type=F32) + run_sc[...]

    idx_out = jnp.zeros(l.shape, jnp.int32)
    gate_out = jnp.zeros(l.shape, F32)
    rank_out = jnp.zeros(l.shape, F32)
    for kk in range(TOP_K):
        idx_out = jnp.where(lane == kk, ids[kk], idx_out)
        gate_out = jnp.where(lane == kk, es[kk] / den, gate_out)
        rk = jnp.sum(jnp.where(lane == ids[kk], before, 0.0), axis=-1, keepdims=True)
        rank_out = jnp.where(lane == kk, rk, rank_out)
    idx_ref[...] = idx_out
    gate_ref[...] = gate_out
    rank_ref[...] = rank_out.astype(jnp.int32)
    run = run_sc[...] + jnp.sum(chosen, axis=0, keepdims=True)
    run_sc[...] = run
    cnt_ref[...] = run.astype(jnp.int32)


def _topk(logits, tm):
    m, n = logits.shape
    spec = pl.BlockSpec((tm, n), lambda i: (i, 0))
    return pl.pallas_call(
        _topk_kernel,
        grid=(m // tm,),
        in_specs=[spec],
        out_specs=[spec, spec, spec, pl.BlockSpec((1, n), lambda i: (0, 0))],
        out_shape=[jax.ShapeDtypeStruct((m, n), jnp.int32), jax.ShapeDtypeStruct((m, n), F32),
                   jax.ShapeDtypeStruct((m, n), jnp.int32), jax.ShapeDtypeStruct((1, n), jnp.int32)],
        scratch_shapes=[pltpu.VMEM((1, n), F32)],
        compiler_params=_cparams(("arbitrary",)),
        name="router_topk",
    )(logits)


def _weight_group_prefetch(s, se_ref, sj_ref, sfirst_ref, ne_ref, nj_ref, hn_ref, copies, cast):
    @pl.when(s == 0)
    def _():
        for c in copies(se_ref[0], sj_ref[0]):
            c.start()

    @pl.when(sfirst_ref[s] == 1)
    def _():
        for c in copies(se_ref[s], sj_ref[s]):
            c.wait()
        cast()

        @pl.when(hn_ref[s] == 1)
        def _():
            for c in copies(ne_ref[s], nj_ref[s]):
                c.start()


def _ffn1_kernel(sb_ref, sj_ref, se_ref, sfirst_ref, svalid_ref, ne_ref, nj_ref, hn_ref,
                 x_ref, w_hbm, bg_ref, bl_ref, o_ref, stg_g, stg_l, wg_sc, wl_sc, sem):
    s = pl.program_id(0)
    tn = wg_sc.shape[1]
    ff = w_hbm.shape[2] // 2

    def copies(e, j):
        col = pl.multiple_of(j * tn, tn)
        return (pltpu.make_async_copy(w_hbm.at[e, :, pl.ds(col, tn)], stg_g, sem.at[0]),
                pltpu.make_async_copy(w_hbm.at[e, :, pl.ds(ff + col, tn)], stg_l, sem.at[1]))

    def cast():
        wg_sc[...] = stg_g[...].astype(BF16)
        wl_sc[...] = stg_l[...].astype(BF16)

    _weight_group_prefetch(s, se_ref, sj_ref, sfirst_ref, ne_ref, nj_ref, hn_ref, copies, cast)

    @pl.when(svalid_ref[s] == 1)
    def _():
        x_lo, x_hi = _unpack_bf16_pairs(x_ref[...])
        x_lo = x_lo.astype(BF16)
        x_hi = x_hi.astype(BF16)
        half = x_lo.shape[-1]

        def proj(w_sc, b_ref):
            return (jnp.dot(x_lo, w_sc[0:half, :], preferred_element_type=F32)
                    + jnp.dot(x_hi, w_sc[half:, :], preferred_element_type=F32) + b_ref[0])

        a_glu = jnp.minimum(proj(wg_sc, bg_ref), SWIGLU_LIMIT)
        a_lin = jnp.clip(proj(wl_sc, bl_ref), -SWIGLU_LIMIT, SWIGLU_LIMIT)
        act = a_glu * jax.nn.sigmoid(SWIGLU_ALPHA * a_glu) * (a_lin + 1.0)
        o_ref[...] = act.astype(o_ref.dtype)

    @pl.when(svalid_ref[s] == 0)
    def _():
        o_ref[...] = jnp.zeros(o_ref.shape, o_ref.dtype)


def _ffn2_kernel(sb_ref, sj_ref, se_ref, sfirst_ref, svalid_ref, ne_ref, nj_ref, hn_ref,
                 h_ref, w_hbm, b_ref, o_ref, stg, w_sc, sem):
    s = pl.program_id(0)
    tn = w_sc.shape[1]

    def copies(e, j):
        col = pl.multiple_of(j * tn, tn)
        return (pltpu.make_async_copy(w_hbm.at[e, :, pl.ds(col, tn)], stg, sem.at[0]),)

    def cast():
        w_sc[...] = stg[...].astype(BF16)

    _weight_group_prefetch(s, se_ref, sj_ref, sfirst_ref, ne_ref, nj_ref, hn_ref, copies, cast)

    @pl.when(svalid_ref[s] == 1)
    def _():
        y = jnp.dot(h_ref[...], w_sc[...], preferred_element_type=F32) + b_ref[0]
        half = y.shape[-1] // 2
        o_ref[...] = _pack_bf16_pairs(y[:, :half], y[:, half:])

    @pl.when(svalid_ref[s] == 0)
    def _():
        o_ref[...] = jnp.zeros(o_ref.shape, o_ref.dtype)


def _step_tables(nb, n_tiles, n_blocks):
    n_steps = n_blocks * n_tiles
    per_e = nb * n_tiles
    cum_end = jnp.cumsum(per_e)
    total = cum_end[-1]
    used_blocks = jnp.sum(nb)
    blk_start = jnp.cumsum(nb) - nb
    s_raw = jnp.arange(n_steps, dtype=jnp.int32)
    valid = s_raw < total
    s = jnp.minimum(s_raw, total - 1)
    e = jnp.minimum(jnp.sum((s[:, None] >= cum_end[None, :]).astype(jnp.int32), axis=1), N_EXPERTS - 1)
    r = s - (cum_end[e] - per_e[e])
    nbe = jnp.maximum(nb[e], 1)
    j = r // nbe
    bi = r % nbe
    first = jnp.logical_and(bi == 0, valid)
    spare = jnp.maximum(n_blocks - used_blocks, 1)
    r_pad = jnp.maximum(s_raw - total, 0)
    blk = jnp.where(valid, blk_start[e] + bi, used_blocks + r_pad % spare)
    j = jnp.where(valid, j, r_pad // spare)
    nxt = s_raw + nb[e]
    has_next = jnp.logical_and(first, nxt < total)
    nxt = jnp.minimum(nxt, n_steps - 1)
    i32 = lambda a: a.astype(jnp.int32)
    return i32(blk), i32(j), i32(e), i32(first), i32(valid), i32(e[nxt]), i32(j[nxt]), i32(has_next)


def _expert_ffn(xs, nb, w1, b1, w2, b2):
    p = xs.shape[0]
    d = 2 * xs.shape[1]
    ff = w2.shape[1]
    n_blocks = p // MOE_BLK
    nj1 = ff // FFN1_TN
    nj2 = d // FFN2_TN
    b1r = b1.reshape(N_EXPERTS, 1, 2 * ff)
    b2r = b2.reshape(N_EXPERTS, 1, d)

    t1 = _step_tables(nb, nj1, n_blocks)
    h = pl.pallas_call(
        _ffn1_kernel,
        grid_spec=pltpu.PrefetchScalarGridSpec(
            num_scalar_prefetch=len(t1),
            grid=(n_blocks * nj1,),
            in_specs=[
                pl.BlockSpec((MOE_BLK, d // 2), lambda s, sb, sj, se, *_: (sb[s], 0)),
                pl.BlockSpec(memory_space=pl.ANY),
                pl.BlockSpec((1, 1, FFN1_TN), lambda s, sb, sj, se, *_: (se[s], 0, sj[s])),
                pl.BlockSpec((1, 1, FFN1_TN), lambda s, sb, sj, se, *_: (se[s], 0, nj1 + sj[s])),
            ],
            out_specs=pl.BlockSpec((MOE_BLK, FFN1_TN), lambda s, sb, sj, se, *_: (sb[s], sj[s])),
            scratch_shapes=[pltpu.VMEM((d, FFN1_TN), F32), pltpu.VMEM((d, FFN1_TN), F32),
                            pltpu.VMEM((d, FFN1_TN), BF16), pltpu.VMEM((d, FFN1_TN), BF16),
                            pltpu.SemaphoreType.DMA((2,))],
        ),
        out_shape=jax.ShapeDtypeStruct((p, ff), BF16),
        compiler_params=_cparams(("arbitrary",)),
        name="expert_ffn_up",
    )(*t1, xs, w1, b1r, b1r)

    t2 = _step_tables(nb, nj2, n_blocks)
    ys = pl.pallas_call(
        _ffn2_kernel,
        grid_spec=pltpu.PrefetchScalarGridSpec(
            num_scalar_prefetch=len(t2),
            grid=(n_blocks * nj2,),
            in_specs=[
                pl.BlockSpec((MOE_BLK, ff), lambda s, sb, sj, se, *_: (sb[s], 0)),
                pl.BlockSpec(memory_space=pl.ANY),
                pl.BlockSpec((1, 1, FFN2_TN), lambda s, sb, sj, se, *_: (se[s], 0, sj[s])),
            ],
            out_specs=pl.BlockSpec((MOE_BLK, FFN2_TN // 2), lambda s, sb, sj, se, *_: (sb[s], sj[s])),
            scratch_shapes=[pltpu.VMEM((ff, FFN2_TN), F32), pltpu.VMEM((ff, FFN2_TN), BF16),
                            pltpu.SemaphoreType.DMA((1,))],
        ),
        out_shape=jax.ShapeDtypeStruct((p, d // 2), jnp.uint32),
        compiler_params=_cparams(("arbitrary",)),
        name="expert_ffn_down",
    )(*t2, h, w2, b2r)
    return ys


def _dispatch_kernel(dest_ref, h_ref, xs_in_ref, xs_ref, sem):
    del xs_in_ref
    tm = h_ref.shape[0]
    base = pl.program_id(0) * (tm * TOP_K)

    def row_copy(r, kk):
        dst = dest_ref[base + r * TOP_K + kk]
        return pltpu.make_async_copy(h_ref.at[pl.ds(r, 1), :], xs_ref.at[pl.ds(dst, 1), :], sem)

    def issue(r, carry):
        for kk in range(TOP_K):
            row_copy(r, kk).start(priority=kk % 2)
        return carry

    lax.fori_loop(0, tm, issue, 0, unroll=DMA_UNROLL)

    def drain(r, carry):
        for kk in range(TOP_K):
            row_copy(r, kk).wait()
        return carry

    lax.fori_loop(0, tm, drain, 0, unroll=DMA_UNROLL)


def _dispatch(h2p, dest_flat, p, tm):
    t, w = h2p.shape
    xs0 = jnp.zeros((p, w), h2p.dtype)
    return pl.pallas_call(
        _dispatch_kernel,
        grid_spec=pltpu.PrefetchScalarGridSpec(
            num_scalar_prefetch=1,
            grid=(t // tm,),
            in_specs=[pl.BlockSpec((tm, w), lambda i, dest: (i, 0)),
                      pl.BlockSpec(memory_space=pl.ANY)],
            out_specs=pl.BlockSpec(memory_space=pl.ANY),
            scratch_shapes=[pltpu.SemaphoreType.DMA(())],
        ),
        out_shape=jax.ShapeDtypeStruct((p, w), h2p.dtype),
        input_output_aliases={2: 0},
        compiler_params=_cparams(("arbitrary",)),
        name="moe_dispatch",
    )(dest_flat, h2p, xs0)


def _combine_kernel(dest_ref, x1_ref, gate_ref, g_ref, ys_ref, o_ref, buf, sem):
    i = pl.program_id(0)
    n = pl.num_programs(0)
    tm = x1_ref.shape[0]
    tn2 = FFN2_TN // 2

    def row_copy(blk, slot, r, kk):
        src = dest_ref[(blk * tm + r) * TOP_K + kk]
        return pltpu.make_async_copy(ys_ref.at[pl.ds(src, 1), :], buf.at[slot, pl.ds(kk * tm + r, 1), :],
                                     sem.at[slot])

    def issue_block(blk, slot):
        def body(r, carry):
            for kk in range(TOP_K):
                row_copy(blk, slot, r, kk).start(priority=kk % 2)
            return carry
        lax.fori_loop(0, tm, body, 0, unroll=DMA_UNROLL)

    @pl.when(i == 0)
    def _():
        issue_block(0, 0)

    @pl.when(i + 1 < n)
    def _():
        issue_block(i + 1, (i + 1) % 2)

    slot = i % 2

    def drain(r, carry):
        for kk in range(TOP_K):
            row_copy(i, slot, r, kk).wait()
        return carry

    lax.fori_loop(0, tm, drain, 0, unroll=DMA_UNROLL)

    gates = gate_ref[...]
    pieces = [None] * (2 * (2 * buf.shape[-1] // FFN2_TN))
    for kk in range(TOP_K):
        lo, hi = _unpack_bf16_pairs(buf[slot, kk * tm:(kk + 1) * tm, :])
        g = gates[:, kk:kk + 1]
        for j in range(len(pieces) // 2):
            for half, part in enumerate((lo, hi)):
                term = g * part[:, j * tn2:(j + 1) * tn2]
                idx = 2 * j + half
                pieces[idx] = term if pieces[idx] is None else pieces[idx] + term
    x = x1_ref[...] + jnp.concatenate(pieces, axis=-1)
    ms = jnp.mean(x * x, axis=-1, keepdims=True)
    o_ref[...] = (x * lax.rsqrt(ms + NORM_EPS) * g_ref[...]).astype(o_ref.dtype)


def _combine(x1, gates, dest_flat, ys, g_final, tm):
    t, d = x1.shape
    w = ys.shape[1]
    return pl.pallas_call(
        _combine_kernel,
        grid_spec=pltpu.PrefetchScalarGridSpec(
            num_scalar_prefetch=1,
            grid=(t // tm,),
            in_specs=[pl.BlockSpec((tm, d), lambda i, dest: (i, 0)),
                      pl.BlockSpec((tm, gates.shape[1]), lambda i, dest: (i, 0)),
                      pl.BlockSpec((1, d), lambda i, dest: (0, 0)),
                      pl.BlockSpec(memory_space=pl.ANY)],
            out_specs=pl.BlockSpec((tm, d), lambda i, dest: (i, 0)),
            scratch_shapes=[pltpu.VMEM((2, TOP_K * tm, w), ys.dtype), pltpu.SemaphoreType.DMA((2,))],
        ),
        out_shape=jax.ShapeDtypeStruct((t, d), F32),
        compiler_params=_cparams(("arbitrary",)),
        name="moe_combine_final_norm",
    )(dest_flat, x1, gates, g_final.reshape(1, d).astype(F32), ys)


def kernel(x, mem, g_mix, w_in, b_gate, rel_bias, lambda_q1, lambda_k1, lambda_q2, lambda_k2, diff_subln_g, g_mem, w_mem_kv, w_br_diff, w_br_sb, w_br_mem, w_out, g_ffn, w_router, b_router, w_exp1, b_exp1, w_exp2, b_exp2, g_final):
    b, s, d = x.shape
    t = b * s
    depth = g_mix.shape[0]
    dqk_w = DIFF_HEADS * 2 * DIFF_QK_DIM
    dv_w = DIFF_HEADS * DIFF_V_DIM
    sb_w = SB_HEADS * SB_DIM
    mem_w = MEM_HEADS * MEM_DIM
    qkv_w = 2 * dqk_w + dv_w + 3 * sb_w + mem_w
    mlen = mem.shape[1]
    tm = min(ROW_TILE, s)

    xf = x.reshape(t, d)
    for l in range(depth):
        w_in_b = w_in[l].astype(BF16)
        hmix = _rmsnorm(xf, g_mix[l], BF16, tm)
        c0 = 0
        dq = _matmul_heads(hmix, w_in_b, c0, b, s, DIFF_HEADS, 2 * DIFF_QK_DIM, _HEADS_T, name="in_proj_dq")
        c0 += dqk_w
        dk = _matmul_heads(hmix, w_in_b, c0, b, s, DIFF_HEADS, 2 * DIFF_QK_DIM, _HEADS, name="in_proj_dk")
        c0 += dqk_w
        dvv = _matmul_heads(hmix, w_in_b, c0, b, s, DIFF_HEADS, DIFF_V_DIM, _HEADS_T_TILED, tk=DIFF_TILE,
                            name="in_proj_dv")
        c0 += dv_w
        sq = _matmul_heads(hmix, w_in_b, c0, b, s, SB_HEADS, SB_DIM, _HEADS_T, name="in_proj_sq")
        c0 += sb_w
        sk = _matmul_heads(hmix, w_in_b, c0, b, s, SB_HEADS, SB_DIM, _HEADS, name="in_proj_sk")
        c0 += sb_w
        sv = _matmul_heads(hmix, w_in_b, c0, b, s, SB_HEADS, SB_DIM, _HEADS_T_TILED, tk=SB_GROUP_KEYS,
                           name="in_proj_sv")
        c0 += sb_w
        mq = _matmul(hmix, w_in_b, c0, mem_w, name="in_proj_mq")
        gates = _matmul(hmix, w_in_b, qkv_w, 3 * d, bias=b_gate[l], name="in_proj_gates")

        lam_init = 0.8 - 0.6 * math.exp(-0.3 * l)
        lam = (jnp.exp(jnp.sum(lambda_q1[l].astype(F32) * lambda_k1[l].astype(F32)))
               - jnp.exp(jnp.sum(lambda_q2[l].astype(F32) * lambda_k2[l].astype(F32))) + lam_init)
        ya = _diff_attention(dq, dk, dvv, rel_bias, lam, diff_subln_g[l], lam_init).reshape(t, dv_w)
        yb = _sb_attention(sq, sk, sv).reshape(t, sb_w)

        hmem = _rmsnorm(mem.reshape(b * mlen, d), g_mem[l], BF16, min(ROW_TILE, b * mlen))
        kv = _matmul(hmem, w_mem_kv[l].astype(BF16), 0, 2 * mem_w, name="mem_kv_proj")
        yc = _mem_attention(mq.reshape(b, s, mem_w), kv.reshape(b, mlen, 2 * mem_w), tm).reshape(t, mem_w)

        merged = _merge(ya, yb, yc, w_br_diff[l].astype(BF16), w_br_sb[l].astype(BF16),
                        w_br_mem[l].astype(BF16), gates, d, tm, MM_TN)

        wr = jnp.pad(w_router[l].astype(F32), ((0, 0), (0, ROUTER_PAD - N_EXPERTS)))
        wr_hi = wr.astype(BF16)
        wr_lo = (wr - wr_hi.astype(F32)).astype(BF16)
        br_pad = jnp.pad(b_router[l].astype(F32), (0, ROUTER_PAD - N_EXPERTS),
                         constant_values=-jnp.inf).reshape(1, ROUTER_PAD)
        x1, h2, logits = _outproj(merged, w_out[l].astype(BF16), xf, g_ffn[l].reshape(1, d).astype(F32),
                                  wr_hi, wr_lo, br_pad, tm)
        top_i, top_g, rank, counts = _topk(logits, tm)

        n = t * TOP_K
        counts = counts[0, :N_EXPERTS]
        nb = (counts + MOE_BLK - 1) // MOE_BLK
        pad_start = (jnp.cumsum(nb) - nb) * MOE_BLK
        dest = (pad_start[top_i[:, :TOP_K]] + rank[:, :TOP_K]).reshape(n)
        n_blocks = n // MOE_BLK + N_EXPERTS
        p = n_blocks * MOE_BLK

        xs = _dispatch(h2, dest, p, min(DMA_ROWS, s))
        ys = _expert_ffn(xs, nb, w_exp1[l], b_exp1[l], w_exp2[l], b_exp2[l])
        assert l + 1 == depth, "combine is fused with the final norm: single-layer trunk"
        return _combine(x1, top_g, dest, ys, g_final, min(DMA_ROWS, s)).reshape(b, s, d)
```
